```python
import math
import jax, jax.numpy as jnp
from jax import lax
import numpy as np

D_MODEL = 1024
BATCH = 8
SEQ = 4096
DEPTH = 2

HEAD_DIM = 64
N_HEADS = D_MODEL // HEAD_DIM
N_HEADS_DIL = (3 * N_HEADS) // 4
N_HEADS_FOX = N_HEADS - N_HEADS_DIL
D_DIL = N_HEADS_DIL * HEAD_DIM
D_FOX = N_HEADS_FOX * HEAD_DIM
D_MIX = D_DIL + D_FOX
D_PROJ = 3 * D_DIL + 3 * D_FOX + N_HEADS_FOX
DILATED_PATTERNS = ((128, 1), (512, 4), (2048, 16))
BLOCK = 128
ROPE_THETA = 10000.0
N_EXPERTS = 16
N_GROUPS = 4
EXPERTS_PER_GROUP = N_EXPERTS // N_GROUPS
TOP_K = 2
D_FF_EXPERT = D_MODEL
DEEPNORM_ALPHA = (2.0 * DEPTH) ** 0.25
DEEPNORM_BETA = (8.0 * DEPTH) ** -0.25
LN_EPS = 1e-5
RMS_EPS = 1e-6

kernel_name = 'hymba_longnet_fox_grouped_moe_deepnorm'


def layer_norm(x, g, b):
    xf = x.astype(jnp.float32)
    mu = jnp.mean(xf, axis=-1, keepdims=True)
    var = jnp.mean(jnp.square(xf - mu), axis=-1, keepdims=True)
    return ((xf - mu) * lax.rsqrt(var + LN_EPS) * g + b).astype(x.dtype)


def rms_norm(x, g):
    xf = x.astype(jnp.float32)
    ms = jnp.mean(jnp.square(xf), axis=-1, keepdims=True)
    return (xf * lax.rsqrt(ms + RMS_EPS) * g).astype(x.dtype)


def rope(t, pos):
    half = HEAD_DIM // 2
    inv = ROPE_THETA ** (-jnp.arange(half, dtype=jnp.float32) / half)
    ang = pos[:, None] * inv[None, :]
    cos, sin = jnp.cos(ang), jnp.sin(ang)
    t1 = t[..., :half].astype(jnp.float32)
    t2 = t[..., half:].astype(jnp.float32)
    return jnp.concatenate([t1 * cos - t2 * sin, t2 * cos + t1 * sin], axis=-1).astype(t.dtype)


def dilated_branch(q, k, v, window, dil):
    B, H, S, hd = q.shape
    L = S // dil
    steps = window // dil
    nb = -(-L // BLOCK)
    Lp = nb * BLOCK

    def fold(t):
        t = t.reshape(B, H, L, dil, hd).transpose(0, 1, 3, 2, 4)
        t = jnp.pad(t, ((0, 0), (0, 0), (0, 0), (0, Lp - L), (0, 0)))
        return t.reshape(B, H, dil, nb, BLOCK, hd)

    def with_prev(t):
        prev = jnp.pad(t[:, :, :, :-1], ((0, 0), (0, 0), (0, 0), (1, 0), (0, 0), (0, 0)))
        return jnp.concatenate([prev, t], axis=4)

    qb = fold(q)
    kw = with_prev(fold(k))
    vw = with_prev(fold(v))
    s = jnp.einsum('bhrnqd,bhrnkd->bhrnqk', qb, kw).astype(jnp.float32) * (hd ** -0.5)
    qi = jnp.arange(BLOCK)[:, None]
    ki = jnp.arange(2 * BLOCK)[None, :]
    delta = BLOCK + qi - ki
    kidx = jnp.arange(nb)[:, None, None] * BLOCK - BLOCK + ki
    valid = (delta >= 0) & (delta <= steps) & (kidx >= 0)
    s = jnp.where(valid, s, -jnp.inf)
    m = jnp.max(s, axis=-1)
    p = jnp.exp(s - m[..., None])
    l = jnp.sum(p, axis=-1)
    o = jnp.einsum('bhrnqk,bhrnkd->bhrnqd', p.astype(v.dtype), vw).astype(jnp.float32) / l[..., None]

    def unfold(t):
        t = t.reshape((B, H, dil, Lp) + t.shape[5:])[:, :, :, :L]
        t = jnp.swapaxes(t, 2, 3)
        return t.reshape((B, H, S) + t.shape[4:])

    return unfold(o), unfold(m), unfold(l)


def dilated_attention(q, k, v):
    res = [dilated_branch(q, k, v, w, r) for (w, r) in DILATED_PATTERNS]
    o_all = jnp.stack([r_[0] for r_ in res])
    m_all = jnp.stack([r_[1] for r_ in res])
    l_all = jnp.stack([r_[2] for r_ in res])
    wts = l_all * jnp.exp(m_all - jnp.max(m_all, axis=0, keepdims=True))
    wts = wts / jnp.sum(wts, axis=0, keepdims=True)
    return jnp.sum(o_all * wts[..., None], axis=0).astype(q.dtype)


def forgetting_attention(q, k, v, logf):
    B, H, S, hd = q.shape
    nb = S // BLOCK
    c = jnp.cumsum(logf.astype(jnp.float32), axis=-1)
    qb = q.reshape(B, H, nb, BLOCK, hd).transpose(2, 0, 1, 3, 4)
    cq = c.reshape(B, H, nb, BLOCK).transpose(2, 0, 1, 3)
    kpos = jnp.arange(S)

    def block(args):
        q_i, c_i, i = args
        s = jnp.einsum('bhqd,bhkd->bhqk', q_i, k).astype(jnp.float32) * (hd ** -0.5)
        s = s + c_i[..., :, None] - c[:, :, None, :]
        qpos = i * BLOCK + jnp.arange(BLOCK)
        s = jnp.where(kpos[None, :] <= qpos[:, None], s, -jnp.inf)
        p = jax.nn.softmax(s, axis=-1)
        return jnp.einsum('bhqk,bhkd->bhqd', p.astype(v.dtype), v)

    o = lax.map(block, (qb, cq, jnp.arange(nb)))
    return o.transpose(1, 2, 0, 3, 4).reshape(B, H, S, hd)


def hybrid_mixer(h, w_in, b_forget, g_dil, g_fox, w_out):
    B, S, _ = h.shape
    z = h @ w_in
    cuts = np.cumsum([D_DIL, D_DIL, D_DIL, D_FOX, D_FOX, D_FOX]).tolist()
    qa, ka, va, qf, kf, vf, zf = jnp.split(z, cuts, axis=-1)

    def heads(t, n):
        return t.reshape(B, S, n, HEAD_DIM).transpose(0, 2, 1, 3)

    def merge(t):
        return t.transpose(0, 2, 1, 3).reshape(B, S, -1)

    pos = jnp.arange(S, dtype=jnp.float32)
    o_dil = dilated_attention(rope(heads(qa, N_HEADS_DIL), pos),
                              rope(heads(ka, N_HEADS_DIL), pos),
                              heads(va, N_HEADS_DIL))
    logf = jax.nn.log_sigmoid((zf + b_forget).astype(jnp.float32)).transpose(0, 2, 1)
    o_fox = forgetting_attention(heads(qf, N_HEADS_FOX), heads(kf, N_HEADS_FOX),
                                 heads(vf, N_HEADS_FOX), logf)
    y = jnp.concatenate([rms_norm(merge(o_dil), g_dil), rms_norm(merge(o_fox), g_fox)], axis=-1)
    return y @ w_out


def grouped_moe(h, w_router, b_router, w_gate, w_up, w_down):
    B, S, D = h.shape
    xt = h.reshape(-1, D)
    T = xt.shape[0]
    logits = (xt @ w_router).astype(jnp.float32) + b_router.astype(jnp.float32)
    probs = jax.nn.softmax(logits, axis=-1)
    grp = probs.reshape(T, N_GROUPS, EXPERTS_PER_GROUP)
    gscore = jnp.sum(lax.top_k(grp, TOP_K)[0], axis=-1)
    gsel = jnp.argmax(gscore, axis=-1)
    in_grp = jnp.take_along_axis(grp, gsel[:, None, None], axis=1)[:, 0]
    gv, gi = lax.top_k(in_grp, TOP_K)
    experts = gsel[:, None] * EXPERTS_PER_GROUP + gi
    gates = gv / jnp.sum(gv, axis=-1, keepdims=True)

    TK = T * TOP_K
    flat_e = experts.reshape(-1).astype(jnp.int32)
    flat_t = jnp.repeat(jnp.arange(T, dtype=jnp.int32), TOP_K)
    flat_g = gates.reshape(-1)
    order = jnp.argsort(flat_e, stable=True)
    se = flat_e[order]
    counts = jnp.bincount(flat_e, length=N_EXPERTS)
    pcounts = (counts + BLOCK - 1) // BLOCK * BLOCK
    pends = jnp.cumsum(pcounts)
    pstarts = pends - pcounts
    starts = jnp.cumsum(counts) - counts
    dest = pstarts[se] + (jnp.arange(TK) - starts[se])
    NB = -(-TK // BLOCK) + N_EXPERTS
    rows = NB * BLOCK
    row_tok = jnp.full((rows,), T, jnp.int32).at[dest].set(flat_t[order])
    row_gate = jnp.zeros((rows,), jnp.float32).at[dest].set(flat_g[order])
    blk_e = jnp.minimum(jnp.searchsorted(pends, jnp.arange(NB) * BLOCK, side='right'), N_EXPERTS - 1)
    xpad = jnp.concatenate([xt, jnp.zeros((1, D), xt.dtype)], axis=0)
    xb = xpad[row_tok].reshape(NB, BLOCK, D)

    def expert_block(args):
        xblk, e = args
        a = jax.nn.silu(xblk @ w_gate[e]) * (xblk @ w_up[e])
        return a @ w_down[e]

    yb = lax.map(expert_block, (xb, blk_e)).reshape(rows, D).astype(jnp.float32)
    out = jnp.zeros((T + 1, D), jnp.float32).at[row_tok].add(yb * row_gate[:, None])[:T]
    return out.reshape(B, S, D).astype(h.dtype)


def setup_inputs(seed: int = 0) -> dict:
    key = jax.random.key(seed)
    ks = jax.random.split(key, 16)
    f32 = jnp.float32

    def nrm(k, shape, scale):
        return jax.random.normal(k, shape, f32) * scale

    x = jax.random.normal(ks[0], (BATCH, SEQ, D_MODEL), f32)
    col_scale = jnp.concatenate([
        jnp.ones((2 * D_DIL,), f32), jnp.full((D_DIL,), DEEPNORM_BETA, f32),
        jnp.ones((2 * D_FOX,), f32), jnp.full((D_FOX,), DEEPNORM_BETA, f32),
        jnp.ones((N_HEADS_FOX,), f32)])
    w_in = nrm(ks[1], (DEPTH, D_MODEL, D_PROJ), D_MODEL ** -0.5) * col_scale
    b_forget = 3.0 + nrm(ks[2], (DEPTH, N_HEADS_FOX), 0.5)
    g_dil = 1.0 + nrm(ks[3], (DEPTH, D_DIL), 0.02)
    g_fox = 1.0 + nrm(ks[4], (DEPTH, D_FOX), 0.02)
    w_out = nrm(ks[5], (DEPTH, D_MIX, D_MODEL), D_MIX ** -0.5 * DEEPNORM_BETA)
    ln1_g = 1.0 + nrm(ks[6], (DEPTH, D_MODEL), 0.02)
    ln1_b = nrm(ks[7], (DEPTH, D_MODEL), 0.02)
    w_router = nrm(ks[8], (D_MODEL, N_EXPERTS), D_MODEL ** -0.5)
    b_router = nrm(ks[9], (N_EXPERTS,), 0.01)
    w_gate = nrm(ks[10], (DEPTH, N_EXPERTS, D_MODEL, D_FF_EXPERT), D_MODEL ** -0.5)
    w_up = nrm(ks[11], (DEPTH, N_EXPERTS, D_MODEL, D_FF_EXPERT), D_MODEL ** -0.5)
    w_down = nrm(ks[12], (DEPTH, N_EXPERTS, D_FF_EXPERT, D_MODEL), D_FF_EXPERT ** -0.5 * DEEPNORM_BETA)
    ln2_g = 1.0 + nrm(ks[13], (DEPTH, D_MODEL), 0.02)
    ln2_b = nrm(ks[14], (DEPTH, D_MODEL), 0.02)
    return {'x': x, 'w_in': w_in, 'b_forget': b_forget, 'g_dil': g_dil, 'g_fox': g_fox,
            'w_out': w_out, 'ln1_g': ln1_g, 'ln1_b': ln1_b, 'w_router': w_router,
            'b_router': b_router, 'w_gate': w_gate, 'w_up': w_up, 'w_down': w_down,
            'ln2_g': ln2_g, 'ln2_b': ln2_b}


def reference(x, w_in, b_forget, g_dil, g_fox, w_out, ln1_g, ln1_b, w_router,
              b_router, w_gate, w_up, w_down, ln2_g, ln2_b):
    h = x
    for l in range(DEPTH):
        y = hybrid_mixer(h, w_in[l], b_forget[l], g_dil[l], g_fox[l], w_out[l])
        h = layer_norm(DEEPNORM_ALPHA * h + y, ln1_g[l], ln1_b[l])
        y = grouped_moe(h, w_router, b_router, w_gate[l], w_up[l], w_down[l])
        h = layer_norm(DEEPNORM_ALPHA * h + y, ln2_g[l], ln2_b[l])
    return h
```

```python
import functools

import numpy as np
import jax
import jax.numpy as jnp
from jax import lax
from jax.experimental import pallas as pl
from jax.experimental.pallas import tpu as pltpu

D_MODEL = 1024
DEPTH = 2
HEAD_DIM = 64
N_HEADS_DIL = 12
N_HEADS_FOX = 4
D_DIL = N_HEADS_DIL * HEAD_DIM
D_FOX = N_HEADS_FOX * HEAD_DIM
DILATIONS = (1, 4, 16)
BLOCK = 128
ROPE_THETA = 10000.0
N_EXPERTS = 16
N_GROUPS = 4
EXPERTS_PER_GROUP = 4
TOP_K = 2
DEEPNORM_ALPHA = (2.0 * DEPTH) ** 0.25
LN_EPS = 1e-5
RMS_EPS = 1e-6

LANES = 128
N_PAIRS_DIL = D_DIL // LANES
D_PROJ_PAD = 3 * D_DIL + 3 * D_FOX + LANES
VMEM_LIMIT = 48 * 1024 * 1024

MXU_DTYPE = jnp.bfloat16
F32 = jnp.float32
NEG_INF = float("-inf")


def _cparams(n_axes):
    return pltpu.CompilerParams(dimension_semantics=("arbitrary",) * n_axes,
                                vmem_limit_bytes=VMEM_LIMIT)


def _dot(a, b):
    return jnp.dot(a, b, preferred_element_type=F32)


def _dot_nt(a, b):
    return lax.dot_general(a, b, (((1,), (1,)), ((), ())), preferred_element_type=F32)


def _inproj_kernel(x_ref, w_ref, cos_ref, sin_ref,
                   qd_ref, kd_ref, vd_ref, qf_ref, kf_ref, vf_ref, zf_ref):
    xb = x_ref[0].astype(MXU_DTYPE)
    tm = xb.shape[0]
    cos = cos_ref[...]
    sin = sin_ref[...]
    lane = lax.broadcasted_iota(jnp.int32, (tm, LANES), 1)
    first_half = (lane % HEAD_DIM) < (HEAD_DIM // 2)

    def rope(z):
        rot = jnp.where(first_half, pltpu.roll(z, LANES - 32, 1), pltpu.roll(z, 32, 1))
        return z * cos + rot * sin

    def proj(col, width):
        return _dot(xb, w_ref[:, col:col + width])

    scale = HEAD_DIM ** -0.5
    col = 0
    for out_ref, roped, mul in ((qd_ref, True, scale), (kd_ref, True, 1.0)):
        for c in range(D_DIL // 256):
            z = proj(col, 256)
            for half in range(2):
                zz = rope(z[:, half * LANES:(half + 1) * LANES]) * mul
                out_ref[0, :, c * 256 + half * LANES:c * 256 + (half + 1) * LANES] = zz.astype(out_ref.dtype)
            col += 256
    for c in range(D_DIL // 256):
        vd_ref[0, :, c * 256:(c + 1) * 256] = proj(col, 256).astype(vd_ref.dtype)
        col += 256
    qf_ref[0] = (proj(col, D_FOX) * scale).astype(qf_ref.dtype)
    col += D_FOX
    kf_ref[0] = proj(col, D_FOX).astype(kf_ref.dtype)
    col += D_FOX
    vf_ref[0] = proj(col, D_FOX).astype(vf_ref.dtype)
    col += D_FOX
    zf_ref[0] = proj(col, LANES)


def _in_projection(h, w_pad, cos_tab, sin_tab, tm=512):
    B, S, D = h.shape
    grid = (B, S // tm)
    row = lambda width: pl.BlockSpec((1, tm, width), lambda b, i: (b, i, 0))
    out_shape = (
        jax.ShapeDtypeStruct((B, S, D_DIL), MXU_DTYPE),
        jax.ShapeDtypeStruct((B, S, D_DIL), MXU_DTYPE),
        jax.ShapeDtypeStruct((B, S, D_DIL), MXU_DTYPE),
        jax.ShapeDtypeStruct((B, S, D_FOX), MXU_DTYPE),
        jax.ShapeDtypeStruct((B, S, D_FOX), MXU_DTYPE),
        jax.ShapeDtypeStruct((B, S, D_FOX), MXU_DTYPE),
        jax.ShapeDtypeStruct((B, S, LANES), F32),
    )
    return pl.pallas_call(
        _inproj_kernel,
        grid=grid,
        in_specs=[
            row(D),
            pl.BlockSpec((D, D_PROJ_PAD), lambda b, i: (0, 0)),
            pl.BlockSpec((tm, LANES), lambda b, i: (i, 0)),
            pl.BlockSpec((tm, LANES), lambda b, i: (i, 0)),
        ],
        out_specs=(row(D_DIL), row(D_DIL), row(D_DIL), row(D_FOX), row(D_FOX), row(D_FOX), row(LANES)),
        out_shape=out_shape,
        compiler_params=_cparams(2),
        name="in_projection",
    )(h, w_pad, cos_tab, sin_tab)


def _split3(x):
    hi = x.astype(jnp.bfloat16)
    r1 = x - hi.astype(F32)
    mid = r1.astype(jnp.bfloat16)
    lo = (r1 - mid.astype(F32)).astype(jnp.bfloat16)
    return hi, mid, lo


def _forget_cumsum_kernel(z_ref, b_ref, tri_ref, c_ref):
    x = z_ref[0] + b_ref[...]
    logf = jnp.minimum(x, 0.0) - jnp.log1p(jnp.exp(-jnp.abs(x)))
    tri = tri_ref[...]
    S = x.shape[1]
    carry = jnp.zeros((x.shape[0], 1), F32)
    for blk in range(S // LANES):
        seg = logf[:, blk * LANES:(blk + 1) * LANES]
        hi, mid, lo = _split3(seg)
        cs = (_dot(hi, tri) + _dot(mid, tri)) + _dot(lo, tri) + carry
        c_ref[0, :, blk * LANES:(blk + 1) * LANES] = cs
        carry = cs[:, LANES - 1:LANES]


def _forget_cumsum(zt, b_forget):
    B, Hf, S = zt.shape
    tri = (np.arange(LANES)[:, None] <= np.arange(LANES)[None, :]).astype(np.float32)
    return pl.pallas_call(
        _forget_cumsum_kernel,
        grid=(B,),
        in_specs=[
            pl.BlockSpec((1, Hf, S), lambda b: (b, 0, 0)),
            pl.BlockSpec((Hf, 1), lambda b: (0, 0)),
            pl.BlockSpec((LANES, LANES), lambda b: (0, 0)),
        ],
        out_specs=pl.BlockSpec((1, Hf, S), lambda b: (b, 0, 0)),
        out_shape=jax.ShapeDtypeStruct((B, Hf, S), F32),
        compiler_params=_cparams(1),
        name="forget_cumsum",
    )(zt, b_forget.reshape(Hf, 1).astype(F32), jnp.asarray(tri, jnp.bfloat16))


def _pair_rows(xp, is_a):
    zero = jnp.zeros_like(xp)
    return jnp.concatenate([jnp.where(is_a, xp, zero), jnp.where(is_a, zero, xp)], axis=0)


def _fox_kernel(q_ref, k_ref, v_ref, ccol_ref, crow_ref, o_ref, *, tq, tk):
    i = pl.program_id(2)
    lane = lax.broadcasted_iota(jnp.int32, (tq, LANES), 1)
    is_a = lane < HEAD_DIM
    q2 = _pair_rows(q_ref[0], is_a)
    cc = ccol_ref[0, 0]
    cq2 = jnp.concatenate([cc[:, 0:1], cc[:, 1:2]], axis=0)
    row = lax.broadcasted_iota(jnp.int32, (2 * tq, tk), 0)
    colk = lax.broadcasted_iota(jnp.int32, (2 * tq, tk), 1)
    qpos = i * tq + (row % tq)
    n_kv = (i * tq + tq + tk - 1) // tk

    def body(j, carry):
        m, l, acc = carry
        k0 = pl.multiple_of(j * tk, tk)
        kb = k_ref[0, pl.ds(k0, tk), :]
        vb = v_ref[0, pl.ds(k0, tk), :]
        s = _dot_nt(q2, kb)
        cr = crow_ref[0, 0, :, pl.ds(k0, tk)]
        ck2 = jnp.concatenate([jnp.broadcast_to(cr[0:1], (tq, tk)),
                               jnp.broadcast_to(cr[1:2], (tq, tk))], axis=0)
        s = s + (cq2 - ck2)
        s = jnp.where(k0 + colk <= qpos, s, NEG_INF)
        m_new = jnp.maximum(m, jnp.max(s, axis=1, keepdims=True))
        alpha = jnp.exp(m - m_new)
        p = jnp.exp(s - m_new)
        l_new = alpha * l + jnp.sum(p, axis=1, keepdims=True)
        acc_new = alpha * acc + _dot(p.astype(MXU_DTYPE), vb)
        return m_new, l_new, acc_new

    m0 = jnp.full((2 * tq, 1), NEG_INF, F32)
    l0 = jnp.zeros((2 * tq, 1), F32)
    a0 = jnp.zeros((2 * tq, LANES), F32)
    m, l, acc = lax.fori_loop(0, n_kv, body, (m0, l0, a0))
    o2 = acc / l
    o_ref[0] = jnp.where(is_a, o2[:tq], o2[tq:]).astype(o_ref.dtype)


def _fox_attention(qf, kf, vf, c, tq=128, tk=256):
    B, S, _ = qf.shape
    n_pairs = D_FOX // LANES
    c4 = c.reshape(B, n_pairs, 2, S)
    ccol = c4.transpose(0, 1, 3, 2)
    return pl.pallas_call(
        functools.partial(_fox_kernel, tq=tq, tk=tk),
        grid=(B, n_pairs, S // tq),
        in_specs=[
            pl.BlockSpec((1, tq, LANES), lambda b, p, i: (b, i, p)),
            pl.BlockSpec((1, S, LANES), lambda b, p, i: (b, 0, p)),
            pl.BlockSpec((1, S, LANES), lambda b, p, i: (b, 0, p)),
            pl.BlockSpec((1, 1, tq, 2), lambda b, p, i: (b, p, i, 0)),
            pl.BlockSpec((1, 1, 2, S), lambda b, p, i: (b, p, 0, 0)),
        ],
        out_specs=pl.BlockSpec((1, tq, LANES), lambda b, p, i: (b, i, p)),
        out_shape=jax.ShapeDtypeStruct((B, S, D_FOX), MXU_DTYPE),
        compiler_params=_cparams(3),
        name="fox_attention",
    )(qf, kf, vf, ccol, c4)


def _lane_pick(cols, offset):
    n = cols[0].shape[0]
    lane = lax.broadcasted_iota(jnp.int32, (n, LANES), 1)
    tile = jnp.zeros((n, LANES), F32)
    for j, cvec in enumerate(cols):
        tile = jnp.where(lane == offset + j, cvec, tile)
    return tile


def _dil_kernel(*refs, rows, merge):
    if merge:
        (q_ref, kc_ref, kp_ref, vc_ref, vp_ref, o4_ref, ml4_ref, o16_ref, ml16_ref,
         o_ref, kbuf, vbuf) = refs
        others = ((o4_ref, ml4_ref), (o16_ref, ml16_ref))
    else:
        q_ref, kc_ref, kp_ref, vc_ref, vp_ref, o_ref, ml_ref, kbuf, vbuf = refs
    chunk = pl.program_id(2)
    n_sub = rows // BLOCK
    kbuf[0:BLOCK, :] = kp_ref[0]
    kbuf[BLOCK:, :] = kc_ref[0]
    vbuf[0:BLOCK, :] = vp_ref[0]
    vbuf[BLOCK:, :] = vc_ref[0]

    lane = lax.broadcasted_iota(jnp.int32, (BLOCK, LANES), 1)
    is_a = lane < HEAD_DIM
    row = lax.broadcasted_iota(jnp.int32, (2 * BLOCK, 2 * BLOCK), 0)
    ki = lax.broadcasted_iota(jnp.int32, (2 * BLOCK, 2 * BLOCK), 1)
    delta = BLOCK + (row % BLOCK) - ki
    band = (delta >= 0) & (delta <= BLOCK)
    in_cur = ki >= BLOCK

    def sub_block(n, _):
        r0 = pl.multiple_of(n * BLOCK, BLOCK)
        has_prev = (chunk * n_sub + n) > 0
        valid = band & (in_cur | has_prev)
        m_cols, l_cols = [], []
        for hp in range(N_PAIRS_DIL):
            cs = slice(hp * LANES, (hp + 1) * LANES)
            q2 = _pair_rows(q_ref[0, pl.ds(r0, BLOCK), cs], is_a)
            kw = kbuf[pl.ds(r0, 2 * BLOCK), cs]
            vw = vbuf[pl.ds(r0, 2 * BLOCK), cs]
            s = jnp.where(valid, _dot_nt(q2, kw), NEG_INF)
            m = jnp.max(s, axis=1, keepdims=True)
            p = jnp.exp(s - m)
            l = jnp.sum(p, axis=1, keepdims=True)
            o2 = _dot(p.astype(MXU_DTYPE), vw)
            m_a, m_b = m[:BLOCK], m[BLOCK:]
            l_a, l_b = l[:BLOCK], l[BLOCK:]
            acc = jnp.where(is_a, o2[:BLOCK], o2[BLOCK:])
            if not merge:
                o_ref[0, pl.ds(r0, BLOCK), cs] = (acc / jnp.where(is_a, l_a, l_b)).astype(o_ref.dtype)
                m_cols += [m_a, m_b]
                l_cols += [l_a, l_b]
            else:
                num_a, num_b = [m_a], [m_b]
                stats = []
                for o_other, ml_other in others:
                    ml = ml_other[0, pl.ds(r0, BLOCK), :]
                    ma = ml[:, 2 * hp:2 * hp + 1]
                    mb = ml[:, 2 * hp + 1:2 * hp + 2]
                    la = ml[:, N_HEADS_DIL + 2 * hp:N_HEADS_DIL + 2 * hp + 1]
                    lb = ml[:, N_HEADS_DIL + 2 * hp + 1:N_HEADS_DIL + 2 * hp + 2]
                    stats.append((ma, mb, la, lb, o_other[0, pl.ds(r0, BLOCK), cs].astype(F32)))
                    num_a.append(ma)
                    num_b.append(mb)
                top_a = functools.reduce(jnp.maximum, num_a)
                top_b = functools.reduce(jnp.maximum, num_b)
                w_a = jnp.exp(m_a - top_a)
                w_b = jnp.exp(m_b - top_b)
                num = jnp.where(is_a, w_a, w_b) * acc
                den_a = w_a * l_a
                den_b = w_b * l_b
                for ma, mb, la, lb, o_n in stats:
                    wa = jnp.exp(ma - top_a) * la
                    wb = jnp.exp(mb - top_b) * lb
                    num = num + jnp.where(is_a, wa, wb) * o_n
                    den_a = den_a + wa
                    den_b = den_b + wb
                o_ref[0, pl.ds(r0, BLOCK), cs] = (num / jnp.where(is_a, den_a, den_b)).astype(o_ref.dtype)
        if not merge:
            ml_ref[0, pl.ds(r0, BLOCK), :] = _lane_pick(m_cols, 0) + _lane_pick(l_cols, N_HEADS_DIL)
        return 0

    lax.fori_loop(0, n_sub, sub_block, 0)


def _dilated_branch(q, k, v, dil, rows, others=None):
    B, S, _ = q.shape
    L = S // dil
    rows = min(rows, L)
    n_chunk = L // rows
    n_sub = rows // BLOCK
    qv, kv, vv = (t.reshape(B, L, dil * D_DIL) for t in (q, k, v))
    cur = pl.BlockSpec((1, rows, D_DIL), lambda b, r, c: (b, c, r))
    prev = pl.BlockSpec((1, BLOCK, D_DIL), lambda b, r, c: (b, jnp.maximum(c * n_sub - 1, 0), r))
    ml_spec = pl.BlockSpec((1, rows, LANES), lambda b, r, c: (b, c, r))
    in_specs = [cur, cur, prev, cur, prev]
    args = [qv, kv, kv, vv, vv]
    merge = others is not None
    if merge:
        assert dil == 1
        for o_n, ml_n in others:
            in_specs += [cur, ml_spec]
            args += [o_n, ml_n]
        out_specs = cur
        out_shape = jax.ShapeDtypeStruct((B, L, dil * D_DIL), MXU_DTYPE)
    else:
        out_specs = (cur, ml_spec)
        out_shape = (jax.ShapeDtypeStruct((B, L, dil * D_DIL), MXU_DTYPE),
                     jax.ShapeDtypeStruct((B, L, dil * LANES), F32))
    out = pl.pallas_call(
        functools.partial(_dil_kernel, rows=rows, merge=merge),
        grid=(B, dil, n_chunk),
        in_specs=in_specs,
        out_specs=out_specs,
        out_shape=out_shape,
        scratch_shapes=[pltpu.VMEM((rows + BLOCK, D_DIL), MXU_DTYPE),
                        pltpu.VMEM((rows + BLOCK, D_DIL), MXU_DTYPE)],
        compiler_params=_cparams(3),
        name=f"dilated_attention_r{dil}",
    )(*args)
    if merge:
        return out.reshape(B, S, D_DIL)
    return out[0].reshape(B, S, D_DIL), out[1].reshape(B, S, LANES)


def _dilated_attention(qd, kd, vd):
    o16 = _dilated_branch(qd, kd, vd, 16, 256)
    o4 = _dilated_branch(qd, kd, vd, 4, 512)
    return _dilated_branch(qd, kd, vd, 1, 512, others=(o4, o16))


def _layer_norm(u, g, b):
    mu = jnp.mean(u, axis=1, keepdims=True)
    d = u - mu
    var = jnp.mean(d * d, axis=1, keepdims=True)
    return d * lax.rsqrt(var + LN_EPS) * g + b


def _rms_norm(x, g):
    ms = jnp.mean(x * x, axis=1, keepdims=True)
    return x * lax.rsqrt(ms + RMS_EPS) * g


def _top2_of4(vals):
    v1, i1 = vals[0], jnp.zeros(vals[0].shape, jnp.int32)
    for i in range(1, 4):
        better = vals[i] > v1
        v1 = jnp.where(better, vals[i], v1)
        i1 = jnp.where(better, i, i1)
    v2 = jnp.full(vals[0].shape, -1.0, F32)
    i2 = jnp.zeros(vals[0].shape, jnp.int32)
    for i in range(4):
        better = (vals[i] > v2) & (i1 != i)
        v2 = jnp.where(better, vals[i], v2)
        i2 = jnp.where(better, i, i2)
    return v1, i1, v2, i2


def _outproj_kernel(od_ref, of_ref, h_ref, wo_ref, gd_ref, gf_ref, lng_ref, lnb_ref,
                    wr_ref, br_ref, tri_ref,
                    h1_ref, e_ref, gate_ref, rank_ref, cnt_ref, base_ref):
    step = pl.program_id(0)

    @pl.when(step == 0)
    def _():
        base_ref[...] = jnp.zeros_like(base_ref)

    xd = _rms_norm(od_ref[...].astype(F32), gd_ref[...])
    xf = _rms_norm(of_ref[...].astype(F32), gf_ref[...])
    y = _dot(xd.astype(MXU_DTYPE), wo_ref[0:D_DIL, :]) + _dot(xf.astype(MXU_DTYPE), wo_ref[D_DIL:, :])
    h1 = _layer_norm(DEEPNORM_ALPHA * h_ref[...] + y, lng_ref[...], lnb_ref[...])
    h1_ref[...] = h1
    tm = h1.shape[0]

    h_hi, h_mid, _ = _split3(h1)
    two = _dot_nt(wr_ref[...], h_hi)
    logits = (two[:N_EXPERTS] + two[N_EXPERTS:]) + _dot_nt(wr_ref[0:N_EXPERTS, :], h_mid) + br_ref[...]
    logits = logits - jnp.max(logits, axis=0, keepdims=True)
    ex = jnp.exp(logits)
    probs = ex / jnp.sum(ex, axis=0, keepdims=True)
    pr = [probs[j:j + 1, :] for j in range(N_EXPERTS)]

    def group_score(g):
        v = pr[4 * g:4 * g + 4]
        pairs = [v[a] + v[b] for a in range(4) for b in range(a + 1, 4)]
        return functools.reduce(jnp.maximum, pairs)

    best = group_score(0)
    gsel = jnp.zeros((1, tm), jnp.int32)
    for g in range(1, N_GROUPS):
        sc = group_score(g)
        better = sc > best
        best = jnp.where(better, sc, best)
        gsel = jnp.where(better, g, gsel)
    in_grp = []
    for i in range(EXPERTS_PER_GROUP):
        v = pr[i]
        for g in range(1, N_GROUPS):
            v = jnp.where(gsel == g, pr[4 * g + i], v)
        in_grp.append(v)
    v1, i1, v2, i2 = _top2_of4(in_grp)
    e1 = gsel * EXPERTS_PER_GROUP + i1
    e2 = gsel * EXPERTS_PER_GROUP + i2
    den = v1 + v2
    e_ref[...] = jnp.concatenate([e1, e2], axis=0)
    gate_ref[...] = jnp.concatenate([v1 / den, v2 / den], axis=0)

    eidx = lax.broadcasted_iota(jnp.int32, (N_EXPERTS, tm), 0)
    oh1 = (eidx == e1).astype(F32)
    oh2 = (eidx == e2).astype(F32)
    tot = oh1 + oh2
    before = base_ref[...] + _dot(tot.astype(jnp.bfloat16), tri_ref[...])
    r1 = jnp.sum(oh1 * before, axis=0, keepdims=True)
    r2 = jnp.sum(oh2 * before, axis=0, keepdims=True)
    rank_ref[...] = jnp.concatenate([r1, r2], axis=0).astype(jnp.int32)
    base_ref[...] = base_ref[...] + jnp.sum(tot, axis=1, keepdims=True)
    cnt_ref[...] = jnp.broadcast_to(base_ref[...], cnt_ref.shape)


def _out_projection(od, of, h, w_out, g_dil, g_fox, ln_g, ln_b, wr3, b_router, tm=256):
    T = h.shape[0]
    tri = (np.arange(tm)[:, None] < np.arange(tm)[None, :]).astype(np.float32)
    rows = lambda width: pl.BlockSpec((tm, width), lambda i: (i, 0))
    full = lambda a: pl.BlockSpec(a.shape, lambda i: (0,) * a.ndim)
    tok = pl.BlockSpec((TOP_K, tm), lambda i: (0, i))
    consts = [w_out, g_dil.reshape(1, -1), g_fox.reshape(1, -1), ln_g.reshape(1, -1), ln_b.reshape(1, -1),
              wr3, b_router.reshape(-1, 1).astype(F32), jnp.asarray(tri, jnp.bfloat16)]
    return pl.pallas_call(
        _outproj_kernel,
        grid=(T // tm,),
        in_specs=[rows(D_DIL), rows(D_FOX), rows(D_MODEL)] + [full(a) for a in consts],
        out_specs=(rows(D_MODEL), tok, tok, tok, pl.BlockSpec((N_EXPERTS, LANES), lambda i: (0, 0))),
        out_shape=(
            jax.ShapeDtypeStruct((T, D_MODEL), F32),
            jax.ShapeDtypeStruct((TOP_K, T), jnp.int32),
            jax.ShapeDtypeStruct((TOP_K, T), F32),
            jax.ShapeDtypeStruct((TOP_K, T), jnp.int32),
            jax.ShapeDtypeStruct((N_EXPERTS, LANES), F32),
        ),
        scratch_shapes=[pltpu.VMEM((N_EXPERTS, 1), F32)],
        compiler_params=_cparams(1),
        name="out_projection_router",
    )(od, of, h, *consts)


def _row_copy(src_hbm, src_row, dst, dst_row, sem):
    return pltpu.make_async_copy(src_hbm.at[pl.ds(src_row, 1)], dst.at[pl.ds(dst_row, 1)], sem)


def _dispatch_kernel(p0_ref, p1_ref, h_hbm, xg_in, xg_out, sem, *, td):
    del xg_in
    base = pl.program_id(0) * td

    def issue(t, _):
        _row_copy(h_hbm, base + t, xg_out, p0_ref[t], sem).start()
        _row_copy(h_hbm, base + t, xg_out, p1_ref[t], sem).start()
        return 0

    def drain(t, _):
        _row_copy(h_hbm, base + t, xg_out, p0_ref[t], sem).wait()
        _row_copy(h_hbm, base + t, xg_out, p1_ref[t], sem).wait()
        return 0

    lax.fori_loop(0, td, issue, 0)
    lax.fori_loop(0, td, drain, 0)


def _dispatch(h1, pos, n_rows, td=256):
    T, D = h1.shape
    smem = pl.BlockSpec((td,), lambda i: (i,), memory_space=pltpu.SMEM)
    anyspec = pl.BlockSpec(memory_space=pl.ANY)
    return pl.pallas_call(
        functools.partial(_dispatch_kernel, td=td),
        grid=(T // td,),
        in_specs=[smem, smem, anyspec, anyspec],
        out_specs=anyspec,
        out_shape=jax.ShapeDtypeStruct((n_rows, D), h1.dtype),
        scratch_shapes=[pltpu.SemaphoreType.DMA(())],
        input_output_aliases={3: 0},
        compiler_params=_cparams(1),
        name="moe_dispatch",
    )(pos[0], pos[1], h1, jnp.zeros((n_rows, D), h1.dtype))


def _ffn_kernel(be_ref, nv_ref, x_ref, wg_ref, wu_ref, wd_ref, y_ref):
    del be_ref
    j = pl.program_id(0)

    @pl.when(j < nv_ref[0])
    def _():
        xb = x_ref[...].astype(MXU_DTYPE)
        a = _dot(xb, wg_ref[0])
        u = _dot(xb, wu_ref[0])
        hmid = (a * jax.nn.sigmoid(a)) * u
        y_ref[...] = _dot(hmid.astype(MXU_DTYPE), wd_ref[0])

    @pl.when(j >= nv_ref[0])
    def _():
        y_ref[...] = jnp.zeros_like(y_ref)


def _expert_ffn(xg, blk_e, n_valid, wg, wu, wd, tmf):
    n_rows, D = xg.shape
    nb = n_rows // tmf
    xmap = lambda j, be, nv: (jnp.minimum(j, nv[0] - 1), 0)
    wmap = lambda j, be, nv: (be[j], 0, 0)
    grid_spec = pltpu.PrefetchScalarGridSpec(
        num_scalar_prefetch=2,
        grid=(nb,),
        in_specs=[
            pl.BlockSpec((tmf, D), xmap),
            pl.BlockSpec((1, D, D), wmap),
            pl.BlockSpec((1, D, D), wmap),
            pl.BlockSpec((1, D, D), wmap),
        ],
        out_specs=pl.BlockSpec((tmf, D), lambda j, be, nv: (j, 0)),
    )
    return pl.pallas_call(
        _ffn_kernel,
        grid_spec=grid_spec,
        out_shape=jax.ShapeDtypeStruct((n_rows, D), F32),
        compiler_params=_cparams(1),
        name="moe_expert_ffn",
    )(blk_e, n_valid, xg, wg, wu, wd)


def _combine_kernel(p0_ref, p1_ref, gate_ref, h_ref, lng_ref, lnb_ref, y_hbm, o_ref, buf0, buf1, sem, *, tc):
    def issue(t, _):
        _row_copy(y_hbm, p0_ref[t], buf0, t, sem).start()
        _row_copy(y_hbm, p1_ref[t], buf1, t, sem).start()
        return 0

    def drain(t, _):
        _row_copy(y_hbm, p0_ref[t], buf0, t, sem).wait()
        _row_copy(y_hbm, p1_ref[t], buf1, t, sem).wait()
        return 0

    lax.fori_loop(0, tc, issue, 0)
    lax.fori_loop(0, tc, drain, 0)
    g = gate_ref[...]
    y = buf0[...] * g[:, 0:1] + buf1[...] * g[:, 1:2]
    o_ref[...] = _layer_norm(DEEPNORM_ALPHA * h_ref[...] + y, lng_ref[...], lnb_ref[...])


def _combine(yg, pos, gates_col, h1, ln_g, ln_b, tc=256):
    T, D = h1.shape
    smem = pl.BlockSpec((tc,), lambda i: (i,), memory_space=pltpu.SMEM)
    rows = lambda width: pl.BlockSpec((tc, width), lambda i: (i, 0))
    vec = pl.BlockSpec((1, D), lambda i: (0, 0))
    return pl.pallas_call(
        functools.partial(_combine_kernel, tc=tc),
        grid=(T // tc,),
        in_specs=[smem, smem, rows(TOP_K), rows(D), vec, vec, pl.BlockSpec(memory_space=pl.ANY)],
        out_specs=rows(D),
        out_shape=jax.ShapeDtypeStruct((T, D), F32),
        scratch_shapes=[pltpu.VMEM((tc, D), F32), pltpu.VMEM((tc, D), F32), pltpu.SemaphoreType.DMA(())],
        compiler_params=_cparams(1),
        name="moe_combine_ln",
    )(pos[0], pos[1], gates_col, h1, ln_g.reshape(1, -1), ln_b.reshape(1, -1), yg)


def _grouped_moe(h1, experts, gates, ranks, counts, wg, wu, wd, ln_g, ln_b, tmf=256):
    T = h1.shape[0]
    nb = (T * TOP_K) // tmf + N_EXPERTS
    cnt = counts[:, 0].astype(jnp.int32)
    pcnt = (cnt + tmf - 1) // tmf * tmf
    pends = jnp.cumsum(pcnt)
    pstart = pends - pcnt
    pos = pstart[experts] + ranks
    blk_e = jnp.minimum(jnp.searchsorted(pends, jnp.arange(nb, dtype=jnp.int32) * tmf, side="right"),
                        N_EXPERTS - 1).astype(jnp.int32)
    n_valid = (pends[-1:] // tmf).astype(jnp.int32)
    xg = _dispatch(h1, pos, nb * tmf)
    yg = _expert_ffn(xg, blk_e, n_valid, wg, wu, wd, tmf)
    return _combine(yg, pos, gates.T, h1, ln_g, ln_b)


def _rope_tables(S):
    half = HEAD_DIM // 2
    inv = ROPE_THETA ** (-jnp.arange(half, dtype=F32) / half)
    ang = jnp.arange(S, dtype=F32)[:, None] * inv[None, :]
    cos, sin = jnp.cos(ang), jnp.sin(ang)
    reps = LANES // HEAD_DIM
    cos_tab = jnp.tile(jnp.concatenate([cos, cos], axis=1), (1, reps))
    sin_tab = jnp.tile(jnp.concatenate([-sin, sin], axis=1), (1, reps))
    return cos_tab, sin_tab


def _pad_w_in(w):
    pad = jnp.zeros((D_MODEL, D_PROJ_PAD - w.shape[1]), w.dtype)
    return jnp.concatenate([w, pad], axis=1).astype(MXU_DTYPE)


def kernel(x, w_in, b_forget, g_dil, g_fox, w_out, ln1_g, ln1_b, w_router, b_router,
           w_gate, w_up, w_down, ln2_g, ln2_b):
    B, S, D = x.shape
    T = B * S
    cos_tab, sin_tab = _rope_tables(S)
    wr_hi, wr_mid, _ = _split3(w_router.astype(F32).T)
    wr3 = jnp.concatenate([wr_hi, wr_mid], axis=0)
    h = x
    for l in range(DEPTH):
        qd, kd, vd, qf, kf, vf, zf = _in_projection(h, _pad_w_in(w_in[l]), cos_tab, sin_tab)
        zt = zf[:, :, :N_HEADS_FOX].transpose(0, 2, 1)
        c = _forget_cumsum(zt, b_forget[l])
        o_fox = _fox_attention(qf, kf, vf, c)
        o_dil = _dilated_attention(qd, kd, vd)
        h1, experts, gates, ranks, counts = _out_projection(
            o_dil.reshape(T, D_DIL), o_fox.reshape(T, D_FOX), h.reshape(T, D),
            w_out[l].astype(MXU_DTYPE), g_dil[l], g_fox[l], ln1_g[l], ln1_b[l], wr3, b_router)
        h2 = _grouped_moe(h1, experts, gates, ranks, counts,
                          w_gate[l].astype(MXU_DTYPE), w_up[l].astype(MXU_DTYPE), w_down[l].astype(MXU_DTYPE),
                          ln2_g[l], ln2_b[l])
        h = h2.reshape(B, S, D)
    return h
```

```python
import functools

import numpy as np
import jax
import jax.numpy as jnp
from jax import lax
from jax.experimental import pallas as pl
from jax.experimental.pallas import tpu as pltpu

D_MODEL = 1024
DEPTH = 2
HEAD_DIM = 64
N_HEADS_DIL = 12
N_HEADS_FOX = 4
D_DIL = N_HEADS_DIL * HEAD_DIM
D_FOX = N_HEADS_FOX * HEAD_DIM
DILATIONS = (1, 4, 16)
BLOCK = 128
ROPE_THETA = 10000.0
N_EXPERTS = 16
N_GROUPS = 4
EXPERTS_PER_GROUP = 4
TOP_K = 2
DEEPNORM_ALPHA = (2.0 * DEPTH) ** 0.25
LN_EPS = 1e-5
RMS_EPS = 1e-6

LANES = 128
N_PAIRS_DIL = D_DIL // LANES
D_PROJ_PAD = 3 * D_DIL + 3 * D_FOX + LANES
VMEM_LIMIT = 48 * 1024 * 1024
TOKEN_TILE_ROWS = D_MODEL // LANES
DMA_ISSUE_UNROLL = 8

MXU_DTYPE = jnp.bfloat16
F32 = jnp.float32
NEG_INF = float("-inf")


def _cparams(n_axes):
    return pltpu.CompilerParams(dimension_semantics=("arbitrary",) * n_axes,
                                vmem_limit_bytes=VMEM_LIMIT)


def _dot(a, b):
    return jnp.dot(a, b, preferred_element_type=F32)


def _dot_nt(a, b):
    return lax.dot_general(a, b, (((1,), (1,)), ((), ())), preferred_element_type=F32)


def _inproj_kernel(x_ref, w_ref, cos_ref, sin_ref,
                   qd_ref, kd_ref, vd_ref, qf_ref, kf_ref, vf_ref, zf_ref):
    xb = x_ref[0].astype(MXU_DTYPE)
    tm = xb.shape[0]
    cos = cos_ref[...]
    sin = sin_ref[...]
    lane = lax.broadcasted_iota(jnp.int32, (tm, LANES), 1)
    first_half = (lane % HEAD_DIM) < (HEAD_DIM // 2)

    def rope(z):
        rot = jnp.where(first_half, pltpu.roll(z, LANES - 32, 1), pltpu.roll(z, 32, 1))
        return z * cos + rot * sin

    def proj(col, width):
        return _dot(xb, w_ref[:, col:col + width])

    scale = HEAD_DIM ** -0.5
    col = 0
    for out_ref, roped, mul in ((qd_ref, True, scale), (kd_ref, True, 1.0)):
        for c in range(D_DIL // 256):
            z = proj(col, 256)
            for half in range(2):
                zz = rope(z[:, half * LANES:(half + 1) * LANES]) * mul
                out_ref[0, :, c * 256 + half * LANES:c * 256 + (half + 1) * LANES] = zz.astype(out_ref.dtype)
            col += 256
    for c in range(D_DIL // 256):
        vd_ref[0, :, c * 256:(c + 1) * 256] = proj(col, 256).astype(vd_ref.dtype)
        col += 256
    qf_ref[0] = (proj(col, D_FOX) * scale).astype(qf_ref.dtype)
    col += D_FOX
    kf_ref[0] = proj(col, D_FOX).astype(kf_ref.dtype)
    col += D_FOX
    vf_ref[0] = proj(col, D_FOX).astype(vf_ref.dtype)
    col += D_FOX
    zf_ref[0] = proj(col, LANES)


def _in_projection(h, w_pad, cos_tab, sin_tab, tm=512):
    B, S, D = h.shape
    grid = (B, S // tm)
    row = lambda width: pl.BlockSpec((1, tm, width), lambda b, i: (b, i, 0))
    out_shape = (
        jax.ShapeDtypeStruct((B, S, D_DIL), MXU_DTYPE),
        jax.ShapeDtypeStruct((B, S, D_DIL), MXU_DTYPE),
        jax.ShapeDtypeStruct((B, S, D_DIL), MXU_DTYPE),
        jax.ShapeDtypeStruct((B, S, D_FOX), MXU_DTYPE),
        jax.ShapeDtypeStruct((B, S, D_FOX), MXU_DTYPE),
        jax.ShapeDtypeStruct((B, S, D_FOX), MXU_DTYPE),
        jax.ShapeDtypeStruct((B, S, LANES), F32),
    )
    return pl.pallas_call(
        _inproj_kernel,
        grid=grid,
        in_specs=[
            row(D),
            pl.BlockSpec((D, D_PROJ_PAD), lambda b, i: (0, 0)),
            pl.BlockSpec((tm, LANES), lambda b, i: (i, 0)),
            pl.BlockSpec((tm, LANES), lambda b, i: (i, 0)),
        ],
        out_specs=(row(D_DIL), row(D_DIL), row(D_DIL), row(D_FOX), row(D_FOX), row(D_FOX), row(LANES)),
        out_shape=out_shape,
        compiler_params=_cparams(2),
        name="in_projection",
    )(h, w_pad, cos_tab, sin_tab)


def _split3(x):
    hi = x.astype(jnp.bfloat16)
    r1 = x - hi.astype(F32)
    mid = r1.astype(jnp.bfloat16)
    lo = (r1 - mid.astype(F32)).astype(jnp.bfloat16)
    return hi, mid, lo


def _forget_cumsum_kernel(z_ref, b_ref, tri_ref, c_ref):
    x = z_ref[0] + b_ref[...]
    logf = jnp.minimum(x, 0.0) - jnp.log1p(jnp.exp(-jnp.abs(x)))
    tri = tri_ref[...]
    S = x.shape[1]
    carry = jnp.zeros((x.shape[0], 1), F32)
    for blk in range(S // LANES):
        seg = logf[:, blk * LANES:(blk + 1) * LANES]
        hi, mid, lo = _split3(seg)
        cs = (_dot(hi, tri) + _dot(mid, tri)) + _dot(lo, tri) + carry
        c_ref[0, :, blk * LANES:(blk + 1) * LANES] = cs
        carry = cs[:, LANES - 1:LANES]


def _forget_cumsum(zt, b_forget):
    B, Hf, S = zt.shape
    tri = (np.arange(LANES)[:, None] <= np.arange(LANES)[None, :]).astype(np.float32)
    return pl.pallas_call(
        _forget_cumsum_kernel,
        grid=(B,),
        in_specs=[
            pl.BlockSpec((1, Hf, S), lambda b: (b, 0, 0)),
            pl.BlockSpec((Hf, 1), lambda b: (0, 0)),
            pl.BlockSpec((LANES, LANES), lambda b: (0, 0)),
        ],
        out_specs=pl.BlockSpec((1, Hf, S), lambda b: (b, 0, 0)),
        out_shape=jax.ShapeDtypeStruct((B, Hf, S), F32),
        compiler_params=_cparams(1),
        name="forget_cumsum",
    )(zt, b_forget.reshape(Hf, 1).astype(F32), jnp.asarray(tri, jnp.bfloat16))


def _pair_rows(xp, is_a):
    zero = jnp.zeros_like(xp)
    return jnp.concatenate([jnp.where(is_a, xp, zero), jnp.where(is_a, zero, xp)], axis=0)


def _fox_kernel(q_ref, k_ref, v_ref, crow_ref, o_ref, *, tq):
    tk = tq
    i = pl.program_id(2)
    lane = lax.broadcasted_iota(jnp.int32, (tq, LANES), 1)
    is_a = lane < HEAD_DIM
    q2 = _pair_rows(q_ref[0], is_a)

    def scores(j):
        k0 = pl.multiple_of(j * tk, tk)
        s = _dot_nt(q2, k_ref[0, pl.ds(k0, tk), :])
        cr = crow_ref[0, 0, :, pl.ds(k0, tk)]
        return jnp.concatenate([s[:tq] - cr[0:1], s[tq:] - cr[1:2]], axis=0)

    def value_product(p, j):
        v0 = pl.multiple_of(j * tk, tk)
        return _dot(p, v_ref[0, pl.ds(v0, tk), :])

    def softmax_step(s, m_old, l_old, acc_old, pv_prev):
        m_new = jnp.maximum(m_old, jnp.max(s, axis=1, keepdims=True))
        alpha = jnp.exp(m_old - m_new)
        p = jnp.exp(s - m_new)
        l_new = alpha * l_old + jnp.sum(p, axis=1, keepdims=True)
        acc_new = alpha * (acc_old + pv_prev)
        return m_new, l_new, acc_new, p.astype(MXU_DTYPE)

    def body(j, carry):
        m, l, acc, s_cur, p_prev = carry
        s_next = scores(j + 1)
        pv_prev = value_product(p_prev, jnp.maximum(j - 1, 0))
        m, l, acc, p = softmax_step(s_cur, m, l, acc, pv_prev)
        return m, l, acc, s_next, p

    init = (jnp.full((2 * tq, 1), NEG_INF, F32), jnp.zeros((2 * tq, 1), F32),
            jnp.zeros((2 * tq, LANES), F32), scores(0),
            jnp.zeros((2 * tq, tk), MXU_DTYPE))
    m, l, acc, s_diag, p_prev = lax.fori_loop(0, i, body, init)

    pv_prev = value_product(p_prev, jnp.maximum(i - 1, 0))
    row = lax.broadcasted_iota(jnp.int32, (2 * tq, tk), 0)
    col = lax.broadcasted_iota(jnp.int32, (2 * tq, tk), 1)
    s_diag = jnp.where(col <= row % tq, s_diag, NEG_INF)
    m, l, acc, p = softmax_step(s_diag, m, l, acc, pv_prev)
    o2 = (acc + value_product(p, i)) / l
    o_ref[0] = jnp.where(is_a, o2[:tq], o2[tq:]).astype(o_ref.dtype)


def _fox_attention(qf, kf, vf, c, tq=256):
    B, S, _ = qf.shape
    n_pairs = D_FOX // LANES
    c4 = c.reshape(B, n_pairs, 2, S)
    return pl.pallas_call(
        functools.partial(_fox_kernel, tq=tq),
        grid=(B, n_pairs, S // tq),
        in_specs=[
            pl.BlockSpec((1, tq, LANES), lambda b, p, i: (b, i, p)),
            pl.BlockSpec((1, S, LANES), lambda b, p, i: (b, 0, p)),
            pl.BlockSpec((1, S, LANES), lambda b, p, i: (b, 0, p)),
            pl.BlockSpec((1, 1, 2, S), lambda b, p, i: (b, p, 0, 0)),
        ],
        out_specs=pl.BlockSpec((1, tq, LANES), lambda b, p, i: (b, i, p)),
        out_shape=jax.ShapeDtypeStruct((B, S, D_FOX), MXU_DTYPE),
        compiler_params=_cparams(3),
        name="fox_attention",
    )(qf, kf, vf, c4)


def _lane_pick(cols, offset):
    n = cols[0].shape[0]
    lane = lax.broadcasted_iota(jnp.int32, (n, LANES), 1)
    tile = jnp.zeros((n, LANES), F32)
    for j, cvec in enumerate(cols):
        tile = jnp.where(lane == offset + j, cvec, tile)
    return tile


def _dil_kernel(*refs, rows, merge):
    if merge:
        (q_ref, kc_ref, kp_ref, vc_ref, vp_ref, o4_ref, ml4_ref, o16_ref, ml16_ref,
         o_ref, kbuf, vbuf) = refs
        others = ((o4_ref, ml4_ref), (o16_ref, ml16_ref))
    else:
        q_ref, kc_ref, kp_ref, vc_ref, vp_ref, o_ref, ml_ref, kbuf, vbuf = refs
    chunk = pl.program_id(2)
    n_sub = rows // BLOCK
    kbuf[0:BLOCK, :] = kp_ref[0]
    kbuf[BLOCK:, :] = kc_ref[0]
    vbuf[0:BLOCK, :] = vp_ref[0]
    vbuf[BLOCK:, :] = vc_ref[0]

    lane = lax.broadcasted_iota(jnp.int32, (BLOCK, LANES), 1)
    is_a = lane < HEAD_DIM
    row = lax.broadcasted_iota(jnp.int32, (2 * BLOCK, 2 * BLOCK), 0)
    ki = lax.broadcasted_iota(jnp.int32, (2 * BLOCK, 2 * BLOCK), 1)
    delta = BLOCK + (row % BLOCK) - ki
    band = (delta >= 0) & (delta <= BLOCK)
    in_cur = ki >= BLOCK

    def sub_block(n, _):
        r0 = pl.multiple_of(n * BLOCK, BLOCK)
        has_prev = (chunk * n_sub + n) > 0
        valid = band & (in_cur | has_prev)
        m_cols, l_cols = [], []
        for hp in range(N_PAIRS_DIL):
            cs = slice(hp * LANES, (hp + 1) * LANES)
            q2 = _pair_rows(q_ref[0, pl.ds(r0, BLOCK), cs], is_a)
            kw = kbuf[pl.ds(r0, 2 * BLOCK), cs]
            vw = vbuf[pl.ds(r0, 2 * BLOCK), cs]
            s = jnp.where(valid, _dot_nt(q2, kw), NEG_INF)
            m = jnp.max(s, axis=1, keepdims=True)
            p = jnp.exp(s - m)
            l = jnp.sum(p, axis=1, keepdims=True)
            o2 = _dot(p.astype(MXU_DTYPE), vw)
            m_a, m_b = m[:BLOCK], m[BLOCK:]
            l_a, l_b = l[:BLOCK], l[BLOCK:]
            acc = jnp.where(is_a, o2[:BLOCK], o2[BLOCK:])
            if not merge:
                o_ref[0, pl.ds(r0, BLOCK), cs] = (acc / jnp.where(is_a, l_a, l_b)).astype(o_ref.dtype)
                m_cols += [m_a, m_b]
                l_cols += [l_a, l_b]
            else:
                num_a, num_b = [m_a], [m_b]
                stats = []
                for o_other, ml_other in others:
                    ml = ml_other[0, pl.ds(r0, BLOCK), :]
                    ma = ml[:, 2 * hp:2 * hp + 1]
                    mb = ml[:, 2 * hp + 1:2 * hp + 2]
                    la = ml[:, N_HEADS_DIL + 2 * hp:N_HEADS_DIL + 2 * hp + 1]
                    lb = ml[:, N_HEADS_DIL + 2 * hp + 1:N_HEADS_DIL + 2 * hp + 2]
                    stats.append((ma, mb, la, lb, o_other[0, pl.ds(r0, BLOCK), cs].astype(F32)))
                    num_a.append(ma)
                    num_b.append(mb)
                top_a = functools.reduce(jnp.maximum, num_a)
                top_b = functools.reduce(jnp.maximum, num_b)
                w_a = jnp.exp(m_a - top_a)
                w_b = jnp.exp(m_b - top_b)
                num = jnp.where(is_a, w_a, w_b) * acc
                den_a = w_a * l_a
                den_b = w_b * l_b
                for ma, mb, la, lb, o_n in stats:
                    wa = jnp.exp(ma - top_a) * la
                    wb = jnp.exp(mb - top_b) * lb
                    num = num + jnp.where(is_a, wa, wb) * o_n
                    den_a = den_a + wa
                    den_b = den_b + wb
                o_ref[0, pl.ds(r0, BLOCK), cs] = (num / jnp.where(is_a, den_a, den_b)).astype(o_ref.dtype)
        if not merge:
            ml_ref[0, pl.ds(r0, BLOCK), :] = _lane_pick(m_cols, 0) + _lane_pick(l_cols, N_HEADS_DIL)
        return 0

    lax.fori_loop(0, n_sub, sub_block, 0)


def _dilated_branch(q, k, v, dil, rows, others=None):
    B, S, _ = q.shape
    L = S // dil
    rows = min(rows, L)
    n_chunk = L // rows
    n_sub = rows // BLOCK
    qv, kv, vv = (t.reshape(B, L, dil * D_DIL) for t in (q, k, v))
    cur = pl.BlockSpec((1, rows, D_DIL), lambda b, r, c: (b, c, r))
    prev = pl.BlockSpec((1, BLOCK, D_DIL), lambda b, r, c: (b, jnp.maximum(c * n_sub - 1, 0), r))
    ml_spec = pl.BlockSpec((1, rows, LANES), lambda b, r, c: (b, c, r))
    in_specs = [cur, cur, prev, cur, prev]
    args = [qv, kv, kv, vv, vv]
    merge = others is not None
    if merge:
        assert dil == 1
        for o_n, ml_n in others:
            in_specs += [cur, ml_spec]
            args += [o_n, ml_n]
        out_specs = cur
        out_shape = jax.ShapeDtypeStruct((B, L, dil * D_DIL), MXU_DTYPE)
    else:
        out_specs = (cur, ml_spec)
        out_shape = (jax.ShapeDtypeStruct((B, L, dil * D_DIL), MXU_DTYPE),
                     jax.ShapeDtypeStruct((B, L, dil * LANES), F32))
    out = pl.pallas_call(
        functools.partial(_dil_kernel, rows=rows, merge=merge),
        grid=(B, dil, n_chunk),
        in_specs=in_specs,
        out_specs=out_specs,
        out_shape=out_shape,
        scratch_shapes=[pltpu.VMEM((rows + BLOCK, D_DIL), MXU_DTYPE),
                        pltpu.VMEM((rows + BLOCK, D_DIL), MXU_DTYPE)],
        compiler_params=_cparams(3),
        name=f"dilated_attention_r{dil}",
    )(*args)
    if merge:
        return out.reshape(B, S, D_DIL)
    return out[0].reshape(B, S, D_DIL), out[1].reshape(B, S, LANES)


def _dilated_attention(qd, kd, vd):
    o16 = _dilated_branch(qd, kd, vd, 16, 256)
    o4 = _dilated_branch(qd, kd, vd, 4, 512)
    return _dilated_branch(qd, kd, vd, 1, 512, others=(o4, o16))


def _layer_norm(u, g, b):
    mu = jnp.mean(u, axis=1, keepdims=True)
    d = u - mu
    var = jnp.mean(d * d, axis=1, keepdims=True)
    return d * lax.rsqrt(var + LN_EPS) * g + b


def _rms_norm(x, g):
    ms = jnp.mean(x * x, axis=1, keepdims=True)
    return x * lax.rsqrt(ms + RMS_EPS) * g


def _top2_of4(vals):
    v1, i1 = vals[0], jnp.zeros(vals[0].shape, jnp.int32)
    for i in range(1, 4):
        better = vals[i] > v1
        v1 = jnp.where(better, vals[i], v1)
        i1 = jnp.where(better, i, i1)
    v2 = jnp.full(vals[0].shape, -1.0, F32)
    i2 = jnp.zeros(vals[0].shape, jnp.int32)
    for i in range(4):
        better = (vals[i] > v2) & (i1 != i)
        v2 = jnp.where(better, vals[i], v2)
        i2 = jnp.where(better, i, i2)
    return v1, i1, v2, i2


def _outproj_kernel(od_ref, of_ref, h_ref, wo_ref, gd_ref, gf_ref, lng_ref, lnb_ref,
                    wr_ref, br_ref, tri_ref,
                    h1_ref, h1t_ref, e_ref, gate_ref, rank_ref, cnt_ref, base_ref):
    step = pl.program_id(0)

    @pl.when(step == 0)
    def _():
        base_ref[...] = jnp.zeros_like(base_ref)

    xd = _rms_norm(od_ref[...].astype(F32), gd_ref[...])
    xf = _rms_norm(of_ref[...].astype(F32), gf_ref[...])
    y = _dot(xd.astype(MXU_DTYPE), wo_ref[0:D_DIL, :]) + _dot(xf.astype(MXU_DTYPE), wo_ref[D_DIL:, :])
    h1 = _layer_norm(DEEPNORM_ALPHA * h_ref[...] + y, lng_ref[...], lnb_ref[...])
    h1_ref[...] = h1
    _store_token_tiles(h1t_ref, h1)
    tm = h1.shape[0]

    h_hi, h_mid, _ = _split3(h1)
    two = _dot_nt(wr_ref[...], h_hi)
    logits = (two[:N_EXPERTS] + two[N_EXPERTS:]) + _dot_nt(wr_ref[0:N_EXPERTS, :], h_mid) + br_ref[...]
    logits = logits - jnp.max(logits, axis=0, keepdims=True)
    ex = jnp.exp(logits)
    probs = ex / jnp.sum(ex, axis=0, keepdims=True)
    pr = [probs[j:j + 1, :] for j in range(N_EXPERTS)]

    def group_score(g):
        v = pr[4 * g:4 * g + 4]
        pairs = [v[a] + v[b] for a in range(4) for b in range(a + 1, 4)]
        return functools.reduce(jnp.maximum, pairs)

    best = group_score(0)
    gsel = jnp.zeros((1, tm), jnp.int32)
    for g in range(1, N_GROUPS):
        sc = group_score(g)
        better = sc > best
        best = jnp.where(better, sc, best)
        gsel = jnp.where(better, g, gsel)
    in_grp = []
    for i in range(EXPERTS_PER_GROUP):
        v = pr[i]
        for g in range(1, N_GROUPS):
            v = jnp.where(gsel == g, pr[4 * g + i], v)
        in_grp.append(v)
    v1, i1, v2, i2 = _top2_of4(in_grp)
    e1 = gsel * EXPERTS_PER_GROUP + i1
    e2 = gsel * EXPERTS_PER_GROUP + i2
    den = v1 + v2
    e_ref[...] = jnp.concatenate([e1, e2], axis=0)
    gate_ref[...] = jnp.concatenate([v1 / den, v2 / den], axis=0)

    eidx = lax.broadcasted_iota(jnp.int32, (N_EXPERTS, tm), 0)
    oh1 = (eidx == e1).astype(F32)
    oh2 = (eidx == e2).astype(F32)
    tot = oh1 + oh2
    before = base_ref[...] + _dot(tot.astype(jnp.bfloat16), tri_ref[...])
    r1 = jnp.sum(oh1 * before, axis=0, keepdims=True)
    r2 = jnp.sum(oh2 * before, axis=0, keepdims=True)
    rank_ref[...] = jnp.concatenate([r1, r2], axis=0).astype(jnp.int32)
    base_ref[...] = base_ref[...] + jnp.sum(tot, axis=1, keepdims=True)
    cnt_ref[...] = jnp.broadcast_to(base_ref[...], cnt_ref.shape)


def _out_projection(od, of, h, w_out, g_dil, g_fox, ln_g, ln_b, wr3, b_router, tm=256):
    T = h.shape[0]
    tri = (np.arange(tm)[:, None] < np.arange(tm)[None, :]).astype(np.float32)
    rows = lambda width: pl.BlockSpec((tm, width), lambda i: (i, 0))
    full = lambda a: pl.BlockSpec(a.shape, lambda i: (0,) * a.ndim)
    tok = pl.BlockSpec((TOP_K, tm), lambda i: (0, i))
    consts = [w_out, g_dil.reshape(1, -1), g_fox.reshape(1, -1), ln_g.reshape(1, -1), ln_b.reshape(1, -1),
              wr3, b_router.reshape(-1, 1).astype(F32), jnp.asarray(tri, jnp.bfloat16)]
    return pl.pallas_call(
        _outproj_kernel,
        grid=(T // tm,),
        in_specs=[rows(D_DIL), rows(D_FOX), rows(D_MODEL)] + [full(a) for a in consts],
        out_specs=(rows(D_MODEL), pl.BlockSpec((tm * TOKEN_TILE_ROWS, LANES), lambda i: (i, 0)),
                   tok, tok, tok, pl.BlockSpec((N_EXPERTS, LANES), lambda i: (0, 0))),
        out_shape=(
            jax.ShapeDtypeStruct((T, D_MODEL), F32),
            jax.ShapeDtypeStruct((T * TOKEN_TILE_ROWS, LANES), F32),
            jax.ShapeDtypeStruct((TOP_K, T), jnp.int32),
            jax.ShapeDtypeStruct((TOP_K, T), F32),
            jax.ShapeDtypeStruct((TOP_K, T), jnp.int32),
            jax.ShapeDtypeStruct((N_EXPERTS, LANES), F32),
        ),
        scratch_shapes=[pltpu.VMEM((N_EXPERTS, 1), F32)],
        compiler_params=_cparams(1),
        name="out_projection_router",
    )(od, of, h, *consts)


def _store_token_tiles(ref, x):
    n = x.shape[0]
    for c in range(TOKEN_TILE_ROWS):
        ref[pl.ds(c, n, stride=TOKEN_TILE_ROWS), :] = x[:, c * LANES:(c + 1) * LANES]


def _load_token_tiles(ref, n):
    return jnp.concatenate([ref[pl.ds(c, n, stride=TOKEN_TILE_ROWS), :] for c in range(TOKEN_TILE_ROWS)], axis=1)


def _tile_copy(src, src_row, dst, dst_row, sem):
    return pltpu.make_async_copy(src.at[pl.ds(src_row, TOKEN_TILE_ROWS)],
                                 dst.at[pl.ds(dst_row, TOKEN_TILE_ROWS)], sem)


def _dispatch_kernel(p0_ref, p1_ref, h_ref, xg_in, xg_out, sem, *, td):
    del xg_in

    def copies(t):
        src = pl.multiple_of(t * TOKEN_TILE_ROWS, TOKEN_TILE_ROWS)
        return (_tile_copy(h_ref, src, xg_out, pl.multiple_of(p0_ref[t], TOKEN_TILE_ROWS), sem),
                _tile_copy(h_ref, src, xg_out, pl.multiple_of(p1_ref[t], TOKEN_TILE_ROWS), sem))

    def issue(t, _):
        for cp in copies(t):
            cp.start()
        return 0

    def drain(t, _):
        for cp in copies(t):
            cp.wait()
        return 0

    lax.fori_loop(0, td, issue, 0, unroll=DMA_ISSUE_UNROLL)
    lax.fori_loop(0, td, drain, 0, unroll=DMA_ISSUE_UNROLL)


def _dispatch(h1t, pos_rows, n_rows, td=256):
    T = h1t.shape[0] // TOKEN_TILE_ROWS
    smem = pl.BlockSpec((td,), lambda i: (i,), memory_space=pltpu.SMEM)
    anyspec = pl.BlockSpec(memory_space=pl.ANY)
    return pl.pallas_call(
        functools.partial(_dispatch_kernel, td=td),
        grid=(T // td,),
        in_specs=[smem, smem, pl.BlockSpec((td * TOKEN_TILE_ROWS, LANES), lambda i: (i, 0)), anyspec],
        out_specs=anyspec,
        out_shape=jax.ShapeDtypeStruct((n_rows * TOKEN_TILE_ROWS, LANES), F32),
        scratch_shapes=[pltpu.SemaphoreType.DMA(())],
        input_output_aliases={3: 0},
        compiler_params=_cparams(1),
        name="moe_dispatch",
    )(pos_rows[0], pos_rows[1], h1t, jnp.zeros((n_rows * TOKEN_TILE_ROWS, LANES), F32))


def _ffn_kernel(be_ref, nv_ref, x_ref, wg_ref, wu_ref, wd_ref, y_ref, *, tmf):
    del be_ref
    j = pl.program_id(0)

    @pl.when(j < nv_ref[0])
    def _():
        xb = _load_token_tiles(x_ref, tmf).astype(MXU_DTYPE)
        a = _dot(xb, wg_ref[0])
        u = _dot(xb, wu_ref[0])
        hmid = (a * jax.nn.sigmoid(a)) * u
        _store_token_tiles(y_ref, _dot(hmid.astype(MXU_DTYPE), wd_ref[0]))

    @pl.when(j >= nv_ref[0])
    def _():
        y_ref[...] = jnp.zeros_like(y_ref)


def _expert_ffn(xg, blk_e, n_valid, wg, wu, wd, tmf):
    n_rows = xg.shape[0] // TOKEN_TILE_ROWS
    D = D_MODEL
    nb = n_rows // tmf
    xmap = lambda j, be, nv: (jnp.minimum(j, nv[0] - 1), 0)
    wmap = lambda j, be, nv: (be[j], 0, 0)
    grid_spec = pltpu.PrefetchScalarGridSpec(
        num_scalar_prefetch=2,
        grid=(nb,),
        in_specs=[
            pl.BlockSpec((tmf * TOKEN_TILE_ROWS, LANES), xmap),
            pl.BlockSpec((1, D, D), wmap),
            pl.BlockSpec((1, D, D), wmap),
            pl.BlockSpec((1, D, D), wmap),
        ],
        out_specs=pl.BlockSpec((tmf * TOKEN_TILE_ROWS, LANES), lambda j, be, nv: (j, 0)),
    )
    return pl.pallas_call(
        functools.partial(_ffn_kernel, tmf=tmf),
        grid_spec=grid_spec,
        out_shape=jax.ShapeDtypeStruct(xg.shape, F32),
        compiler_params=_cparams(1),
        name="moe_expert_ffn",
    )(blk_e, n_valid, xg, wg, wu, wd)


def _combine_kernel(p0_ref, p1_ref, gate_ref, h_ref, lng_ref, lnb_ref, y_hbm, o_ref, buf0, buf1, sem, *, tc):
    def copies(t):
        dst = pl.multiple_of(t * TOKEN_TILE_ROWS, TOKEN_TILE_ROWS)
        return (_tile_copy(y_hbm, pl.multiple_of(p0_ref[t], TOKEN_TILE_ROWS), buf0, dst, sem),
                _tile_copy(y_hbm, pl.multiple_of(p1_ref[t], TOKEN_TILE_ROWS), buf1, dst, sem))

    def issue(t, _):
        for cp in copies(t):
            cp.start()
        return 0

    def drain(t, _):
        for cp in copies(t):
            cp.wait()
        return 0

    lax.fori_loop(0, tc, issue, 0, unroll=DMA_ISSUE_UNROLL)
    lax.fori_loop(0, tc, drain, 0, unroll=DMA_ISSUE_UNROLL)
    g = gate_ref[...]
    y = _load_token_tiles(buf0, tc) * g[:, 0:1] + _load_token_tiles(buf1, tc) * g[:, 1:2]
    o_ref[...] = _layer_norm(DEEPNORM_ALPHA * h_ref[...] + y, lng_ref[...], lnb_ref[...])


def _combine(yg, pos_rows, gates_col, h1, ln_g, ln_b, tc=256):
    T, D = h1.shape
    smem = pl.BlockSpec((tc,), lambda i: (i,), memory_space=pltpu.SMEM)
    rows = lambda width: pl.BlockSpec((tc, width), lambda i: (i, 0))
    vec = pl.BlockSpec((1, D), lambda i: (0, 0))
    buf = pltpu.VMEM((tc * TOKEN_TILE_ROWS, LANES), F32)
    return pl.pallas_call(
        functools.partial(_combine_kernel, tc=tc),
        grid=(T // tc,),
        in_specs=[smem, smem, rows(TOP_K), rows(D), vec, vec, pl.BlockSpec(memory_space=pl.ANY)],
        out_specs=rows(D),
        out_shape=jax.ShapeDtypeStruct((T, D), F32),
        scratch_shapes=[buf, buf, pltpu.SemaphoreType.DMA(())],
        compiler_params=_cparams(1),
        name="moe_combine_ln",
    )(pos_rows[0], pos_rows[1], gates_col, h1, ln_g.reshape(1, -1), ln_b.reshape(1, -1), yg)


def _grouped_moe(h1, h1t, experts, gates, ranks, counts, wg, wu, wd, ln_g, ln_b, tmf=256):
    T = h1.shape[0]
    nb = (T * TOP_K) // tmf + N_EXPERTS
    cnt = counts[:, 0].astype(jnp.int32)
    pcnt = (cnt + tmf - 1) // tmf * tmf
    pends = jnp.cumsum(pcnt)
    pstart = pends - pcnt
    seg = jnp.zeros_like(experts)
    for e in range(N_EXPERTS):
        seg = jnp.where(experts == e, pstart[e], seg)
    pos_rows = (seg + ranks) * TOKEN_TILE_ROWS
    blk_e = jnp.minimum(jnp.searchsorted(pends, jnp.arange(nb, dtype=jnp.int32) * tmf, side="right"),
                        N_EXPERTS - 1).astype(jnp.int32)
    n_valid = (pends[-1:] // tmf).astype(jnp.int32)
    xg = _dispatch(h1t, pos_rows, nb * tmf)
    yg = _expert_ffn(xg, blk_e, n_valid, wg, wu, wd, tmf)
    return _combine(yg, pos_rows, gates.T, h1, ln_g, ln_b)


def _rope_tables(S):
    half = HEAD_DIM // 2
    inv = ROPE_THETA ** (-jnp.arange(half, dtype=F32) / half)
    ang = jnp.arange(S, dtype=F32)[:, None] * inv[None, :]
    cos, sin = jnp.cos(ang), jnp.sin(ang)
    reps = LANES // HEAD_DIM
    cos_tab = jnp.tile(jnp.concatenate([cos, cos], axis=1), (1, reps))
    sin_tab = jnp.tile(jnp.concatenate([-sin, sin], axis=1), (1, reps))
    return cos_tab, sin_tab


def _pad_w_in(w):
    pad = jnp.zeros((D_MODEL, D_PROJ_PAD - w.shape[1]), w.dtype)
    return jnp.concatenate([w, pad], axis=1).astype(MXU_DTYPE)


def kernel(x, w_in, b_forget, g_dil, g_fox, w_out, ln1_g, ln1_b, w_router, b_router,
           w_gate, w_up, w_down, ln2_g, ln2_b):
    B, S, D = x.shape
    T = B * S
    cos_tab, sin_tab = _rope_tables(S)
    wr_hi, wr_mid, _ = _split3(w_router.astype(F32).T)
    wr3 = jnp.concatenate([wr_hi, wr_mid], axis=0)
    h = x
    for l in range(DEPTH):
        qd, kd, vd, qf, kf, vf, zf = _in_projection(h, _pad_w_in(w_in[l]), cos_tab, sin_tab)
        zt = zf[:, :, :N_HEADS_FOX].transpose(0, 2, 1)
        c = _forget_cumsum(zt, b_forget[l])
        o_fox = _fox_attention(qf, kf, vf, c)
        o_dil = _dilated_attention(qd, kd, vd)
        h1, h1t, experts, gates, ranks, counts = _out_projection(
            o_dil.reshape(T, D_DIL), o_fox.reshape(T, D_FOX), h.reshape(T, D),
            w_out[l].astype(MXU_DTYPE), g_dil[l], g_fox[l], ln1_g[l], ln1_b[l], wr3, b_router)
        h2 = _grouped_moe(h1, h1t, experts, gates, ranks, counts,
                          w_gate[l].astype(MXU_DTYPE), w_up[l].astype(MXU_DTYPE), w_down[l].astype(MXU_DTYPE),
                          ln2_g[l], ln2_b[l])
        h = h2.reshape(B, S, D)
    return h
```

```python
import functools

import numpy as np
import jax
import jax.numpy as jnp
from jax import lax
from jax.experimental import pallas as pl
from jax.experimental.pallas import tpu as pltpu

D_MODEL = 1024
DEPTH = 2
HEAD_DIM = 64
N_HEADS_DIL = 12
N_HEADS_FOX = 4
D_DIL = N_HEADS_DIL * HEAD_DIM
D_FOX = N_HEADS_FOX * HEAD_DIM
DILATIONS = (1, 4, 16)
BLOCK = 128
ROPE_THETA = 10000.0
N_EXPERTS = 16
N_GROUPS = 4
EXPERTS_PER_GROUP = 4
TOP_K = 2
DEEPNORM_ALPHA = (2.0 * DEPTH) ** 0.25
LN_EPS = 1e-5
RMS_EPS = 1e-6

LANES = 128
N_PAIRS_DIL = D_DIL // LANES
D_PROJ_PAD = 3 * D_DIL + 3 * D_FOX + LANES
VMEM_LIMIT = 48 * 1024 * 1024
TOKEN_TILE_ROWS = D_MODEL // LANES
DMA_ISSUE_UNROLL = 8
DIL_PAD = max(DILATIONS) * BLOCK
MASK_BIAS = -1e30
LOG2_E = 1.4426950408889634
DIL_UNITS_PER_BODY = 8

MXU_DTYPE = jnp.bfloat16
F32 = jnp.float32
NEG_INF = float("-inf")


def _cparams(n_axes):
    return pltpu.CompilerParams(dimension_semantics=("arbitrary",) * n_axes,
                                vmem_limit_bytes=VMEM_LIMIT)


def _dot(a, b):
    return jnp.dot(a, b, preferred_element_type=F32)


def _dot_nt(a, b):
    return lax.dot_general(a, b, (((1,), (1,)), ((), ())), preferred_element_type=F32)


def _inproj_kernel(x_ref, w_ref, cos_ref, sin_ref,
                   qd_ref, kd_ref, vd_ref, qf_ref, kf_ref, vf_ref, zf_ref):
    xb = x_ref[0].astype(MXU_DTYPE)
    tm = xb.shape[0]
    cos = cos_ref[...]
    sin = sin_ref[...]
    lane = lax.broadcasted_iota(jnp.int32, (tm, LANES), 1)
    first_half = (lane % HEAD_DIM) < (HEAD_DIM // 2)

    def rope(z):
        rot = jnp.where(first_half, pltpu.roll(z, LANES - 32, 1), pltpu.roll(z, 32, 1))
        return z * cos + rot * sin

    def proj(col, width):
        return _dot(xb, w_ref[:, col:col + width])

    scale = HEAD_DIM ** -0.5
    col = 0
    for out_ref, roped, mul in ((qd_ref, True, scale * LOG2_E), (kd_ref, True, 1.0)):
        for c in range(D_DIL // 256):
            z = proj(col, 256)
            for half in range(2):
                zz = rope(z[:, half * LANES:(half + 1) * LANES]) * mul
                out_ref[0, :, c * 256 + half * LANES:c * 256 + (half + 1) * LANES] = zz.astype(out_ref.dtype)
            col += 256
    for c in range(D_DIL // 256):
        vd_ref[0, :, c * 256:(c + 1) * 256] = proj(col, 256).astype(vd_ref.dtype)
        col += 256
    qf_ref[0] = (proj(col, D_FOX) * scale).astype(qf_ref.dtype)
    col += D_FOX
    kf_ref[0] = proj(col, D_FOX).astype(kf_ref.dtype)
    col += D_FOX
    vf_ref[0] = proj(col, D_FOX).astype(vf_ref.dtype)
    col += D_FOX
    zf_ref[0] = proj(col, LANES)


def _in_projection(h, w_pad, cos_tab, sin_tab, tm=512):
    B, S, D = h.shape
    grid = (B, S // tm)
    row = lambda width: pl.BlockSpec((1, tm, width), lambda b, i: (b, i, 0))
    out_shape = (
        jax.ShapeDtypeStruct((B, S, D_DIL), MXU_DTYPE),
        jax.ShapeDtypeStruct((B, S, D_DIL), MXU_DTYPE),
        jax.ShapeDtypeStruct((B, S, D_DIL), MXU_DTYPE),
        jax.ShapeDtypeStruct((B, S, D_FOX), MXU_DTYPE),
        jax.ShapeDtypeStruct((B, S, D_FOX), MXU_DTYPE),
        jax.ShapeDtypeStruct((B, S, D_FOX), MXU_DTYPE),
        jax.ShapeDtypeStruct((B, S, LANES), F32),
    )
    return pl.pallas_call(
        _inproj_kernel,
        grid=grid,
        in_specs=[
            row(D),
            pl.BlockSpec((D, D_PROJ_PAD), lambda b, i: (0, 0)),
            pl.BlockSpec((tm, LANES), lambda b, i: (i, 0)),
            pl.BlockSpec((tm, LANES), lambda b, i: (i, 0)),
        ],
        out_specs=(row(D_DIL), row(D_DIL), row(D_DIL), row(D_FOX), row(D_FOX), row(D_FOX), row(LANES)),
        out_shape=out_shape,
        compiler_params=_cparams(2),
        name="in_projection",
    )(h, w_pad, cos_tab, sin_tab)


def _split3(x):
    hi = x.astype(jnp.bfloat16)
    r1 = x - hi.astype(F32)
    mid = r1.astype(jnp.bfloat16)
    lo = (r1 - mid.astype(F32)).astype(jnp.bfloat16)
    return hi, mid, lo


def _forget_cumsum_kernel(z_ref, b_ref, tri_ref, c_ref):
    x = z_ref[0] + b_ref[...]
    logf = jnp.minimum(x, 0.0) - jnp.log1p(jnp.exp(-jnp.abs(x)))
    tri = tri_ref[...]
    S = x.shape[1]
    carry = jnp.zeros((x.shape[0], 1), F32)
    for blk in range(S // LANES):
        seg = logf[:, blk * LANES:(blk + 1) * LANES]
        hi, mid, lo = _split3(seg)
        cs = (_dot(hi, tri) + _dot(mid, tri)) + _dot(lo, tri) + carry
        c_ref[0, :, blk * LANES:(blk + 1) * LANES] = cs
        carry = cs[:, LANES - 1:LANES]


def _forget_cumsum(zt, b_forget):
    B, Hf, S = zt.shape
    tri = (np.arange(LANES)[:, None] <= np.arange(LANES)[None, :]).astype(np.float32)
    return pl.pallas_call(
        _forget_cumsum_kernel,
        grid=(B,),
        in_specs=[
            pl.BlockSpec((1, Hf, S), lambda b: (b, 0, 0)),
            pl.BlockSpec((Hf, 1), lambda b: (0, 0)),
            pl.BlockSpec((LANES, LANES), lambda b: (0, 0)),
        ],
        out_specs=pl.BlockSpec((1, Hf, S), lambda b: (b, 0, 0)),
        out_shape=jax.ShapeDtypeStruct((B, Hf, S), F32),
        compiler_params=_cparams(1),
        name="forget_cumsum",
    )(zt, b_forget.reshape(Hf, 1).astype(F32), jnp.asarray(tri, jnp.bfloat16))


def _pair_rows(xp, is_a):
    zero = jnp.zeros_like(xp)
    return jnp.concatenate([jnp.where(is_a, xp, zero), jnp.where(is_a, zero, xp)], axis=0)


def _fox_kernel(q_ref, k_ref, v_ref, crow_ref, o_ref, *, tq):
    tk = tq
    i = pl.program_id(2)
    lane = lax.broadcasted_iota(jnp.int32, (tq, LANES), 1)
    is_a = lane < HEAD_DIM
    q2 = _pair_rows(q_ref[0], is_a)

    def scores(j):
        k0 = pl.multiple_of(j * tk, tk)
        s = _dot_nt(q2, k_ref[0, pl.ds(k0, tk), :])
        cr = crow_ref[0, 0, :, pl.ds(k0, tk)]
        return jnp.concatenate([s[:tq] - cr[0:1], s[tq:] - cr[1:2]], axis=0)

    def value_product(p, j):
        v0 = pl.multiple_of(j * tk, tk)
        return _dot(p, v_ref[0, pl.ds(v0, tk), :])

    def softmax_step(s, m_old, l_old, acc_old, pv_prev):
        m_new = jnp.maximum(m_old, jnp.max(s, axis=1, keepdims=True))
        alpha = jnp.exp(m_old - m_new)
        p = jnp.exp(s - m_new)
        l_new = alpha * l_old + jnp.sum(p, axis=1, keepdims=True)
        acc_new = alpha * (acc_old + pv_prev)
        return m_new, l_new, acc_new, p.astype(MXU_DTYPE)

    def body(j, carry):
        m, l, acc, s_cur, p_prev = carry
        s_next = scores(j + 1)
        pv_prev = value_product(p_prev, jnp.maximum(j - 1, 0))
        m, l, acc, p = softmax_step(s_cur, m, l, acc, pv_prev)
        return m, l, acc, s_next, p

    init = (jnp.full((2 * tq, 1), NEG_INF, F32), jnp.zeros((2 * tq, 1), F32),
            jnp.zeros((2 * tq, LANES), F32), scores(0),
            jnp.zeros((2 * tq, tk), MXU_DTYPE))
    m, l, acc, s_diag, p_prev = lax.fori_loop(0, i, body, init)

    pv_prev = value_product(p_prev, jnp.maximum(i - 1, 0))
    row = lax.broadcasted_iota(jnp.int32, (2 * tq, tk), 0)
    col = lax.broadcasted_iota(jnp.int32, (2 * tq, tk), 1)
    s_diag = jnp.where(col <= row % tq, s_diag, NEG_INF)
    m, l, acc, p = softmax_step(s_diag, m, l, acc, pv_prev)
    o2 = (acc + value_product(p, i)) / l
    o_ref[0] = jnp.where(is_a, o2[:tq], o2[tq:]).astype(o_ref.dtype)


def _fox_attention(qf, kf, vf, c, tq=256):
    B, S, _ = qf.shape
    n_pairs = D_FOX // LANES
    c4 = c.reshape(B, n_pairs, 2, S)
    return pl.pallas_call(
        functools.partial(_fox_kernel, tq=tq),
        grid=(B, n_pairs, S // tq),
        in_specs=[
            pl.BlockSpec((1, tq, LANES), lambda b, p, i: (b, i, p)),
            pl.BlockSpec((1, S, LANES), lambda b, p, i: (b, 0, p)),
            pl.BlockSpec((1, S, LANES), lambda b, p, i: (b, 0, p)),
            pl.BlockSpec((1, 1, 2, S), lambda b, p, i: (b, p, 0, 0)),
        ],
        out_specs=pl.BlockSpec((1, tq, LANES), lambda b, p, i: (b, i, p)),
        out_shape=jax.ShapeDtypeStruct((B, S, D_FOX), MXU_DTYPE),
        compiler_params=_cparams(3),
        name="fox_attention",
    )(qf, kf, vf, c4)


def _band_bias():
    qi = np.arange(BLOCK)[:, None]
    ki = np.arange(2 * BLOCK)[None, :]
    delta = BLOCK + qi - ki
    band = (delta >= 0) & (delta <= BLOCK)
    first = band & (ki >= BLOCK)
    return np.where(np.stack([first, band]), 0.0, MASK_BIAS).astype(np.float32)


def _dil_kernel(q_ref, k_ref, v_ref, bias_ref, o_ref, q32, k32, v32, onorm_s, mu_s):
    S = q_ref.shape[1]
    q32[...] = q_ref[0].astype(F32)
    zeros = jnp.zeros((DIL_PAD, LANES), F32)
    k32[0:DIL_PAD, :] = zeros
    v32[0:DIL_PAD, :] = zeros
    k32[DIL_PAD:, :] = k_ref[0].astype(F32)
    v32[DIL_PAD:, :] = v_ref[0].astype(F32)

    lane = lax.broadcasted_iota(jnp.int32, (BLOCK, LANES), 1)
    is_a = lane < HEAD_DIM
    ones = jnp.ones((2 * BLOCK, LANES), MXU_DTYPE)

    def rows(ref, start, count, dil):
        if dil == 1:
            return ref[pl.ds(start, count), :]
        return ref[pl.ds(start, count, stride=dil), :]

    def unit(dil, rho, n):
        q0 = rho + dil * BLOCK * n
        qb = rows(q32, q0, BLOCK, dil).astype(MXU_DTYPE)
        kw = rows(k32, q0 + DIL_PAD - dil * BLOCK, 2 * BLOCK, dil).astype(MXU_DTYPE)
        vw = rows(v32, q0 + DIL_PAD - dil * BLOCK, 2 * BLOCK, dil).astype(MXU_DTYPE)
        bias = bias_ref[jnp.minimum(n, 1)]
        s = _dot_nt(_pair_rows(qb, is_a), kw)
        s = jnp.concatenate([s[:BLOCK] + bias, s[BLOCK:] + bias], axis=0)
        m = jnp.max(s, axis=1, keepdims=True)
        p = jnp.exp2(s - m).astype(MXU_DTYPE)
        o = _dot(p, jnp.concatenate([vw, ones], axis=1))
        acc = jnp.where(is_a, o[:BLOCK, :LANES], o[BLOCK:, :LANES])
        l = jnp.where(is_a, o[:BLOCK, LANES:], o[BLOCK:, LANES:])
        mu = jnp.where(is_a, m[:BLOCK], m[BLOCK:]) + jnp.log2(l)
        return acc / l, mu

    n_units = S // BLOCK

    for slab, dil in enumerate(d for d in DILATIONS if d != 1):
        blk_bits = (n_units // dil).bit_length() - 1

        def several_units(g, _, slab=slab, dil=dil, blk_bits=blk_bits):
            for u in range(DIL_UNITS_PER_BODY):
                t = g * DIL_UNITS_PER_BODY + u
                rho = lax.shift_right_logical(t, blk_bits)
                n = t & ((1 << blk_bits) - 1)
                o_n, mu = unit(dil, rho, n)
                q0 = rho + dil * BLOCK * n
                onorm_s[slab, pl.ds(q0, BLOCK, stride=dil), :] = o_n
                mu_s[slab, pl.ds(q0, BLOCK, stride=dil), :] = mu
            return 0

        lax.fori_loop(0, n_units // DIL_UNITS_PER_BODY, several_units, 0)

    def merge_blocks(g, _):
        for u in range(DIL_UNITS_PER_BODY):
            n = g * DIL_UNITS_PER_BODY + u
            o_1, mu_1 = unit(1, 0, n)
            r0 = pl.multiple_of(n * BLOCK, BLOCK)
            others = [(onorm_s[sl, pl.ds(r0, BLOCK), :], mu_s[sl, pl.ds(r0, BLOCK), :])
                      for sl in range(len(DILATIONS) - 1)]
            top = functools.reduce(jnp.maximum, [mu_1] + [mu for _, mu in others])
            w = jnp.exp2(mu_1 - top)
            num, den = w * o_1, w
            for o_p, mu_p in others:
                w = jnp.exp2(mu_p - top)
                num = num + w * o_p
                den = den + w
            o_ref[0, pl.ds(r0, BLOCK), :] = (num / den).astype(o_ref.dtype)
        return 0

    lax.fori_loop(0, n_units // DIL_UNITS_PER_BODY, merge_blocks, 0)


def _dilated_attention(qd, kd, vd):
    B, S, _ = qd.shape
    blk = pl.BlockSpec((1, S, LANES), lambda b, p: (b, 0, p))
    n_slabs = len(DILATIONS) - 1
    return pl.pallas_call(
        _dil_kernel,
        grid=(B, N_PAIRS_DIL),
        in_specs=[blk, blk, blk, pl.BlockSpec((2, BLOCK, 2 * BLOCK), lambda b, p: (0, 0, 0))],
        out_specs=blk,
        out_shape=jax.ShapeDtypeStruct((B, S, D_DIL), MXU_DTYPE),
        scratch_shapes=[
            pltpu.VMEM((S, LANES), F32),
            pltpu.VMEM((DIL_PAD + S, LANES), F32),
            pltpu.VMEM((DIL_PAD + S, LANES), F32),
            pltpu.VMEM((n_slabs, S, LANES), F32),
            pltpu.VMEM((n_slabs, S, LANES), F32),
        ],
        compiler_params=_cparams(2),
        name="dilated_attention",
    )(qd, kd, vd, jnp.asarray(_band_bias()))


def _layer_norm(u, g, b):
    mu = jnp.mean(u, axis=1, keepdims=True)
    d = u - mu
    var = jnp.mean(d * d, axis=1, keepdims=True)
    return d * lax.rsqrt(var + LN_EPS) * g + b


def _rms_norm(x, g):
    ms = jnp.mean(x * x, axis=1, keepdims=True)
    return x * lax.rsqrt(ms + RMS_EPS) * g


def _top2_of4(vals):
    v1, i1 = vals[0], jnp.zeros(vals[0].shape, jnp.int32)
    for i in range(1, 4):
        better = vals[i] > v1
        v1 = jnp.where(better, vals[i], v1)
        i1 = jnp.where(better, i, i1)
    v2 = jnp.full(vals[0].shape, -1.0, F32)
    i2 = jnp.zeros(vals[0].shape, jnp.int32)
    for i in range(4):
        better = (vals[i] > v2) & (i1 != i)
        v2 = jnp.where(better, vals[i], v2)
        i2 = jnp.where(better, i, i2)
    return v1, i1, v2, i2


def _outproj_kernel(od_ref, of_ref, h_ref, wo_ref, gd_ref, gf_ref, lng_ref, lnb_ref,
                    wr_ref, br_ref, tri_ref,
                    h1_ref, h1t_ref, e_ref, gate_ref, rank_ref, cnt_ref, base_ref):
    step = pl.program_id(0)

    @pl.when(step == 0)
    def _():
        base_ref[...] = jnp.zeros_like(base_ref)

    xd = _rms_norm(od_ref[...].astype(F32), gd_ref[...])
    xf = _rms_norm(of_ref[...].astype(F32), gf_ref[...])
    y = _dot(xd.astype(MXU_DTYPE), wo_ref[0:D_DIL, :]) + _dot(xf.astype(MXU_DTYPE), wo_ref[D_DIL:, :])
    h1 = _layer_norm(DEEPNORM_ALPHA * h_ref[...] + y, lng_ref[...], lnb_ref[...])
    h1_ref[...] = h1
    _store_token_tiles(h1t_ref, h1)
    tm = h1.shape[0]

    h_hi, h_mid, _ = _split3(h1)
    two = _dot_nt(wr_ref[...], h_hi)
    logits = (two[:N_EXPERTS] + two[N_EXPERTS:]) + _dot_nt(wr_ref[0:N_EXPERTS, :], h_mid) + br_ref[...]
    logits = logits - jnp.max(logits, axis=0, keepdims=True)
    ex = jnp.exp(logits)
    probs = ex / jnp.sum(ex, axis=0, keepdims=True)
    pr = [probs[j:j + 1, :] for j in range(N_EXPERTS)]

    def group_score(g):
        v = pr[4 * g:4 * g + 4]
        pairs = [v[a] + v[b] for a in range(4) for b in range(a + 1, 4)]
        return functools.reduce(jnp.maximum, pairs)

    best = group_score(0)
    gsel = jnp.zeros((1, tm), jnp.int32)
    for g in range(1, N_GROUPS):
        sc = group_score(g)
        better = sc > best
        best = jnp.where(better, sc, best)
        gsel = jnp.where(better, g, gsel)
    in_grp = []
    for i in range(EXPERTS_PER_GROUP):
        v = pr[i]
        for g in range(1, N_GROUPS):
            v = jnp.where(gsel == g, pr[4 * g + i], v)
        in_grp.append(v)
    v1, i1, v2, i2 = _top2_of4(in_grp)
    e1 = gsel * EXPERTS_PER_GROUP + i1
    e2 = gsel * EXPERTS_PER_GROUP + i2
    den = v1 + v2
    e_ref[...] = jnp.concatenate([e1, e2], axis=0)
    gate_ref[...] = jnp.concatenate([v1 / den, v2 / den], axis=0)

    eidx = lax.broadcasted_iota(jnp.int32, (N_EXPERTS, tm), 0)
    oh1 = (eidx == e1).astype(F32)
    oh2 = (eidx == e2).astype(F32)
    tot = oh1 + oh2
    before = base_ref[...] + _dot(tot.astype(jnp.bfloat16), tri_ref[...])
    r1 = jnp.sum(oh1 * before, axis=0, keepdims=True)
    r2 = jnp.sum(oh2 * before, axis=0, keepdims=True)
    rank_ref[...] = jnp.concatenate([r1, r2], axis=0).astype(jnp.int32)
    base_ref[...] = base_ref[...] + jnp.sum(tot, axis=1, keepdims=True)
    cnt_ref[...] = jnp.broadcast_to(base_ref[...], cnt_ref.shape)


def _out_projection(od, of, h, w_out, g_dil, g_fox, ln_g, ln_b, wr3, b_router, tm=256):
    T = h.shape[0]
    tri = (np.arange(tm)[:, None] < np.arange(tm)[None, :]).astype(np.float32)
    rows = lambda width: pl.BlockSpec((tm, width), lambda i: (i, 0))
    full = lambda a: pl.BlockSpec(a.shape, lambda i: (0,) * a.ndim)
    tok = pl.BlockSpec((TOP_K, tm), lambda i: (0, i))
    consts = [w_out, g_dil.reshape(1, -1), g_fox.reshape(1, -1), ln_g.reshape(1, -1), ln_b.reshape(1, -1),
              wr3, b_router.reshape(-1, 1).astype(F32), jnp.asarray(tri, jnp.bfloat16)]
    return pl.pallas_call(
        _outproj_kernel,
        grid=(T // tm,),
        in_specs=[rows(D_DIL), rows(D_FOX), rows(D_MODEL)] + [full(a) for a in consts],
        out_specs=(rows(D_MODEL), pl.BlockSpec((tm * TOKEN_TILE_ROWS, LANES), lambda i: (i, 0)),
                   tok, tok, tok, pl.BlockSpec((N_EXPERTS, LANES), lambda i: (0, 0))),
        out_shape=(
            jax.ShapeDtypeStruct((T, D_MODEL), F32),
            jax.ShapeDtypeStruct((T * TOKEN_TILE_ROWS, LANES), F32),
            jax.ShapeDtypeStruct((TOP_K, T), jnp.int32),
            jax.ShapeDtypeStruct((TOP_K, T), F32),
            jax.ShapeDtypeStruct((TOP_K, T), jnp.int32),
            jax.ShapeDtypeStruct((N_EXPERTS, LANES), F32),
        ),
        scratch_shapes=[pltpu.VMEM((N_EXPERTS, 1), F32)],
        compiler_params=_cparams(1),
        name="out_projection_router",
    )(od, of, h, *consts)


def _store_token_tiles(ref, x):
    n = x.shape[0]
    for c in range(TOKEN_TILE_ROWS):
        ref[pl.ds(c, n, stride=TOKEN_TILE_ROWS), :] = x[:, c * LANES:(c + 1) * LANES]


def _load_token_tiles(ref, n):
    return jnp.concatenate([ref[pl.ds(c, n, stride=TOKEN_TILE_ROWS), :] for c in range(TOKEN_TILE_ROWS)], axis=1)


def _tile_copy(src, src_row, dst, dst_row, sem):
    return pltpu.make_async_copy(src.at[pl.ds(src_row, TOKEN_TILE_ROWS)],
                                 dst.at[pl.ds(dst_row, TOKEN_TILE_ROWS)], sem)


def _dispatch_kernel(p0_ref, p1_ref, h_ref, xg_in, xg_out, sem, *, td):
    del xg_in

    def copies(t):
        src = pl.multiple_of(t * TOKEN_TILE_ROWS, TOKEN_TILE_ROWS)
        return (_tile_copy(h_ref, src, xg_out, pl.multiple_of(p0_ref[t], TOKEN_TILE_ROWS), sem),
                _tile_copy(h_ref, src, xg_out, pl.multiple_of(p1_ref[t], TOKEN_TILE_ROWS), sem))

    def issue(t, _):
        for cp in copies(t):
            cp.start()
        return 0

    def drain(t, _):
        for cp in copies(t):
            cp.wait()
        return 0

    lax.fori_loop(0, td, issue, 0, unroll=DMA_ISSUE_UNROLL)
    lax.fori_loop(0, td, drain, 0, unroll=DMA_ISSUE_UNROLL)


def _dispatch(h1t, pos_rows, n_rows, td=256):
    T = h1t.shape[0] // TOKEN_TILE_ROWS
    smem = pl.BlockSpec((td,), lambda i: (i,), memory_space=pltpu.SMEM)
    anyspec = pl.BlockSpec(memory_space=pl.ANY)
    return pl.pallas_call(
        functools.partial(_dispatch_kernel, td=td),
        grid=(T // td,),
        in_specs=[smem, smem, pl.BlockSpec((td * TOKEN_TILE_ROWS, LANES), lambda i: (i, 0)), anyspec],
        out_specs=anyspec,
        out_shape=jax.ShapeDtypeStruct((n_rows * TOKEN_TILE_ROWS, LANES), F32),
        scratch_shapes=[pltpu.SemaphoreType.DMA(())],
        input_output_aliases={3: 0},
        compiler_params=_cparams(1),
        name="moe_dispatch",
    )(pos_rows[0], pos_rows[1], h1t, jnp.zeros((n_rows * TOKEN_TILE_ROWS, LANES), F32))


def _ffn_kernel(be_ref, nv_ref, x_ref, wg_ref, wu_ref, wd_ref, y_ref, *, tmf):
    del be_ref
    j = pl.program_id(0)

    @pl.when(j < nv_ref[0])
    def _():
        xb = _load_token_tiles(x_ref, tmf).astype(MXU_DTYPE)
        a = _dot(xb, wg_ref[0])
        u = _dot(xb, wu_ref[0])
        hmid = (a * jax.nn.sigmoid(a)) * u
        _store_token_tiles(y_ref, _dot(hmid.astype(MXU_DTYPE), wd_ref[0]))

    @pl.when(j >= nv_ref[0])
    def _():
        y_ref[...] = jnp.zeros_like(y_ref)


def _expert_ffn(xg, blk_e, n_valid, wg, wu, wd, tmf):
    n_rows = xg.shape[0] // TOKEN_TILE_ROWS
    D = D_MODEL
    nb = n_rows // tmf
    xmap = lambda j, be, nv: (jnp.minimum(j, nv[0] - 1), 0)
    wmap = lambda j, be, nv: (be[j], 0, 0)
    grid_spec = pltpu.PrefetchScalarGridSpec(
        num_scalar_prefetch=2,
        grid=(nb,),
        in_specs=[
            pl.BlockSpec((tmf * TOKEN_TILE_ROWS, LANES), xmap),
            pl.BlockSpec((1, D, D), wmap),
            pl.BlockSpec((1, D, D), wmap),
            pl.BlockSpec((1, D, D), wmap),
        ],
        out_specs=pl.BlockSpec((tmf * TOKEN_TILE_ROWS, LANES), lambda j, be, nv: (j, 0)),
    )
    return pl.pallas_call(
        functools.partial(_ffn_kernel, tmf=tmf),
        grid_spec=grid_spec,
        out_shape=jax.ShapeDtypeStruct(xg.shape, F32),
        compiler_params=_cparams(1),
        name="moe_expert_ffn",
    )(blk_e, n_valid, xg, wg, wu, wd)


def _combine_kernel(p0_ref, p1_ref, gate_ref, h_ref, lng_ref, lnb_ref, y_hbm, o_ref, buf0, buf1, sem, *, tc):
    def copies(t):
        dst = pl.multiple_of(t * TOKEN_TILE_ROWS, TOKEN_TILE_ROWS)
        return (_tile_copy(y_hbm, pl.multiple_of(p0_ref[t], TOKEN_TILE_ROWS), buf0, dst, sem),
                _tile_copy(y_hbm, pl.multiple_of(p1_ref[t], TOKEN_TILE_ROWS), buf1, dst, sem))

    def issue(t, _):
        for cp in copies(t):
            cp.start()
        return 0

    def drain(t, _):
        for cp in copies(t):
            cp.wait()
        return 0

    lax.fori_loop(0, tc, issue, 0, unroll=DMA_ISSUE_UNROLL)
    lax.fori_loop(0, tc, drain, 0, unroll=DMA_ISSUE_UNROLL)
    g = gate_ref[...]
    y = _load_token_tiles(buf0, tc) * g[:, 0:1] + _load_token_tiles(buf1, tc) * g[:, 1:2]
    o_ref[...] = _layer_norm(DEEPNORM_ALPHA * h_ref[...] + y, lng_ref[...], lnb_ref[...])


def _combine(yg, pos_rows, gates_col, h1, ln_g, ln_b, tc=256):
    T, D = h1.shape
    smem = pl.BlockSpec((tc,), lambda i: (i,), memory_space=pltpu.SMEM)
    rows = lambda width: pl.BlockSpec((tc, width), lambda i: (i, 0))
    vec = pl.BlockSpec((1, D), lambda i: (0, 0))
    buf = pltpu.VMEM((tc * TOKEN_TILE_ROWS, LANES), F32)
    return pl.pallas_call(
        functools.partial(_combine_kernel, tc=tc),
        grid=(T // tc,),
        in_specs=[smem, smem, rows(TOP_K), rows(D), vec, vec, pl.BlockSpec(memory_space=pl.ANY)],
        out_specs=rows(D),
        out_shape=jax.ShapeDtypeStruct((T, D), F32),
        scratch_shapes=[buf, buf, pltpu.SemaphoreType.DMA(())],
        compiler_params=_cparams(1),
        name="moe_combine_ln",
    )(pos_rows[0], pos_rows[1], gates_col, h1, ln_g.reshape(1, -1), ln_b.reshape(1, -1), yg)


def _grouped_moe(h1, h1t, experts, gates, ranks, counts, wg, wu, wd, ln_g, ln_b, tmf=256):
    T = h1.shape[0]
    nb = (T * TOP_K) // tmf + N_EXPERTS
    cnt = counts[:, 0].astype(jnp.int32)
    pcnt = (cnt + tmf - 1) // tmf * tmf
    pends = jnp.cumsum(pcnt)
    pstart = pends - pcnt
    seg = jnp.zeros_like(experts)
    for e in range(N_EXPERTS):
        seg = jnp.where(experts == e, pstart[e], seg)
    pos_rows = (seg + ranks) * TOKEN_TILE_ROWS
    blk_e = jnp.minimum(jnp.searchsorted(pends, jnp.arange(nb, dtype=jnp.int32) * tmf, side="right"),
                        N_EXPERTS - 1).astype(jnp.int32)
    n_valid = (pends[-1:] // tmf).astype(jnp.int32)
    xg = _dispatch(h1t, pos_rows, nb * tmf)
    yg = _expert_ffn(xg, blk_e, n_valid, wg, wu, wd, tmf)
    return _combine(yg, pos_rows, gates.T, h1, ln_g, ln_b)


def _rope_tables(S):
    half = HEAD_DIM // 2
    inv = ROPE_THETA ** (-jnp.arange(half, dtype=F32) / half)
    ang = jnp.arange(S, dtype=F32)[:, None] * inv[None, :]
    cos, sin = jnp.cos(ang), jnp.sin(ang)
    reps = LANES // HEAD_DIM
    cos_tab = jnp.tile(jnp.concatenate([cos, cos], axis=1), (1, reps))
    sin_tab = jnp.tile(jnp.concatenate([-sin, sin], axis=1), (1, reps))
    return cos_tab, sin_tab


def _pad_w_in(w):
    pad = jnp.zeros((D_MODEL, D_PROJ_PAD - w.shape[1]), w.dtype)
    return jnp.concatenate([w, pad], axis=1).astype(MXU_DTYPE)


def kernel(x, w_in, b_forget, g_dil, g_fox, w_out, ln1_g, ln1_b, w_router, b_router,
           w_gate, w_up, w_down, ln2_g, ln2_b):
    B, S, D = x.shape
    T = B * S
    cos_tab, sin_tab = _rope_tables(S)
    wr_hi, wr_mid, _ = _split3(w_router.astype(F32).T)
    wr3 = jnp.concatenate([wr_hi, wr_mid], axis=0)
    h = x
    for l in range(DEPTH):
        qd, kd, vd, qf, kf, vf, zf = _in_projection(h, _pad_w_in(w_in[l]), cos_tab, sin_tab)
        zt = zf[:, :, :N_HEADS_FOX].transpose(0, 2, 1)
        c = _forget_cumsum(zt, b_forget[l])
        o_fox = _fox_attention(qf, kf, vf, c)
        o_dil = _dilated_attention(qd, kd, vd)
        h1, h1t, experts, gates, ranks, counts = _out_projection(
            o_dil.reshape(T, D_DIL), o_fox.reshape(T, D_FOX), h.reshape(T, D),
            w_out[l].astype(MXU_DTYPE), g_dil[l], g_fox[l], ln1_g[l], ln1_b[l], wr3, b_router)
        h2 = _grouped_moe(h1, h1t, experts, gates, ranks, counts,
                          w_gate[l].astype(MXU_DTYPE), w_up[l].astype(MXU_DTYPE), w_down[l].astype(MXU_DTYPE),
                          ln2_g[l], ln2_b[l])
        h = h2.reshape(B, S, D)
    return h
```

```python
import functools

import numpy as np
import jax
import jax.numpy as jnp
from jax import lax
from jax.experimental import pallas as pl
from jax.experimental.pallas import tpu as pltpu

D_MODEL = 1024
DEPTH = 2
HEAD_DIM = 64
N_HEADS_DIL = 12
N_HEADS_FOX = 4
D_DIL = N_HEADS_DIL * HEAD_DIM
D_FOX = N_HEADS_FOX * HEAD_DIM
DILATIONS = (1, 4, 16)
BLOCK = 128
ROPE_THETA = 10000.0
N_EXPERTS = 16
N_GROUPS = 4
EXPERTS_PER_GROUP = 4
TOP_K = 2
DEEPNORM_ALPHA = (2.0 * DEPTH) ** 0.25
LN_EPS = 1e-5
RMS_EPS = 1e-6

LANES = 128
N_PAIRS_DIL = D_DIL // LANES
D_PROJ_PAD = 3 * D_DIL + 3 * D_FOX + LANES
VMEM_LIMIT = 48 * 1024 * 1024
TOKEN_TILE_ROWS = D_MODEL // LANES
DMA_ISSUE_UNROLL = 8
DIL_PAD = max(DILATIONS) * BLOCK
MASK_BIAS = -1e30
LOG2_E = 1.4426950408889634
DIL_UNITS_PER_BODY = 8
FOX_TK = 256

MXU_DTYPE = jnp.bfloat16
F32 = jnp.float32
NEG_INF = float("-inf")


def _cparams(n_axes):
    return pltpu.CompilerParams(dimension_semantics=("arbitrary",) * n_axes,
                                vmem_limit_bytes=VMEM_LIMIT)


def _dot(a, b):
    return jnp.dot(a, b, preferred_element_type=F32)


def _dot_nt(a, b):
    return lax.dot_general(a, b, (((1,), (1,)), ((), ())), preferred_element_type=F32)


def _inproj_kernel(x_ref, w_ref, cos_ref, sin_ref,
                   qd_ref, kd_ref, vd_ref, qf_ref, kf_ref, vf_ref, zf_ref):
    xb = x_ref[0].astype(MXU_DTYPE)
    tm = xb.shape[0]
    cos = cos_ref[...]
    sin = sin_ref[...]
    lane = lax.broadcasted_iota(jnp.int32, (tm, LANES), 1)
    first_half = (lane % HEAD_DIM) < (HEAD_DIM // 2)

    def rope(z):
        rot = jnp.where(first_half, pltpu.roll(z, LANES - 32, 1), pltpu.roll(z, 32, 1))
        return z * cos + rot * sin

    def proj(col, width):
        return _dot(xb, w_ref[:, col:col + width])

    scale = HEAD_DIM ** -0.5
    col = 0
    for out_ref, roped, mul in ((qd_ref, True, scale * LOG2_E), (kd_ref, True, 1.0)):
        for c in range(D_DIL // 256):
            z = proj(col, 256)
            for half in range(2):
                zz = rope(z[:, half * LANES:(half + 1) * LANES]) * mul
                out_ref[0, :, c * 256 + half * LANES:c * 256 + (half + 1) * LANES] = zz.astype(out_ref.dtype)
            col += 256
    for c in range(D_DIL // 256):
        vd_ref[0, :, c * 256:(c + 1) * 256] = proj(col, 256).astype(vd_ref.dtype)
        col += 256
    qf_ref[0] = (proj(col, D_FOX) * (scale * LOG2_E)).astype(qf_ref.dtype)
    col += D_FOX
    kf_ref[0] = proj(col, D_FOX).astype(kf_ref.dtype)
    col += D_FOX
    vf_ref[0] = proj(col, D_FOX).astype(vf_ref.dtype)
    col += D_FOX
    zf_ref[0] = proj(col, LANES)


def _in_projection(h, w_pad, cos_tab, sin_tab, tm=512):
    B, S, D = h.shape
    grid = (B, S // tm)
    row = lambda width: pl.BlockSpec((1, tm, width), lambda b, i: (b, i, 0))
    out_shape = (
        jax.ShapeDtypeStruct((B, S, D_DIL), MXU_DTYPE),
        jax.ShapeDtypeStruct((B, S, D_DIL), MXU_DTYPE),
        jax.ShapeDtypeStruct((B, S, D_DIL), MXU_DTYPE),
        jax.ShapeDtypeStruct((B, S, D_FOX), MXU_DTYPE),
        jax.ShapeDtypeStruct((B, S, D_FOX), MXU_DTYPE),
        jax.ShapeDtypeStruct((B, S, D_FOX), MXU_DTYPE),
        jax.ShapeDtypeStruct((B, S, LANES), F32),
    )
    return pl.pallas_call(
        _inproj_kernel,
        grid=grid,
        in_specs=[
            row(D),
            pl.BlockSpec((D, D_PROJ_PAD), lambda b, i: (0, 0)),
            pl.BlockSpec((tm, LANES), lambda b, i: (i, 0)),
            pl.BlockSpec((tm, LANES), lambda b, i: (i, 0)),
        ],
        out_specs=(row(D_DIL), row(D_DIL), row(D_DIL), row(D_FOX), row(D_FOX), row(D_FOX), row(LANES)),
        out_shape=out_shape,
        compiler_params=_cparams(2),
        name="in_projection",
    )(h, w_pad, cos_tab, sin_tab)


def _split3(x):
    hi = x.astype(jnp.bfloat16)
    r1 = x - hi.astype(F32)
    mid = r1.astype(jnp.bfloat16)
    lo = (r1 - mid.astype(F32)).astype(jnp.bfloat16)
    return hi, mid, lo


def _forget_cumsum_kernel(z_ref, b_ref, tri_ref, c_ref):
    x = z_ref[0] + b_ref[...]
    logf = jnp.minimum(x, 0.0) - jnp.log1p(jnp.exp(-jnp.abs(x)))
    tri = tri_ref[...]
    S = x.shape[1]
    carry = jnp.zeros((x.shape[0], 1), F32)
    for blk in range(S // LANES):
        seg = logf[:, blk * LANES:(blk + 1) * LANES]
        hi, mid, lo = _split3(seg)
        cs = (_dot(hi, tri) + _dot(mid, tri)) + _dot(lo, tri) + carry
        c_ref[0, :, blk * LANES:(blk + 1) * LANES] = cs * LOG2_E
        carry = cs[:, LANES - 1:LANES]


def _forget_cumsum(zt, b_forget):
    B, Hf, S = zt.shape
    tri = (np.arange(LANES)[:, None] <= np.arange(LANES)[None, :]).astype(np.float32)
    return pl.pallas_call(
        _forget_cumsum_kernel,
        grid=(B,),
        in_specs=[
            pl.BlockSpec((1, Hf, S), lambda b: (b, 0, 0)),
            pl.BlockSpec((Hf, 1), lambda b: (0, 0)),
            pl.BlockSpec((LANES, LANES), lambda b: (0, 0)),
        ],
        out_specs=pl.BlockSpec((1, Hf, S), lambda b: (b, 0, 0)),
        out_shape=jax.ShapeDtypeStruct((B, Hf, S), F32),
        compiler_params=_cparams(1),
        name="forget_cumsum",
    )(zt, b_forget.reshape(Hf, 1).astype(F32), jnp.asarray(tri, jnp.bfloat16))


def _pair_rows(xp, is_a):
    zero = jnp.zeros_like(xp)
    return jnp.concatenate([jnp.where(is_a, xp, zero), jnp.where(is_a, zero, xp)], axis=0)


def _causal_bias():
    qi = np.arange(BLOCK)[:, None]
    ki = np.arange(FOX_TK)[None, :]
    vis = np.stack([ki <= qi, ki <= qi + BLOCK])
    return np.where(vis, 0.0, MASK_BIAS).astype(np.float32)


def _fox_kernel(q_ref, k_ref, v_ref, crow_ref, cbias_ref, o_ref):
    S = q_ref.shape[1]
    tk = FOX_TK
    lane = lax.broadcasted_iota(jnp.int32, (BLOCK, LANES), 1)
    is_a = lane < HEAD_DIM
    ones = jnp.ones((tk, LANES), MXU_DTYPE)

    def unit(q2, kv, bias):
        kb, v_ones, cr = kv
        s = _dot_nt(q2, kb)
        sa = s[:BLOCK] - cr[0:1]
        sb = s[BLOCK:] - cr[1:2]
        if bias is not None:
            sa = sa + bias
            sb = sb + bias
        s = jnp.concatenate([sa, sb], axis=0)
        m = jnp.max(s, axis=1, keepdims=True)
        p = jnp.exp2(s - m).astype(MXU_DTYPE)
        o = _dot(p, v_ones)
        acc = jnp.where(is_a, o[:BLOCK, :LANES], o[BLOCK:, :LANES])
        l = jnp.where(is_a, o[:BLOCK, LANES:], o[BLOCK:, LANES:])
        return acc, l, jnp.where(is_a, m[:BLOCK], m[BLOCK:])

    def key_block(j):
        k0 = pl.multiple_of(j * tk, tk)
        return (k_ref[0, pl.ds(k0, tk), :],
                jnp.concatenate([v_ref[0, pl.ds(k0, tk), :], ones], axis=1),
                crow_ref[0, 0, :, pl.ds(k0, tk)])

    def fold(state, units):
        num, den, top_old = state
        top = functools.reduce(jnp.maximum, [top_old] + [m for _, _, m in units])
        scale = jnp.exp2(top_old - top)
        num, den = scale * num, scale * den
        for acc, l, m in units:
            w = jnp.exp2(m - top)
            num = num + w * acc
            den = den + w * l
        return num, den, top

    def q_block(i, _):
        r0 = pl.multiple_of(i * tk, tk)
        q2 = [_pair_rows(q_ref[0, pl.ds(r0 + h * BLOCK, BLOCK), :], is_a) for h in range(2)]
        zero = jnp.zeros((BLOCK, LANES), F32)
        empty = (zero, zero, jnp.full((BLOCK, LANES), MASK_BIAS, F32))

        def fold_blocks(states, blocks):
            units = [[unit(q2[h], kv, None if bias is None else bias[h]) for h in range(2)]
                     for kv, bias in ((key_block(j), bias) for j, bias in blocks)]
            return tuple(fold(states[h], [u[h] for u in units]) for h in range(2))

        def two_blocks(g, states):
            return fold_blocks(states, [(2 * g, None), (2 * g + 1, None)])

        states = lax.fori_loop(0, lax.shift_right_logical(i, 1), two_blocks, (empty, empty))
        diag = (i, (cbias_ref[0], cbias_ref[1]))
        states = lax.cond((i & 1) == 1,
                          lambda st: fold_blocks(st, [(i - 1, None), diag]),
                          lambda st: fold_blocks(st, [diag]), states)
        for h in range(2):
            num, den, _ = states[h]
            o_ref[0, pl.ds(r0 + h * BLOCK, BLOCK), :] = (num / den).astype(o_ref.dtype)
        return 0

    lax.fori_loop(0, S // tk, q_block, 0)


def _fox_attention(qf, kf, vf, c2):
    B, S, _ = qf.shape
    n_pairs = D_FOX // LANES
    c4 = c2.reshape(B, n_pairs, 2, S)
    blk = pl.BlockSpec((1, S, LANES), lambda b, p: (b, 0, p))
    return pl.pallas_call(
        _fox_kernel,
        grid=(B, n_pairs),
        in_specs=[blk, blk, blk,
                  pl.BlockSpec((1, 1, 2, S), lambda b, p: (b, p, 0, 0)),
                  pl.BlockSpec((2, BLOCK, FOX_TK), lambda b, p: (0, 0, 0))],
        out_specs=blk,
        out_shape=jax.ShapeDtypeStruct((B, S, D_FOX), MXU_DTYPE),
        compiler_params=_cparams(2),
        name="fox_attention",
    )(qf, kf, vf, c4, jnp.asarray(_causal_bias()))


def _band_bias():
    qi = np.arange(BLOCK)[:, None]
    ki = np.arange(2 * BLOCK)[None, :]
    delta = BLOCK + qi - ki
    band = (delta >= 0) & (delta <= BLOCK)
    first = band & (ki >= BLOCK)
    return np.where(np.stack([first, band]), 0.0, MASK_BIAS).astype(np.float32)


def _dil_kernel(q_ref, k_ref, v_ref, bias_ref, o_ref, q32, k32, v32, onorm_s, mu_s):
    S = q_ref.shape[1]
    q32[...] = q_ref[0].astype(F32)
    zeros = jnp.zeros((DIL_PAD, LANES), F32)
    k32[0:DIL_PAD, :] = zeros
    v32[0:DIL_PAD, :] = zeros
    k32[DIL_PAD:, :] = k_ref[0].astype(F32)
    v32[DIL_PAD:, :] = v_ref[0].astype(F32)

    lane = lax.broadcasted_iota(jnp.int32, (BLOCK, LANES), 1)
    is_a = lane < HEAD_DIM
    ones = jnp.ones((2 * BLOCK, LANES), MXU_DTYPE)

    def rows(ref, start, count, dil):
        if dil == 1:
            return ref[pl.ds(start, count), :]
        return ref[pl.ds(start, count, stride=dil), :]

    def unit(dil, rho, n):
        q0 = rho + dil * BLOCK * n
        qb = rows(q32, q0, BLOCK, dil).astype(MXU_DTYPE)
        kw = rows(k32, q0 + DIL_PAD - dil * BLOCK, 2 * BLOCK, dil).astype(MXU_DTYPE)
        vw = rows(v32, q0 + DIL_PAD - dil * BLOCK, 2 * BLOCK, dil).astype(MXU_DTYPE)
        bias = bias_ref[jnp.minimum(n, 1)]
        s = _dot_nt(_pair_rows(qb, is_a), kw)
        s = jnp.concatenate([s[:BLOCK] + bias, s[BLOCK:] + bias], axis=0)
        m = jnp.max(s, axis=1, keepdims=True)
        p = jnp.exp2(s - m).astype(MXU_DTYPE)
        o = _dot(p, jnp.concatenate([vw, ones], axis=1))
        acc = jnp.where(is_a, o[:BLOCK, :LANES], o[BLOCK:, :LANES])
        l = jnp.where(is_a, o[:BLOCK, LANES:], o[BLOCK:, LANES:])
        mu = jnp.where(is_a, m[:BLOCK], m[BLOCK:]) + jnp.log2(l)
        return acc / l, mu

    n_units = S // BLOCK

    for slab, dil in enumerate(d for d in DILATIONS if d != 1):
        blk_bits = (n_units // dil).bit_length() - 1

        def several_units(g, _, slab=slab, dil=dil, blk_bits=blk_bits):
            for u in range(DIL_UNITS_PER_BODY):
                t = g * DIL_UNITS_PER_BODY + u
                rho = lax.shift_right_logical(t, blk_bits)
                n = t & ((1 << blk_bits) - 1)
                o_n, mu = unit(dil, rho, n)
                q0 = rho + dil * BLOCK * n
                onorm_s[slab, pl.ds(q0, BLOCK, stride=dil), :] = o_n
                mu_s[slab, pl.ds(q0, BLOCK, stride=dil), :] = mu
            return 0

        lax.fori_loop(0, n_units // DIL_UNITS_PER_BODY, several_units, 0)

    def merge_blocks(g, _):
        for u in range(DIL_UNITS_PER_BODY):
            n = g * DIL_UNITS_PER_BODY + u
            o_1, mu_1 = unit(1, 0, n)
            r0 = pl.multiple_of(n * BLOCK, BLOCK)
            others = [(onorm_s[sl, pl.ds(r0, BLOCK), :], mu_s[sl, pl.ds(r0, BLOCK), :])
                      for sl in range(len(DILATIONS) - 1)]
            top = functools.reduce(jnp.maximum, [mu_1] + [mu for _, mu in others])
            w = jnp.exp2(mu_1 - top)
            num, den = w * o_1, w
            for o_p, mu_p in others:
                w = jnp.exp2(mu_p - top)
                num = num + w * o_p
                den = den + w
            o_ref[0, pl.ds(r0, BLOCK), :] = (num / den).astype(o_ref.dtype)
        return 0

    lax.fori_loop(0, n_units // DIL_UNITS_PER_BODY, merge_blocks, 0)


def _dilated_attention(qd, kd, vd):
    B, S, _ = qd.shape
    blk = pl.BlockSpec((1, S, LANES), lambda b, p: (b, 0, p))
    n_slabs = len(DILATIONS) - 1
    return pl.pallas_call(
        _dil_kernel,
        grid=(B, N_PAIRS_DIL),
        in_specs=[blk, blk, blk, pl.BlockSpec((2, BLOCK, 2 * BLOCK), lambda b, p: (0, 0, 0))],
        out_specs=blk,
        out_shape=jax.ShapeDtypeStruct((B, S, D_DIL), MXU_DTYPE),
        scratch_shapes=[
            pltpu.VMEM((S, LANES), F32),
            pltpu.VMEM((DIL_PAD + S, LANES), F32),
            pltpu.VMEM((DIL_PAD + S, LANES), F32),
            pltpu.VMEM((n_slabs, S, LANES), F32),
            pltpu.VMEM((n_slabs, S, LANES), F32),
        ],
        compiler_params=_cparams(2),
        name="dilated_attention",
    )(qd, kd, vd, jnp.asarray(_band_bias()))


def _layer_norm(u, g, b):
    mu = jnp.mean(u, axis=1, keepdims=True)
    d = u - mu
    var = jnp.mean(d * d, axis=1, keepdims=True)
    return d * lax.rsqrt(var + LN_EPS) * g + b


def _rms_norm(x, g):
    ms = jnp.mean(x * x, axis=1, keepdims=True)
    return x * lax.rsqrt(ms + RMS_EPS) * g


def _top2_of4(vals):
    v1, i1 = vals[0], jnp.zeros(vals[0].shape, jnp.int32)
    for i in range(1, 4):
        better = vals[i] > v1
        v1 = jnp.where(better, vals[i], v1)
        i1 = jnp.where(better, i, i1)
    v2 = jnp.full(vals[0].shape, -1.0, F32)
    i2 = jnp.zeros(vals[0].shape, jnp.int32)
    for i in range(4):
        better = (vals[i] > v2) & (i1 != i)
        v2 = jnp.where(better, vals[i], v2)
        i2 = jnp.where(better, i, i2)
    return v1, i1, v2, i2


def _outproj_kernel(od_ref, of_ref, h_ref, wo_ref, gd_ref, gf_ref, lng_ref, lnb_ref,
                    wr_ref, br_ref, tri_ref,
                    h1_ref, h1t_ref, e_ref, gate_ref, rank_ref, cnt_ref, base_ref):
    step = pl.program_id(0)

    @pl.when(step == 0)
    def _():
        base_ref[...] = jnp.zeros_like(base_ref)

    xd = _rms_norm(od_ref[...].astype(F32), gd_ref[...])
    xf = _rms_norm(of_ref[...].astype(F32), gf_ref[...])
    y = _dot(xd.astype(MXU_DTYPE), wo_ref[0:D_DIL, :]) + _dot(xf.astype(MXU_DTYPE), wo_ref[D_DIL:, :])
    h1 = _layer_norm(DEEPNORM_ALPHA * h_ref[...] + y, lng_ref[...], lnb_ref[...])
    h1_ref[...] = h1
    _store_token_tiles(h1t_ref, h1)
    tm = h1.shape[0]

    h_hi, h_mid, _ = _split3(h1)
    two = _dot_nt(wr_ref[...], h_hi)
    logits = (two[:N_EXPERTS] + two[N_EXPERTS:]) + _dot_nt(wr_ref[0:N_EXPERTS, :], h_mid) + br_ref[...]
    logits = logits - jnp.max(logits, axis=0, keepdims=True)
    ex = jnp.exp(logits)
    probs = ex / jnp.sum(ex, axis=0, keepdims=True)
    pr = [probs[j:j + 1, :] for j in range(N_EXPERTS)]

    def group_score(g):
        v = pr[4 * g:4 * g + 4]
        pairs = [v[a] + v[b] for a in range(4) for b in range(a + 1, 4)]
        return functools.reduce(jnp.maximum, pairs)

    best = group_score(0)
    gsel = jnp.zeros((1, tm), jnp.int32)
    for g in range(1, N_GROUPS):
        sc = group_score(g)
        better = sc > best
        best = jnp.where(better, sc, best)
        gsel = jnp.where(better, g, gsel)
    in_grp = []
    for i in range(EXPERTS_PER_GROUP):
        v = pr[i]
        for g in range(1, N_GROUPS):
            v = jnp.where(gsel == g, pr[4 * g + i], v)
        in_grp.append(v)
    v1, i1, v2, i2 = _top2_of4(in_grp)
    e1 = gsel * EXPERTS_PER_GROUP + i1
    e2 = gsel * EXPERTS_PER_GROUP + i2
    den = v1 + v2
    e_ref[...] = jnp.concatenate([e1, e2], axis=0)
    gate_ref[...] = jnp.concatenate([v1 / den, v2 / den], axis=0)

    eidx = lax.broadcasted_iota(jnp.int32, (N_EXPERTS, tm), 0)
    oh1 = (eidx == e1).astype(F32)
    oh2 = (eidx == e2).astype(F32)
    tot = oh1 + oh2
    before = base_ref[...] + _dot(tot.astype(jnp.bfloat16), tri_ref[...])
    r1 = jnp.sum(oh1 * before, axis=0, keepdims=True)
    r2 = jnp.sum(oh2 * before, axis=0, keepdims=True)
    rank_ref[...] = jnp.concatenate([r1, r2], axis=0).astype(jnp.int32)
    base_ref[...] = base_ref[...] + jnp.sum(tot, axis=1, keepdims=True)
    cnt_ref[...] = jnp.broadcast_to(base_ref[...], cnt_ref.shape)


def _out_projection(od, of, h, w_out, g_dil, g_fox, ln_g, ln_b, wr3, b_router, tm=256):
    T = h.shape[0]
    tri = (np.arange(tm)[:, None] < np.arange(tm)[None, :]).astype(np.float32)
    rows = lambda width: pl.BlockSpec((tm, width), lambda i: (i, 0))
    full = lambda a: pl.BlockSpec(a.shape, lambda i: (0,) * a.ndim)
    tok = pl.BlockSpec((TOP_K, tm), lambda i: (0, i))
    consts = [w_out, g_dil.reshape(1, -1), g_fox.reshape(1, -1), ln_g.reshape(1, -1), ln_b.reshape(1, -1),
              wr3, b_router.reshape(-1, 1).astype(F32), jnp.asarray(tri, jnp.bfloat16)]
    return pl.pallas_call(
        _outproj_kernel,
        grid=(T // tm,),
        in_specs=[rows(D_DIL), rows(D_FOX), rows(D_MODEL)] + [full(a) for a in consts],
        out_specs=(rows(D_MODEL), pl.BlockSpec((tm * TOKEN_TILE_ROWS, LANES), lambda i: (i, 0)),
                   tok, tok, tok, pl.BlockSpec((N_EXPERTS, LANES), lambda i: (0, 0))),
        out_shape=(
            jax.ShapeDtypeStruct((T, D_MODEL), F32),
            jax.ShapeDtypeStruct((T * TOKEN_TILE_ROWS, LANES), F32),
            jax.ShapeDtypeStruct((TOP_K, T), jnp.int32),
            jax.ShapeDtypeStruct((TOP_K, T), F32),
            jax.ShapeDtypeStruct((TOP_K, T), jnp.int32),
            jax.ShapeDtypeStruct((N_EXPERTS, LANES), F32),
        ),
        scratch_shapes=[pltpu.VMEM((N_EXPERTS, 1), F32)],
        compiler_params=_cparams(1),
        name="out_projection_router",
    )(od, of, h, *consts)


def _store_token_tiles(ref, x):
    n = x.shape[0]
    for c in range(TOKEN_TILE_ROWS):
        ref[pl.ds(c, n, stride=TOKEN_TILE_ROWS), :] = x[:, c * LANES:(c + 1) * LANES]


def _load_token_tiles(ref, n):
    return jnp.concatenate([ref[pl.ds(c, n, stride=TOKEN_TILE_ROWS), :] for c in range(TOKEN_TILE_ROWS)], axis=1)


def _tile_copy(src, src_row, dst, dst_row, sem):
    return pltpu.make_async_copy(src.at[pl.ds(src_row, TOKEN_TILE_ROWS)],
                                 dst.at[pl.ds(dst_row, TOKEN_TILE_ROWS)], sem)


def _dispatch_kernel(p0_ref, p1_ref, h_ref, xg_in, xg_out, sem, *, td):
    del xg_in

    def copies(t):
        src = pl.multiple_of(t * TOKEN_TILE_ROWS, TOKEN_TILE_ROWS)
        return (_tile_copy(h_ref, src, xg_out, pl.multiple_of(p0_ref[t], TOKEN_TILE_ROWS), sem),
                _tile_copy(h_ref, src, xg_out, pl.multiple_of(p1_ref[t], TOKEN_TILE_ROWS), sem))

    def issue(t, _):
        for cp in copies(t):
            cp.start()
        return 0

    def drain(t, _):
        for cp in copies(t):
            cp.wait()
        return 0

    lax.fori_loop(0, td, issue, 0, unroll=DMA_ISSUE_UNROLL)
    lax.fori_loop(0, td, drain, 0, unroll=DMA_ISSUE_UNROLL)


def _dispatch(h1t, pos_rows, n_rows, td=256):
    T = h1t.shape[0] // TOKEN_TILE_ROWS
    smem = pl.BlockSpec((td,), lambda i: (i,), memory_space=pltpu.SMEM)
    anyspec = pl.BlockSpec(memory_space=pl.ANY)
    return pl.pallas_call(
        functools.partial(_dispatch_kernel, td=td),
        grid=(T // td,),
        in_specs=[smem, smem, pl.BlockSpec((td * TOKEN_TILE_ROWS, LANES), lambda i: (i, 0)), anyspec],
        out_specs=anyspec,
        out_shape=jax.ShapeDtypeStruct((n_rows * TOKEN_TILE_ROWS, LANES), F32),
        scratch_shapes=[pltpu.SemaphoreType.DMA(())],
        input_output_aliases={3: 0},
        compiler_params=_cparams(1),
        name="moe_dispatch",
    )(pos_rows[0], pos_rows[1], h1t, jnp.zeros((n_rows * TOKEN_TILE_ROWS, LANES), F32))


def _ffn_kernel(be_ref, nv_ref, x_ref, wg_ref, wu_ref, wd_ref, y_ref, *, tmf):
    del be_ref
    j = pl.program_id(0)

    @pl.when(j < nv_ref[0])
    def _():
        xb = _load_token_tiles(x_ref, tmf).astype(MXU_DTYPE)
        a = _dot(xb, wg_ref[0])
        u = _dot(xb, wu_ref[0])
        hmid = (a * jax.nn.sigmoid(a)) * u
        _store_token_tiles(y_ref, _dot(hmid.astype(MXU_DTYPE), wd_ref[0]))

    @pl.when(j >= nv_ref[0])
    def _():
        y_ref[...] = jnp.zeros_like(y_ref)


def _expert_ffn(xg, blk_e, n_valid, wg, wu, wd, tmf):
    n_rows = xg.shape[0] // TOKEN_TILE_ROWS
    D = D_MODEL
    nb = n_rows // tmf
    xmap = lambda j, be, nv: (jnp.minimum(j, nv[0] - 1), 0)
    wmap = lambda j, be, nv: (be[j], 0, 0)
    grid_spec = pltpu.PrefetchScalarGridSpec(
        num_scalar_prefetch=2,
        grid=(nb,),
        in_specs=[
            pl.BlockSpec((tmf * TOKEN_TILE_ROWS, LANES), xmap),
            pl.BlockSpec((1, D, D), wmap),
            pl.BlockSpec((1, D, D), wmap),
            pl.BlockSpec((1, D, D), wmap),
        ],
        out_specs=pl.BlockSpec((tmf * TOKEN_TILE_ROWS, LANES), lambda j, be, nv: (j, 0)),
    )
    return pl.pallas_call(
        functools.partial(_ffn_kernel, tmf=tmf),
        grid_spec=grid_spec,
        out_shape=jax.ShapeDtypeStruct(xg.shape, F32),
        compiler_params=_cparams(1),
        name="moe_expert_ffn",
    )(blk_e, n_valid, xg, wg, wu, wd)


def _combine_kernel(p0_ref, p1_ref, gate_ref, h_ref, lng_ref, lnb_ref, y_hbm, o_ref, buf0, buf1, sem, *, tc):
    def copies(t):
        dst = pl.multiple_of(t * TOKEN_TILE_ROWS, TOKEN_TILE_ROWS)
        return (_tile_copy(y_hbm, pl.multiple_of(p0_ref[t], TOKEN_TILE_ROWS), buf0, dst, sem),
                _tile_copy(y_hbm, pl.multiple_of(p1_ref[t], TOKEN_TILE_ROWS), buf1, dst, sem))

    def issue(t, _):
        for cp in copies(t):
            cp.start()
        return 0

    def drain(t, _):
        for cp in copies(t):
            cp.wait()
        return 0

    lax.fori_loop(0, tc, issue, 0, unroll=DMA_ISSUE_UNROLL)
    lax.fori_loop(0, tc, drain, 0, unroll=DMA_ISSUE_UNROLL)
    g = gate_ref[...]
    y = _load_token_tiles(buf0, tc) * g[:, 0:1] + _load_token_tiles(buf1, tc) * g[:, 1:2]
    o_ref[...] = _layer_norm(DEEPNORM_ALPHA * h_ref[...] + y, lng_ref[...], lnb_ref[...])


def _combine(yg, pos_rows, gates_col, h1, ln_g, ln_b, tc=256):
    T, D = h1.shape
    smem = pl.BlockSpec((tc,), lambda i: (i,), memory_space=pltpu.SMEM)
    rows = lambda width: pl.BlockSpec((tc, width), lambda i: (i, 0))
    vec = pl.BlockSpec((1, D), lambda i: (0, 0))
    buf = pltpu.VMEM((tc * TOKEN_TILE_ROWS, LANES), F32)
    return pl.pallas_call(
        functools.partial(_combine_kernel, tc=tc),
        grid=(T // tc,),
        in_specs=[smem, smem, rows(TOP_K), rows(D), vec, vec, pl.BlockSpec(memory_space=pl.ANY)],
        out_specs=rows(D),
        out_shape=jax.ShapeDtypeStruct((T, D), F32),
        scratch_shapes=[buf, buf, pltpu.SemaphoreType.DMA(())],
        compiler_params=_cparams(1),
        name="moe_combine_ln",
    )(pos_rows[0], pos_rows[1], gates_col, h1, ln_g.reshape(1, -1), ln_b.reshape(1, -1), yg)


def _grouped_moe(h1, h1t, experts, gates, ranks, counts, wg, wu, wd, ln_g, ln_b, tmf=256):
    T = h1.shape[0]
    nb = (T * TOP_K) // tmf + N_EXPERTS
    cnt = counts[:, 0].astype(jnp.int32)
    pcnt = (cnt + tmf - 1) // tmf * tmf
    pends = jnp.cumsum(pcnt)
    pstart = pends - pcnt
    seg = jnp.zeros_like(experts)
    for e in range(N_EXPERTS):
        seg = jnp.where(experts == e, pstart[e], seg)
    pos_rows = (seg + ranks) * TOKEN_TILE_ROWS
    blk_e = jnp.minimum(jnp.searchsorted(pends, jnp.arange(nb, dtype=jnp.int32) * tmf, side="right"),
                        N_EXPERTS - 1).astype(jnp.int32)
    n_valid = (pends[-1:] // tmf).astype(jnp.int32)
    xg = _dispatch(h1t, pos_rows, nb * tmf)
    yg = _expert_ffn(xg, blk_e, n_valid, wg, wu, wd, tmf)
    return _combine(yg, pos_rows, gates.T, h1, ln_g, ln_b)


def _rope_tables(S):
    half = HEAD_DIM // 2
    inv = ROPE_THETA ** (-jnp.arange(half, dtype=F32) / half)
    ang = jnp.arange(S, dtype=F32)[:, None] * inv[None, :]
    cos, sin = jnp.cos(ang), jnp.sin(ang)
    reps = LANES // HEAD_DIM
    cos_tab = jnp.tile(jnp.concatenate([cos, cos], axis=1), (1, reps))
    sin_tab = jnp.tile(jnp.concatenate([-sin, sin], axis=1), (1, reps))
    return cos_tab, sin_tab


def _pad_w_in(w):
    pad = jnp.zeros((D_MODEL, D_PROJ_PAD - w.shape[1]), w.dtype)
    return jnp.concatenate([w, pad], axis=1).astype(MXU_DTYPE)


def kernel(x, w_in, b_forget, g_dil, g_fox, w_out, ln1_g, ln1_b, w_router, b_router,
           w_gate, w_up, w_down, ln2_g, ln2_b):
    B, S, D = x.shape
    T = B * S
    cos_tab, sin_tab = _rope_tables(S)
    wr_hi, wr_mid, _ = _split3(w_router.astype(F32).T)
    wr3 = jnp.concatenate([wr_hi, wr_mid], axis=0)
    h = x
    for l in range(DEPTH):
        qd, kd, vd, qf, kf, vf, zf = _in_projection(h, _pad_w_in(w_in[l]), cos_tab, sin_tab)
        zt = zf[:, :, :N_HEADS_FOX].transpose(0, 2, 1)
        c = _forget_cumsum(zt, b_forget[l])
        o_fox = _fox_attention(qf, kf, vf, c)
        o_dil = _dilated_attention(qd, kd, vd)
        h1, h1t, experts, gates, ranks, counts = _out_projection(
            o_dil.reshape(T, D_DIL), o_fox.reshape(T, D_FOX), h.reshape(T, D),
            w_out[l].astype(MXU_DTYPE), g_dil[l], g_fox[l], ln1_g[l], ln1_b[l], wr3, b_router)
        h2 = _grouped_moe(h1, h1t, experts, gates, ranks, counts,
                          w_gate[l].astype(MXU_DTYPE), w_up[l].astype(MXU_DTYPE), w_down[l].astype(MXU_DTYPE),
                          ln2_g[l], ln2_b[l])
        h = h2.reshape(B, S, D)
    return h
```

```python
import functools

import numpy as np
import jax
import jax.numpy as jnp
from jax import lax
from jax.experimental import pallas as pl
from jax.experimental.pallas import tpu as pltpu

D_MODEL = 1024
DEPTH = 2
HEAD_DIM = 64
N_HEADS_DIL = 12
N_HEADS_FOX = 4
D_DIL = N_HEADS_DIL * HEAD_DIM
D_FOX = N_HEADS_FOX * HEAD_DIM
DILATIONS = (1, 4, 16)
BLOCK = 128
ROPE_THETA = 10000.0
N_EXPERTS = 16
N_GROUPS = 4
EXPERTS_PER_GROUP = 4
TOP_K = 2
DEEPNORM_ALPHA = (2.0 * DEPTH) ** 0.25
LN_EPS = 1e-5
RMS_EPS = 1e-6

LANES = 128
N_PAIRS_DIL = D_DIL // LANES
D_PROJ_PAD = 3 * D_DIL + 3 * D_FOX + LANES
VMEM_LIMIT = 48 * 1024 * 1024
TOKEN_TILE_ROWS = D_MODEL // LANES
DMA_ISSUE_UNROLL = 8
DIL_PAD = max(DILATIONS) * BLOCK
MASK_BIAS = -1e30
LOG2_E = 1.4426950408889634
DIL_UNITS_PER_BODY = 8
FOX_TK = 256
ZF_ROWS = 8

MXU_DTYPE = jnp.bfloat16
F32 = jnp.float32
NEG_INF = float("-inf")


def _cparams(n_axes):
    return pltpu.CompilerParams(dimension_semantics=("arbitrary",) * n_axes,
                                vmem_limit_bytes=VMEM_LIMIT)


def _dot(a, b):
    return jnp.dot(a, b, preferred_element_type=F32)


def _dot_nt(a, b):
    return lax.dot_general(a, b, (((1,), (1,)), ((), ())), preferred_element_type=F32)


def _inproj_kernel(x_ref, w_ref, cos_ref, sin_ref,
                   qd_ref, kd_ref, vd_ref, qf_ref, kf_ref, vf_ref, zf_ref):
    xb = x_ref[0].astype(MXU_DTYPE)
    tm = xb.shape[0]
    cos = cos_ref[...]
    sin = sin_ref[...]
    lane = lax.broadcasted_iota(jnp.int32, (tm, LANES), 1)
    first_half = (lane % HEAD_DIM) < (HEAD_DIM // 2)

    def rope(z):
        rot = jnp.where(first_half, pltpu.roll(z, LANES - 32, 1), pltpu.roll(z, 32, 1))
        return z * cos + rot * sin

    def proj(col, width):
        return _dot(xb, w_ref[:, col:col + width])

    scale = HEAD_DIM ** -0.5
    col = 0
    for out_ref, roped, mul in ((qd_ref, True, scale * LOG2_E), (kd_ref, True, 1.0)):
        for c in range(D_DIL // 256):
            z = proj(col, 256)
            for half in range(2):
                zz = rope(z[:, half * LANES:(half + 1) * LANES]) * mul
                out_ref[0, :, c * 256 + half * LANES:c * 256 + (half + 1) * LANES] = zz.astype(out_ref.dtype)
            col += 256
    for c in range(D_DIL // 256):
        vd_ref[0, :, c * 256:(c + 1) * 256] = proj(col, 256).astype(vd_ref.dtype)
        col += 256
    qf_ref[0] = (proj(col, D_FOX) * (scale * LOG2_E)).astype(qf_ref.dtype)
    col += D_FOX
    kf_ref[0] = proj(col, D_FOX).astype(kf_ref.dtype)
    col += D_FOX
    vf_ref[0] = proj(col, D_FOX).astype(vf_ref.dtype)
    col += D_FOX
    zf_ref[0] = proj(col, LANES).T[0:ZF_ROWS, :]


def _in_projection(h, w_pad, cos_tab, sin_tab, tm=512):
    B, S, D = h.shape
    grid = (B, S // tm)
    row = lambda width: pl.BlockSpec((1, tm, width), lambda b, i: (b, i, 0))
    out_shape = (
        jax.ShapeDtypeStruct((B, S, D_DIL), MXU_DTYPE),
        jax.ShapeDtypeStruct((B, S, D_DIL), MXU_DTYPE),
        jax.ShapeDtypeStruct((B, S, D_DIL), MXU_DTYPE),
        jax.ShapeDtypeStruct((B, S, D_FOX), MXU_DTYPE),
        jax.ShapeDtypeStruct((B, S, D_FOX), MXU_DTYPE),
        jax.ShapeDtypeStruct((B, S, D_FOX), MXU_DTYPE),
        jax.ShapeDtypeStruct((B, ZF_ROWS, S), F32),
    )
    return pl.pallas_call(
        _inproj_kernel,
        grid=grid,
        in_specs=[
            row(D),
            pl.BlockSpec((D, D_PROJ_PAD), lambda b, i: (0, 0)),
            pl.BlockSpec((tm, LANES), lambda b, i: (i, 0)),
            pl.BlockSpec((tm, LANES), lambda b, i: (i, 0)),
        ],
        out_specs=(row(D_DIL), row(D_DIL), row(D_DIL), row(D_FOX), row(D_FOX), row(D_FOX),
                   pl.BlockSpec((1, ZF_ROWS, tm), lambda b, i: (b, 0, i))),
        out_shape=out_shape,
        compiler_params=_cparams(2),
        name="in_projection",
    )(h, w_pad, cos_tab, sin_tab)


def _split3(x):
    hi = x.astype(jnp.bfloat16)
    r1 = x - hi.astype(F32)
    mid = r1.astype(jnp.bfloat16)
    lo = (r1 - mid.astype(F32)).astype(jnp.bfloat16)
    return hi, mid, lo


def _forget_cumsum_kernel(z_ref, b_ref, tri_ref, c_ref):
    x = z_ref[0] + b_ref[...]
    logf = jnp.minimum(x, 0.0) - jnp.log1p(jnp.exp(-jnp.abs(x)))
    tri = tri_ref[...]
    S = x.shape[1]
    carry = jnp.zeros((x.shape[0], 1), F32)
    for blk in range(S // LANES):
        seg = logf[:, blk * LANES:(blk + 1) * LANES]
        hi, mid, lo = _split3(seg)
        cs = (_dot(hi, tri) + _dot(mid, tri)) + _dot(lo, tri) + carry
        c_ref[0, :, blk * LANES:(blk + 1) * LANES] = cs * LOG2_E
        carry = cs[:, LANES - 1:LANES]


def _forget_cumsum(zt, b_forget):
    B, Hf, S = zt.shape
    tri = (np.arange(LANES)[:, None] <= np.arange(LANES)[None, :]).astype(np.float32)
    return pl.pallas_call(
        _forget_cumsum_kernel,
        grid=(B,),
        in_specs=[
            pl.BlockSpec((1, Hf, S), lambda b: (b, 0, 0)),
            pl.BlockSpec((Hf, 1), lambda b: (0, 0)),
            pl.BlockSpec((LANES, LANES), lambda b: (0, 0)),
        ],
        out_specs=pl.BlockSpec((1, Hf, S), lambda b: (b, 0, 0)),
        out_shape=jax.ShapeDtypeStruct((B, Hf, S), F32),
        compiler_params=_cparams(1),
        name="forget_cumsum",
    )(zt, jnp.pad(b_forget.astype(F32), (0, Hf - b_forget.shape[0])).reshape(Hf, 1), jnp.asarray(tri, jnp.bfloat16))


def _pair_rows(xp, is_a):
    zero = jnp.zeros_like(xp)
    return jnp.concatenate([jnp.where(is_a, xp, zero), jnp.where(is_a, zero, xp)], axis=0)


def _causal_bias():
    qi = np.arange(BLOCK)[:, None]
    ki = np.arange(FOX_TK)[None, :]
    vis = np.stack([ki <= qi, ki <= qi + BLOCK])
    return np.where(vis, 0.0, MASK_BIAS).astype(np.float32)


def _fox_kernel(q_ref, k_ref, v_ref, crow_ref, cbias_ref, o_ref):
    S = q_ref.shape[1]
    tk = FOX_TK
    lane = lax.broadcasted_iota(jnp.int32, (BLOCK, LANES), 1)
    is_a = lane < HEAD_DIM
    ones = jnp.ones((tk, LANES), MXU_DTYPE)

    def unit(q2, kv, bias):
        kb, v_ones, cr = kv
        s = _dot_nt(q2, kb)
        sa = s[:BLOCK] - cr[0:1]
        sb = s[BLOCK:] - cr[1:2]
        if bias is not None:
            sa = sa + bias
            sb = sb + bias
        s = jnp.concatenate([sa, sb], axis=0)
        m = jnp.max(s, axis=1, keepdims=True)
        p = jnp.exp2(s - m).astype(MXU_DTYPE)
        o = _dot(p, v_ones)
        acc = jnp.where(is_a, o[:BLOCK, :LANES], o[BLOCK:, :LANES])
        l = jnp.where(is_a, o[:BLOCK, LANES:], o[BLOCK:, LANES:])
        return acc, l, jnp.where(is_a, m[:BLOCK], m[BLOCK:])

    def key_block(j):
        k0 = pl.multiple_of(j * tk, tk)
        return (k_ref[0, pl.ds(k0, tk), :],
                jnp.concatenate([v_ref[0, pl.ds(k0, tk), :], ones], axis=1),
                crow_ref[0, 0, :, pl.ds(k0, tk)])

    def fold(state, units):
        num, den, top_old = state
        top = functools.reduce(jnp.maximum, [top_old] + [m for _, _, m in units])
        scale = jnp.exp2(top_old - top)
        num, den = scale * num, scale * den
        for acc, l, m in units:
            w = jnp.exp2(m - top)
            num = num + w * acc
            den = den + w * l
        return num, den, top

    def q_block(i, _):
        r0 = pl.multiple_of(i * tk, tk)
        q2 = [_pair_rows(q_ref[0, pl.ds(r0 + h * BLOCK, BLOCK), :], is_a) for h in range(2)]
        zero = jnp.zeros((BLOCK, LANES), F32)
        empty = (zero, zero, jnp.full((BLOCK, LANES), MASK_BIAS, F32))

        def fold_blocks(states, blocks):
            units = [[unit(q2[h], kv, None if bias is None else bias[h]) for h in range(2)]
                     for kv, bias in ((key_block(j), bias) for j, bias in blocks)]
            return tuple(fold(states[h], [u[h] for u in units]) for h in range(2))

        def two_blocks(g, states):
            return fold_blocks(states, [(2 * g, None), (2 * g + 1, None)])

        states = lax.fori_loop(0, lax.shift_right_logical(i, 1), two_blocks, (empty, empty))
        diag = (i, (cbias_ref[0], cbias_ref[1]))
        states = lax.cond((i & 1) == 1,
                          lambda st: fold_blocks(st, [(i - 1, None), diag]),
                          lambda st: fold_blocks(st, [diag]), states)
        for h in range(2):
            num, den, _ = states[h]
            o_ref[0, pl.ds(r0 + h * BLOCK, BLOCK), :] = (num / den).astype(o_ref.dtype)
        return 0

    lax.fori_loop(0, S // tk, q_block, 0)


def _fox_attention(qf, kf, vf, c2):
    B, S, _ = qf.shape
    n_pairs = D_FOX // LANES
    c4 = c2.reshape(B, n_pairs, 2, S)
    blk = pl.BlockSpec((1, S, LANES), lambda b, p: (b, 0, p))
    return pl.pallas_call(
        _fox_kernel,
        grid=(B, n_pairs),
        in_specs=[blk, blk, blk,
                  pl.BlockSpec((1, 1, 2, S), lambda b, p: (b, p, 0, 0)),
                  pl.BlockSpec((2, BLOCK, FOX_TK), lambda b, p: (0, 0, 0))],
        out_specs=blk,
        out_shape=jax.ShapeDtypeStruct((B, S, D_FOX), MXU_DTYPE),
        compiler_params=_cparams(2),
        name="fox_attention",
    )(qf, kf, vf, c4, jnp.asarray(_causal_bias()))


def _band_bias():
    qi = np.arange(BLOCK)[:, None]
    ki = np.arange(2 * BLOCK)[None, :]
    delta = BLOCK + qi - ki
    band = (delta >= 0) & (delta <= BLOCK)
    first = band & (ki >= BLOCK)
    return np.where(np.stack([first, band]), 0.0, MASK_BIAS).astype(np.float32)


def _dil_kernel(q_ref, k_ref, v_ref, bias_ref, o_ref, q32, k32, v32, onorm_s, mu_s):
    S = q_ref.shape[1]
    q32[...] = q_ref[0].astype(F32)
    zeros = jnp.zeros((DIL_PAD, LANES), F32)
    k32[0:DIL_PAD, :] = zeros
    v32[0:DIL_PAD, :] = zeros
    k32[DIL_PAD:, :] = k_ref[0].astype(F32)
    v32[DIL_PAD:, :] = v_ref[0].astype(F32)

    lane = lax.broadcasted_iota(jnp.int32, (BLOCK, LANES), 1)
    is_a = lane < HEAD_DIM
    ones = jnp.ones((2 * BLOCK, LANES), MXU_DTYPE)

    def rows(ref, start, count, dil):
        if dil == 1:
            return ref[pl.ds(start, count), :]
        return ref[pl.ds(start, count, stride=dil), :]

    def unit(dil, rho, n):
        q0 = rho + dil * BLOCK * n
        qb = rows(q32, q0, BLOCK, dil).astype(MXU_DTYPE)
        kw = rows(k32, q0 + DIL_PAD - dil * BLOCK, 2 * BLOCK, dil).astype(MXU_DTYPE)
        vw = rows(v32, q0 + DIL_PAD - dil * BLOCK, 2 * BLOCK, dil).astype(MXU_DTYPE)
        bias = bias_ref[jnp.minimum(n, 1)]
        s = _dot_nt(_pair_rows(qb, is_a), kw)
        s = jnp.concatenate([s[:BLOCK] + bias, s[BLOCK:] + bias], axis=0)
        m = jnp.max(s, axis=1, keepdims=True)
        p = jnp.exp2(s - m).astype(MXU_DTYPE)
        o = _dot(p, jnp.concatenate([vw, ones], axis=1))
        acc = jnp.where(is_a, o[:BLOCK, :LANES], o[BLOCK:, :LANES])
        l = jnp.where(is_a, o[:BLOCK, LANES:], o[BLOCK:, LANES:])
        mu = jnp.where(is_a, m[:BLOCK], m[BLOCK:]) + jnp.log2(l)
        return acc / l, mu

    n_units = S // BLOCK

    for slab, dil in enumerate(d for d in DILATIONS if d != 1):
        blk_bits = (n_units // dil).bit_length() - 1

        def several_units(g, _, slab=slab, dil=dil, blk_bits=blk_bits):
            for u in range(DIL_UNITS_PER_BODY):
                t = g * DIL_UNITS_PER_BODY + u
                rho = lax.shift_right_logical(t, blk_bits)
                n = t & ((1 << blk_bits) - 1)
                o_n, mu = unit(dil, rho, n)
                q0 = rho + dil * BLOCK * n
                onorm_s[slab, pl.ds(q0, BLOCK, stride=dil), :] = o_n
                mu_s[slab, pl.ds(q0, BLOCK, stride=dil), :] = mu
            return 0

        lax.fori_loop(0, n_units // DIL_UNITS_PER_BODY, several_units, 0)

    def merge_blocks(g, _):
        for u in range(DIL_UNITS_PER_BODY):
            n = g * DIL_UNITS_PER_BODY + u
            o_1, mu_1 = unit(1, 0, n)
            r0 = pl.multiple_of(n * BLOCK, BLOCK)
            others = [(onorm_s[sl, pl.ds(r0, BLOCK), :], mu_s[sl, pl.ds(r0, BLOCK), :])
                      for sl in range(len(DILATIONS) - 1)]
            top = functools.reduce(jnp.maximum, [mu_1] + [mu for _, mu in others])
            w = jnp.exp2(mu_1 - top)
            num, den = w * o_1, w
            for o_p, mu_p in others:
                w = jnp.exp2(mu_p - top)
                num = num + w * o_p
                den = den + w
            o_ref[0, pl.ds(r0, BLOCK), :] = (num / den).astype(o_ref.dtype)
        return 0

    lax.fori_loop(0, n_units // DIL_UNITS_PER_BODY, merge_blocks, 0)


def _dilated_attention(qd, kd, vd):
    B, S, _ = qd.shape
    blk = pl.BlockSpec((1, S, LANES), lambda b, p: (b, 0, p))
    n_slabs = len(DILATIONS) - 1
    return pl.pallas_call(
        _dil_kernel,
        grid=(B, N_PAIRS_DIL),
        in_specs=[blk, blk, blk, pl.BlockSpec((2, BLOCK, 2 * BLOCK), lambda b, p: (0, 0, 0))],
        out_specs=blk,
        out_shape=jax.ShapeDtypeStruct((B, S, D_DIL), MXU_DTYPE),
        scratch_shapes=[
            pltpu.VMEM((S, LANES), F32),
            pltpu.VMEM((DIL_PAD + S, LANES), F32),
            pltpu.VMEM((DIL_PAD + S, LANES), F32),
            pltpu.VMEM((n_slabs, S, LANES), F32),
            pltpu.VMEM((n_slabs, S, LANES), F32),
        ],
        compiler_params=_cparams(2),
        name="dilated_attention",
    )(qd, kd, vd, jnp.asarray(_band_bias()))


def _layer_norm(u, g, b):
    mu = jnp.mean(u, axis=1, keepdims=True)
    d = u - mu
    var = jnp.mean(d * d, axis=1, keepdims=True)
    return d * lax.rsqrt(var + LN_EPS) * g + b


def _rms_norm(x, g):
    ms = jnp.mean(x * x, axis=1, keepdims=True)
    return x * lax.rsqrt(ms + RMS_EPS) * g


def _top2_of4(vals):
    v1, i1 = vals[0], jnp.zeros(vals[0].shape, jnp.int32)
    for i in range(1, 4):
        better = vals[i] > v1
        v1 = jnp.where(better, vals[i], v1)
        i1 = jnp.where(better, i, i1)
    v2 = jnp.full(vals[0].shape, -1.0, F32)
    i2 = jnp.zeros(vals[0].shape, jnp.int32)
    for i in range(4):
        better = (vals[i] > v2) & (i1 != i)
        v2 = jnp.where(better, vals[i], v2)
        i2 = jnp.where(better, i, i2)
    return v1, i1, v2, i2


def _outproj_kernel(od_ref, of_ref, h_ref, wo_ref, gd_ref, gf_ref, lng_ref, lnb_ref,
                    wr_ref, br_ref, tri_ref,
                    h1_ref, h1t_ref, e_ref, gate_ref, rank_ref, cnt_ref, base_ref):
    step = pl.program_id(0)

    @pl.when(step == 0)
    def _():
        base_ref[...] = jnp.zeros_like(base_ref)

    xd = _rms_norm(od_ref[...].astype(F32), gd_ref[...])
    xf = _rms_norm(of_ref[...].astype(F32), gf_ref[...])
    y = _dot(xd.astype(MXU_DTYPE), wo_ref[0:D_DIL, :]) + _dot(xf.astype(MXU_DTYPE), wo_ref[D_DIL:, :])
    h1 = _layer_norm(DEEPNORM_ALPHA * h_ref[...] + y, lng_ref[...], lnb_ref[...])
    h1_ref[...] = h1
    _store_token_tiles(h1t_ref, h1)
    tm = h1.shape[0]

    h_hi, h_mid, _ = _split3(h1)
    two = _dot_nt(wr_ref[...], h_hi)
    logits = (two[:N_EXPERTS] + two[N_EXPERTS:]) + _dot_nt(wr_ref[0:N_EXPERTS, :], h_mid) + br_ref[...]
    logits = logits - jnp.max(logits, axis=0, keepdims=True)
    ex = jnp.exp(logits)
    probs = ex / jnp.sum(ex, axis=0, keepdims=True)
    pr = [probs[j:j + 1, :] for j in range(N_EXPERTS)]

    def group_score(g):
        v = pr[4 * g:4 * g + 4]
        pairs = [v[a] + v[b] for a in range(4) for b in range(a + 1, 4)]
        return functools.reduce(jnp.maximum, pairs)

    best = group_score(0)
    gsel = jnp.zeros((1, tm), jnp.int32)
    for g in range(1, N_GROUPS):
        sc = group_score(g)
        better = sc > best
        best = jnp.where(better, sc, best)
        gsel = jnp.where(better, g, gsel)
    in_grp = []
    for i in range(EXPERTS_PER_GROUP):
        v = pr[i]
        for g in range(1, N_GROUPS):
            v = jnp.where(gsel == g, pr[4 * g + i], v)
        in_grp.append(v)
    v1, i1, v2, i2 = _top2_of4(in_grp)
    e1 = gsel * EXPERTS_PER_GROUP + i1
    e2 = gsel * EXPERTS_PER_GROUP + i2
    den = v1 + v2
    e_ref[...] = jnp.concatenate([e1, e2], axis=0)
    gate_ref[...] = jnp.concatenate([v1 / den, v2 / den], axis=0)

    eidx = lax.broadcasted_iota(jnp.int32, (N_EXPERTS, tm), 0)
    oh1 = (eidx == e1).astype(F32)
    oh2 = (eidx == e2).astype(F32)
    tot = oh1 + oh2
    before = base_ref[...] + _dot(tot.astype(jnp.bfloat16), tri_ref[...])
    r1 = jnp.sum(oh1 * before, axis=0, keepdims=True)
    r2 = jnp.sum(oh2 * before, axis=0, keepdims=True)
    rank_ref[...] = jnp.concatenate([r1, r2], axis=0).astype(jnp.int32)
    base_ref[...] = base_ref[...] + jnp.sum(tot, axis=1, keepdims=True)
    cnt_ref[...] = jnp.broadcast_to(base_ref[...], cnt_ref.shape)


def _out_projection(od, of, h, w_out, g_dil, g_fox, ln_g, ln_b, wr3, b_router, tm=256):
    T = h.shape[0]
    tri = (np.arange(tm)[:, None] < np.arange(tm)[None, :]).astype(np.float32)
    rows = lambda width: pl.BlockSpec((tm, width), lambda i: (i, 0))
    full = lambda a: pl.BlockSpec(a.shape, lambda i: (0,) * a.ndim)
    tok = pl.BlockSpec((TOP_K, tm), lambda i: (0, i))
    consts = [w_out, g_dil.reshape(1, -1), g_fox.reshape(1, -1), ln_g.reshape(1, -1), ln_b.reshape(1, -1),
              wr3, b_router.reshape(-1, 1).astype(F32), jnp.asarray(tri, jnp.bfloat16)]
    return pl.pallas_call(
        _outproj_kernel,
        grid=(T // tm,),
        in_specs=[rows(D_DIL), rows(D_FOX), rows(D_MODEL)] + [full(a) for a in consts],
        out_specs=(rows(D_MODEL), pl.BlockSpec((tm * TOKEN_TILE_ROWS, LANES), lambda i: (i, 0)),
                   tok, tok, tok, pl.BlockSpec((N_EXPERTS, LANES), lambda i: (0, 0))),
        out_shape=(
            jax.ShapeDtypeStruct((T, D_MODEL), F32),
            jax.ShapeDtypeStruct((T * TOKEN_TILE_ROWS, LANES), F32),
            jax.ShapeDtypeStruct((TOP_K, T), jnp.int32),
            jax.ShapeDtypeStruct((TOP_K, T), F32),
            jax.ShapeDtypeStruct((TOP_K, T), jnp.int32),
            jax.ShapeDtypeStruct((N_EXPERTS, LANES), F32),
        ),
        scratch_shapes=[pltpu.VMEM((N_EXPERTS, 1), F32)],
        compiler_params=_cparams(1),
        name="out_projection_router",
    )(od, of, h, *consts)


def _store_token_tiles(ref, x):
    n = x.shape[0]
    for c in range(TOKEN_TILE_ROWS):
        ref[pl.ds(c, n, stride=TOKEN_TILE_ROWS), :] = x[:, c * LANES:(c + 1) * LANES]


def _load_token_tiles(ref, n):
    return jnp.concatenate([ref[pl.ds(c, n, stride=TOKEN_TILE_ROWS), :] for c in range(TOKEN_TILE_ROWS)], axis=1)


def _tile_copy(src, src_row, dst, dst_row, sem):
    return pltpu.make_async_copy(src.at[pl.ds(src_row, TOKEN_TILE_ROWS)],
                                 dst.at[pl.ds(dst_row, TOKEN_TILE_ROWS)], sem)


def _dispatch_kernel(p0_ref, p1_ref, h_ref, xg_in, xg_out, sem, *, td):
    del xg_in

    def copies(t):
        src = pl.multiple_of(t * TOKEN_TILE_ROWS, TOKEN_TILE_ROWS)
        return (_tile_copy(h_ref, src, xg_out, pl.multiple_of(p0_ref[t], TOKEN_TILE_ROWS), sem),
                _tile_copy(h_ref, src, xg_out, pl.multiple_of(p1_ref[t], TOKEN_TILE_ROWS), sem))

    def issue(t, _):
        for thread, cp in enumerate(copies(t)):
            cp.start(priority=thread)
        return 0

    def drain(t, _):
        for cp in copies(t):
            cp.wait()
        return 0

    lax.fori_loop(0, td, issue, 0, unroll=DMA_ISSUE_UNROLL)
    lax.fori_loop(0, td, drain, 0, unroll=DMA_ISSUE_UNROLL)


def _dispatch(h1t, pos_rows, n_rows, td=256):
    T = h1t.shape[0] // TOKEN_TILE_ROWS
    smem = pl.BlockSpec((td,), lambda i: (i,), memory_space=pltpu.SMEM)
    anyspec = pl.BlockSpec(memory_space=pl.ANY)
    return pl.pallas_call(
        functools.partial(_dispatch_kernel, td=td),
        grid=(T // td,),
        in_specs=[smem, smem, pl.BlockSpec((td * TOKEN_TILE_ROWS, LANES), lambda i: (i, 0)), anyspec],
        out_specs=anyspec,
        out_shape=jax.ShapeDtypeStruct((n_rows * TOKEN_TILE_ROWS, LANES), F32),
        scratch_shapes=[pltpu.SemaphoreType.DMA(())],
        input_output_aliases={3: 0},
        compiler_params=_cparams(1),
        name="moe_dispatch",
    )(pos_rows[0], pos_rows[1], h1t, jnp.zeros((n_rows * TOKEN_TILE_ROWS, LANES), F32))


def _ffn_kernel(be_ref, nv_ref, x_ref, wg_ref, wu_ref, wd_ref, y_ref, wg_b, wu_b, wd_b, *, tmf):
    j = pl.program_id(0)

    @pl.when((j == 0) | (be_ref[j] != be_ref[jnp.maximum(j - 1, 0)]))
    def _():
        wg_b[...] = wg_ref[0, 0].astype(MXU_DTYPE)
        wu_b[...] = wu_ref[0, 0].astype(MXU_DTYPE)
        wd_b[...] = wd_ref[0, 0].astype(MXU_DTYPE)

    @pl.when(j < nv_ref[0])
    def _():
        xb = _load_token_tiles(x_ref, tmf).astype(MXU_DTYPE)
        a = _dot(xb, wg_b[...])
        u = _dot(xb, wu_b[...])
        hmid = (a * jax.nn.sigmoid(a)) * u
        _store_token_tiles(y_ref, _dot(hmid.astype(MXU_DTYPE), wd_b[...]))

    @pl.when(j >= nv_ref[0])
    def _():
        y_ref[...] = jnp.zeros_like(y_ref)


def _expert_ffn(xg, blk_e, n_valid, wg, wu, wd, layer, tmf):
    n_rows = xg.shape[0] // TOKEN_TILE_ROWS
    D = D_MODEL
    nb = n_rows // tmf
    xmap = lambda j, be, nv: (jnp.minimum(j, nv[0] - 1), 0)
    wmap = lambda j, be, nv: (layer, be[j], 0, 0)
    grid_spec = pltpu.PrefetchScalarGridSpec(
        num_scalar_prefetch=2,
        grid=(nb,),
        in_specs=[
            pl.BlockSpec((tmf * TOKEN_TILE_ROWS, LANES), xmap),
            pl.BlockSpec((1, 1, D, D), wmap),
            pl.BlockSpec((1, 1, D, D), wmap),
            pl.BlockSpec((1, 1, D, D), wmap),
        ],
        out_specs=pl.BlockSpec((tmf * TOKEN_TILE_ROWS, LANES), lambda j, be, nv: (j, 0)),
        scratch_shapes=[pltpu.VMEM((D, D), MXU_DTYPE)] * 3,
    )
    return pl.pallas_call(
        functools.partial(_ffn_kernel, tmf=tmf),
        grid_spec=grid_spec,
        out_shape=jax.ShapeDtypeStruct(xg.shape, F32),
        compiler_params=_cparams(1),
        name="moe_expert_ffn",
    )(blk_e, n_valid, xg, wg, wu, wd)


def _combine_kernel(p0_ref, p1_ref, gate_ref, h_ref, lng_ref, lnb_ref, y_hbm, o_ref, buf0, buf1, sem, *, tc):
    def copies(t):
        dst = pl.multiple_of(t * TOKEN_TILE_ROWS, TOKEN_TILE_ROWS)
        return (_tile_copy(y_hbm, pl.multiple_of(p0_ref[t], TOKEN_TILE_ROWS), buf0, dst, sem),
                _tile_copy(y_hbm, pl.multiple_of(p1_ref[t], TOKEN_TILE_ROWS), buf1, dst, sem))

    def issue(t, _):
        for thread, cp in enumerate(copies(t)):
            cp.start(priority=thread)
        return 0

    def drain(t, _):
        for cp in copies(t):
            cp.wait()
        return 0

    lax.fori_loop(0, tc, issue, 0, unroll=DMA_ISSUE_UNROLL)
    lax.fori_loop(0, tc, drain, 0, unroll=DMA_ISSUE_UNROLL)
    g = gate_ref[...]
    y = _load_token_tiles(buf0, tc) * g[:, 0:1] + _load_token_tiles(buf1, tc) * g[:, 1:2]
    o_ref[...] = _layer_norm(DEEPNORM_ALPHA * h_ref[...] + y, lng_ref[...], lnb_ref[...])


def _combine(yg, pos_rows, gates_col, h1, ln_g, ln_b, tc=256):
    T, D = h1.shape
    smem = pl.BlockSpec((tc,), lambda i: (i,), memory_space=pltpu.SMEM)
    rows = lambda width: pl.BlockSpec((tc, width), lambda i: (i, 0))
    vec = pl.BlockSpec((1, D), lambda i: (0, 0))
    buf = pltpu.VMEM((tc * TOKEN_TILE_ROWS, LANES), F32)
    return pl.pallas_call(
        functools.partial(_combine_kernel, tc=tc),
        grid=(T // tc,),
        in_specs=[smem, smem, rows(TOP_K), rows(D), vec, vec, pl.BlockSpec(memory_space=pl.ANY)],
        out_specs=rows(D),
        out_shape=jax.ShapeDtypeStruct((T, D), F32),
        scratch_shapes=[buf, buf, pltpu.SemaphoreType.DMA(())],
        compiler_params=_cparams(1),
        name="moe_combine_ln",
    )(pos_rows[0], pos_rows[1], gates_col, h1, ln_g.reshape(1, -1), ln_b.reshape(1, -1), yg)


def _grouped_moe(h1, h1t, experts, gates, ranks, counts, wg, wu, wd, layer, ln_g, ln_b, tmf=256):
    T = h1.shape[0]
    nb = (T * TOP_K) // tmf + N_EXPERTS
    cnt = counts[:, 0].astype(jnp.int32)
    pcnt = (cnt + tmf - 1) // tmf * tmf
    pends = jnp.cumsum(pcnt)
    pstart = pends - pcnt
    eids = jnp.arange(N_EXPERTS, dtype=jnp.int32)
    seg = jnp.sum(jnp.where(experts[None] == eids[:, None, None], pstart[:, None, None], 0), axis=0)
    pos_rows = (seg + ranks) * TOKEN_TILE_ROWS
    blk_start = jnp.arange(nb, dtype=jnp.int32) * tmf
    blk_e = jnp.minimum(jnp.sum((pends[None, :] <= blk_start[:, None]).astype(jnp.int32), axis=1), N_EXPERTS - 1)
    n_valid = (pends[-1:] // tmf).astype(jnp.int32)
    xg = _dispatch(h1t, pos_rows, nb * tmf)
    yg = _expert_ffn(xg, blk_e, n_valid, wg, wu, wd, layer, tmf)
    return _combine(yg, pos_rows, gates.T, h1, ln_g, ln_b)


def _rope_tables(S):
    half = HEAD_DIM // 2
    inv = ROPE_THETA ** (-jnp.arange(half, dtype=F32) / half)
    ang = jnp.arange(S, dtype=F32)[:, None] * inv[None, :]
    cos, sin = jnp.cos(ang), jnp.sin(ang)
    reps = LANES // HEAD_DIM
    cos_tab = jnp.tile(jnp.concatenate([cos, cos], axis=1), (1, reps))
    sin_tab = jnp.tile(jnp.concatenate([-sin, sin], axis=1), (1, reps))
    return cos_tab, sin_tab


def _pad_w_in(w):
    pad = jnp.zeros((D_MODEL, D_PROJ_PAD - w.shape[1]), w.dtype)
    return jnp.concatenate([w, pad], axis=1).astype(MXU_DTYPE)


def kernel(x, w_in, b_forget, g_dil, g_fox, w_out, ln1_g, ln1_b, w_router, b_router,
           w_gate, w_up, w_down, ln2_g, ln2_b):
    B, S, D = x.shape
    T = B * S
    cos_tab, sin_tab = _rope_tables(S)
    wr_hi, wr_mid, _ = _split3(w_router.astype(F32).T)
    wr3 = jnp.concatenate([wr_hi, wr_mid], axis=0)
    h = x
    for l in range(DEPTH):
        qd, kd, vd, qf, kf, vf, zf = _in_projection(h, _pad_w_in(w_in[l]), cos_tab, sin_tab)
        c = _forget_cumsum(zf, b_forget[l])[:, :N_HEADS_FOX]
        o_fox = _fox_attention(qf, kf, vf, c)
        o_dil = _dilated_attention(qd, kd, vd)
        h1, h1t, experts, gates, ranks, counts = _out_projection(
            o_dil.reshape(T, D_DIL), o_fox.reshape(T, D_FOX), h.reshape(T, D),
            w_out[l].astype(MXU_DTYPE), g_dil[l], g_fox[l], ln1_g[l], ln1_b[l], wr3, b_router)
        h2 = _grouped_moe(h1, h1t, experts, gates, ranks, counts,
                          w_gate, w_up, w_down, l,
                          ln2_g[l], ln2_b[l])
        h = h2.reshape(B, S, D)
    return h
```

```python
import functools

import numpy as np
import jax
import jax.numpy as jnp
from jax import lax
from jax.experimental import pallas as pl
from jax.experimental.pallas import tpu as pltpu

D_MODEL = 1024
DEPTH = 2
HEAD_DIM = 64
N_HEADS_DIL = 12
N_HEADS_FOX = 4
D_DIL = N_HEADS_DIL * HEAD_DIM
D_FOX = N_HEADS_FOX * HEAD_DIM
DILATIONS = (1, 4, 16)
BLOCK = 128
ROPE_THETA = 10000.0
N_EXPERTS = 16
N_GROUPS = 4
EXPERTS_PER_GROUP = 4
TOP_K = 2
DEEPNORM_ALPHA = (2.0 * DEPTH) ** 0.25
LN_EPS = 1e-5
RMS_EPS = 1e-6

LANES = 128
N_PAIRS_DIL = D_DIL // LANES
D_PROJ_PAD = 3 * D_DIL + 3 * D_FOX + LANES
VMEM_LIMIT = 48 * 1024 * 1024
TOKEN_TILE_ROWS = D_MODEL // LANES
DMA_ISSUE_UNROLL = 8
DIL_PAD = max(DILATIONS) * BLOCK
MASK_BIAS = -1e30
LOG2_E = 1.4426950408889634
DIL_UNITS_PER_BODY = 16
FOX_TK = 256
FOX_BLOCKS_PER_BODY = 4
ZF_ROWS = 8

MXU_DTYPE = jnp.bfloat16
F32 = jnp.float32
NEG_INF = float("-inf")


def _cparams(n_axes):
    return pltpu.CompilerParams(dimension_semantics=("arbitrary",) * n_axes,
                                vmem_limit_bytes=VMEM_LIMIT)


def _dot(a, b):
    return jnp.dot(a, b, preferred_element_type=F32)


def _dot_nt(a, b):
    return lax.dot_general(a, b, (((1,), (1,)), ((), ())), preferred_element_type=F32)


def _inproj_kernel(x_ref, w_ref, cos_ref, sin_ref,
                   qd_ref, kd_ref, vd_ref, qf_ref, kf_ref, vf_ref, zf_ref):
    xb = x_ref[0].astype(MXU_DTYPE)
    tm = xb.shape[0]
    cos = cos_ref[...]
    sin = sin_ref[...]
    lane = lax.broadcasted_iota(jnp.int32, (tm, LANES), 1)
    first_half = (lane % HEAD_DIM) < (HEAD_DIM // 2)

    def rope(z):
        rot = jnp.where(first_half, pltpu.roll(z, LANES - 32, 1), pltpu.roll(z, 32, 1))
        return z * cos + rot * sin

    def proj(col, width):
        return _dot(xb, w_ref[:, col:col + width])

    scale = HEAD_DIM ** -0.5
    col = 0
    for out_ref, roped, mul in ((qd_ref, True, scale * LOG2_E), (kd_ref, True, 1.0)):
        for c in range(D_DIL // 256):
            z = proj(col, 256)
            for half in range(2):
                zz = rope(z[:, half * LANES:(half + 1) * LANES]) * mul
                out_ref[0, :, c * 256 + half * LANES:c * 256 + (half + 1) * LANES] = zz.astype(out_ref.dtype)
            col += 256
    for c in range(D_DIL // 256):
        vd_ref[0, :, c * 256:(c + 1) * 256] = proj(col, 256).astype(vd_ref.dtype)
        col += 256
    qf_ref[0] = (proj(col, D_FOX) * (scale * LOG2_E)).astype(qf_ref.dtype)
    col += D_FOX
    kf_ref[0] = proj(col, D_FOX).astype(kf_ref.dtype)
    col += D_FOX
    vf_ref[0] = proj(col, D_FOX).astype(vf_ref.dtype)
    col += D_FOX
    zf_ref[0] = proj(col, LANES).T[0:ZF_ROWS, :]


def _in_projection(h, w_pad, cos_tab, sin_tab, tm=512):
    B, S, D = h.shape
    grid = (B, S // tm)
    row = lambda width: pl.BlockSpec((1, tm, width), lambda b, i: (b, i, 0))
    out_shape = (
        jax.ShapeDtypeStruct((B, S, D_DIL), MXU_DTYPE),
        jax.ShapeDtypeStruct((B, S, D_DIL), MXU_DTYPE),
        jax.ShapeDtypeStruct((B, S, D_DIL), MXU_DTYPE),
        jax.ShapeDtypeStruct((B, S, D_FOX), MXU_DTYPE),
        jax.ShapeDtypeStruct((B, S, D_FOX), MXU_DTYPE),
        jax.ShapeDtypeStruct((B, S, D_FOX), MXU_DTYPE),
        jax.ShapeDtypeStruct((B, ZF_ROWS, S), F32),
    )
    return pl.pallas_call(
        _inproj_kernel,
        grid=grid,
        in_specs=[
            row(D),
            pl.BlockSpec((D, D_PROJ_PAD), lambda b, i: (0, 0)),
            pl.BlockSpec((tm, LANES), lambda b, i: (i, 0)),
            pl.BlockSpec((tm, LANES), lambda b, i: (i, 0)),
        ],
        out_specs=(row(D_DIL), row(D_DIL), row(D_DIL), row(D_FOX), row(D_FOX), row(D_FOX),
                   pl.BlockSpec((1, ZF_ROWS, tm), lambda b, i: (b, 0, i))),
        out_shape=out_shape,
        compiler_params=_cparams(2),
        name="in_projection",
    )(h, w_pad, cos_tab, sin_tab)


def _split3(x):
    hi = x.astype(jnp.bfloat16)
    r1 = x - hi.astype(F32)
    mid = r1.astype(jnp.bfloat16)
    lo = (r1 - mid.astype(F32)).astype(jnp.bfloat16)
    return hi, mid, lo


def _forget_cumsum_kernel(z_ref, b_ref, tri_ref, c_ref):
    x = z_ref[0] + b_ref[...]
    logf = jnp.minimum(x, 0.0) - jnp.log1p(jnp.exp(-jnp.abs(x)))
    tri = tri_ref[...]
    S = x.shape[1]
    carry = jnp.zeros((x.shape[0], 1), F32)
    for blk in range(S // LANES):
        seg = logf[:, blk * LANES:(blk + 1) * LANES]
        hi, mid, lo = _split3(seg)
        cs = (_dot(hi, tri) + _dot(mid, tri)) + _dot(lo, tri) + carry
        c_ref[0, :, blk * LANES:(blk + 1) * LANES] = cs * LOG2_E
        carry = cs[:, LANES - 1:LANES]


def _forget_cumsum(zt, b_forget):
    B, Hf, S = zt.shape
    tri = (np.arange(LANES)[:, None] <= np.arange(LANES)[None, :]).astype(np.float32)
    return pl.pallas_call(
        _forget_cumsum_kernel,
        grid=(B,),
        in_specs=[
            pl.BlockSpec((1, Hf, S), lambda b: (b, 0, 0)),
            pl.BlockSpec((Hf, 1), lambda b: (0, 0)),
            pl.BlockSpec((LANES, LANES), lambda b: (0, 0)),
        ],
        out_specs=pl.BlockSpec((1, Hf, S), lambda b: (b, 0, 0)),
        out_shape=jax.ShapeDtypeStruct((B, Hf, S), F32),
        compiler_params=_cparams(1),
        name="forget_cumsum",
    )(zt, jnp.pad(b_forget.astype(F32), (0, Hf - b_forget.shape[0])).reshape(Hf, 1), jnp.asarray(tri, jnp.bfloat16))


def _pair_rows(xp, is_a):
    zero = jnp.zeros_like(xp)
    return jnp.concatenate([jnp.where(is_a, xp, zero), jnp.where(is_a, zero, xp)], axis=0)


def _causal_bias():
    qi = np.arange(BLOCK)[:, None]
    ki = np.arange(FOX_TK)[None, :]
    vis = np.stack([ki <= qi, ki <= qi + BLOCK])
    return np.where(vis, 0.0, MASK_BIAS).astype(np.float32)


def _fox_kernel(q_ref, k_ref, v_ref, crow_ref, cbias_ref, o_ref):
    S = q_ref.shape[1]
    tk = FOX_TK
    lane = lax.broadcasted_iota(jnp.int32, (BLOCK, LANES), 1)
    is_a = lane < HEAD_DIM
    ones = jnp.ones((tk, LANES), MXU_DTYPE)

    def unit(q2, kv, bias):
        kb, v_ones, cr = kv
        s = _dot_nt(q2, kb)
        sa = s[:BLOCK] - cr[0:1]
        sb = s[BLOCK:] - cr[1:2]
        if bias is not None:
            sa = sa + bias
            sb = sb + bias
        s = jnp.concatenate([sa, sb], axis=0)
        m = jnp.max(s, axis=1, keepdims=True)
        p = jnp.exp2(s - m).astype(MXU_DTYPE)
        o = _dot(p, v_ones)
        acc = jnp.where(is_a, o[:BLOCK, :LANES], o[BLOCK:, :LANES])
        l = jnp.where(is_a, o[:BLOCK, LANES:], o[BLOCK:, LANES:])
        return acc, l, jnp.where(is_a, m[:BLOCK], m[BLOCK:])

    def key_block(j):
        k0 = pl.multiple_of(j * tk, tk)
        return (k_ref[0, pl.ds(k0, tk), :],
                jnp.concatenate([v_ref[0, pl.ds(k0, tk), :], ones], axis=1),
                crow_ref[0, 0, :, pl.ds(k0, tk)])

    def fold(state, units):
        num, den, top_old = state
        top = functools.reduce(jnp.maximum, [top_old] + [m for _, _, m in units])
        scale = jnp.exp2(top_old - top)
        num, den = scale * num, scale * den
        for acc, l, m in units:
            w = jnp.exp2(m - top)
            num = num + w * acc
            den = den + w * l
        return num, den, top

    def q_block(i, _):
        r0 = pl.multiple_of(i * tk, tk)
        q2 = [_pair_rows(q_ref[0, pl.ds(r0 + h * BLOCK, BLOCK), :], is_a) for h in range(2)]
        zero = jnp.zeros((BLOCK, LANES), F32)
        empty = (zero, zero, jnp.full((BLOCK, LANES), MASK_BIAS, F32))

        def fold_blocks(states, blocks):
            units = [[unit(q2[h], kv, None if bias is None else bias[h]) for h in range(2)]
                     for kv, bias in ((key_block(j), bias) for j, bias in blocks)]
            return tuple(fold(states[h], [u[h] for u in units]) for h in range(2))

        def full_blocks(g, states):
            return fold_blocks(states, [(FOX_BLOCKS_PER_BODY * g + u, None) for u in range(FOX_BLOCKS_PER_BODY)])

        states = lax.fori_loop(0, i // FOX_BLOCKS_PER_BODY, full_blocks, (empty, empty))
        diag = (i, (cbias_ref[0], cbias_ref[1]))

        def tail(n_left):
            return lambda st: fold_blocks(st, [(i - n_left + u, None) for u in range(n_left)] + [diag])

        states = lax.switch(i % FOX_BLOCKS_PER_BODY, [tail(r) for r in range(FOX_BLOCKS_PER_BODY)], states)
        for h in range(2):
            num, den, _ = states[h]
            o_ref[0, pl.ds(r0 + h * BLOCK, BLOCK), :] = (num / den).astype(o_ref.dtype)
        return 0

    lax.fori_loop(0, S // tk, q_block, 0)


def _fox_attention(qf, kf, vf, c2):
    B, S, _ = qf.shape
    n_pairs = D_FOX // LANES
    c4 = c2.reshape(B, n_pairs, 2, S)
    blk = pl.BlockSpec((1, S, LANES), lambda b, p: (b, 0, p))
    return pl.pallas_call(
        _fox_kernel,
        grid=(B, n_pairs),
        in_specs=[blk, blk, blk,
                  pl.BlockSpec((1, 1, 2, S), lambda b, p: (b, p, 0, 0)),
                  pl.BlockSpec((2, BLOCK, FOX_TK), lambda b, p: (0, 0, 0))],
        out_specs=blk,
        out_shape=jax.ShapeDtypeStruct((B, S, D_FOX), MXU_DTYPE),
        compiler_params=_cparams(2),
        name="fox_attention",
    )(qf, kf, vf, c4, jnp.asarray(_causal_bias()))


def _band_bias():
    qi = np.arange(BLOCK)[:, None]
    ki = np.arange(2 * BLOCK)[None, :]
    delta = BLOCK + qi - ki
    band = (delta >= 0) & (delta <= BLOCK)
    first = band & (ki >= BLOCK)
    return np.where(np.stack([first, band]), 0.0, MASK_BIAS).astype(np.float32)


def _dil_kernel(q_ref, k_ref, v_ref, bias_ref, o_ref, q32, k32, v32, onorm_s, mu_s):
    S = q_ref.shape[1]
    q32[...] = q_ref[0].astype(F32)
    zeros = jnp.zeros((DIL_PAD, LANES), F32)
    k32[0:DIL_PAD, :] = zeros
    v32[0:DIL_PAD, :] = zeros
    k32[DIL_PAD:, :] = k_ref[0].astype(F32)
    v32[DIL_PAD:, :] = v_ref[0].astype(F32)

    lane = lax.broadcasted_iota(jnp.int32, (BLOCK, LANES), 1)
    is_a = lane < HEAD_DIM
    ones = jnp.ones((2 * BLOCK, LANES), MXU_DTYPE)

    def rows(ref, start, count, dil):
        if dil == 1:
            return ref[pl.ds(start, count), :]
        return ref[pl.ds(start, count, stride=dil), :]

    def unit(dil, rho, n):
        q0 = rho + dil * BLOCK * n
        qb = rows(q32, q0, BLOCK, dil).astype(MXU_DTYPE)
        kw = rows(k32, q0 + DIL_PAD - dil * BLOCK, 2 * BLOCK, dil).astype(MXU_DTYPE)
        vw = rows(v32, q0 + DIL_PAD - dil * BLOCK, 2 * BLOCK, dil).astype(MXU_DTYPE)
        bias = bias_ref[jnp.minimum(n, 1)]
        s = _dot_nt(_pair_rows(qb, is_a), kw)
        s = jnp.concatenate([s[:BLOCK] + bias, s[BLOCK:] + bias], axis=0)
        m = jnp.max(s, axis=1, keepdims=True)
        p = jnp.exp2(s - m).astype(MXU_DTYPE)
        o = _dot(p, jnp.concatenate([vw, ones], axis=1))
        acc = jnp.where(is_a, o[:BLOCK, :LANES], o[BLOCK:, :LANES])
        l = jnp.where(is_a, o[:BLOCK, LANES:], o[BLOCK:, LANES:])
        mu = jnp.where(is_a, m[:BLOCK], m[BLOCK:]) + jnp.log2(l)
        return acc / l, mu

    n_units = S // BLOCK

    for slab, dil in enumerate(d for d in DILATIONS if d != 1):
        blk_bits = (n_units // dil).bit_length() - 1

        def several_units(g, _, slab=slab, dil=dil, blk_bits=blk_bits):
            for u in range(DIL_UNITS_PER_BODY):
                t = g * DIL_UNITS_PER_BODY + u
                rho = lax.shift_right_logical(t, blk_bits)
                n = t & ((1 << blk_bits) - 1)
                o_n, mu = unit(dil, rho, n)
                q0 = rho + dil * BLOCK * n
                onorm_s[slab, pl.ds(q0, BLOCK, stride=dil), :] = o_n
                mu_s[slab, pl.ds(q0, BLOCK, stride=dil), :] = mu
            return 0

        lax.fori_loop(0, n_units // DIL_UNITS_PER_BODY, several_units, 0)

    def merge_blocks(g, _):
        for u in range(DIL_UNITS_PER_BODY):
            n = g * DIL_UNITS_PER_BODY + u
            o_1, mu_1 = unit(1, 0, n)
            r0 = pl.multiple_of(n * BLOCK, BLOCK)
            others = [(onorm_s[sl, pl.ds(r0, BLOCK), :], mu_s[sl, pl.ds(r0, BLOCK), :])
                      for sl in range(len(DILATIONS) - 1)]
            top = functools.reduce(jnp.maximum, [mu_1] + [mu for _, mu in others])
            w = jnp.exp2(mu_1 - top)
            num, den = w * o_1, w
            for o_p, mu_p in others:
                w = jnp.exp2(mu_p - top)
                num = num + w * o_p
                den = den + w
            o_ref[0, pl.ds(r0, BLOCK), :] = (num / den).astype(o_ref.dtype)
        return 0

    lax.fori_loop(0, n_units // DIL_UNITS_PER_BODY, merge_blocks, 0)


def _dilated_attention(qd, kd, vd):
    B, S, _ = qd.shape
    blk = pl.BlockSpec((1, S, LANES), lambda b, p: (b, 0, p))
    n_slabs = len(DILATIONS) - 1
    return pl.pallas_call(
        _dil_kernel,
        grid=(B, N_PAIRS_DIL),
        in_specs=[blk, blk, blk, pl.BlockSpec((2, BLOCK, 2 * BLOCK), lambda b, p: (0, 0, 0))],
        out_specs=blk,
        out_shape=jax.ShapeDtypeStruct((B, S, D_DIL), MXU_DTYPE),
        scratch_shapes=[
            pltpu.VMEM((S, LANES), F32),
            pltpu.VMEM((DIL_PAD + S, LANES), F32),
            pltpu.VMEM((DIL_PAD + S, LANES), F32),
            pltpu.VMEM((n_slabs, S, LANES), F32),
            pltpu.VMEM((n_slabs, S, LANES), F32),
        ],
        compiler_params=_cparams(2),
        name="dilated_attention",
    )(qd, kd, vd, jnp.asarray(_band_bias()))


def _layer_norm(u, g, b):
    mu = jnp.mean(u, axis=1, keepdims=True)
    d = u - mu
    var = jnp.mean(d * d, axis=1, keepdims=True)
    return d * lax.rsqrt(var + LN_EPS) * g + b


def _rms_norm(x, g):
    ms = jnp.mean(x * x, axis=1, keepdims=True)
    return x * lax.rsqrt(ms + RMS_EPS) * g


def _top2_of4(vals):
    v1, i1 = vals[0], jnp.zeros(vals[0].shape, jnp.int32)
    for i in range(1, 4):
        better = vals[i] > v1
        v1 = jnp.where(better, vals[i], v1)
        i1 = jnp.where(better, i, i1)
    v2 = jnp.full(vals[0].shape, -1.0, F32)
    i2 = jnp.zeros(vals[0].shape, jnp.int32)
    for i in range(4):
        better = (vals[i] > v2) & (i1 != i)
        v2 = jnp.where(better, vals[i], v2)
        i2 = jnp.where(better, i, i2)
    return v1, i1, v2, i2


def _outproj_kernel(od_ref, of_ref, h_ref, wo_ref, gd_ref, gf_ref, lng_ref, lnb_ref,
                    wr_ref, br_ref, tri_ref,
                    h1_ref, h1t_ref, e_ref, gate_ref, rank_ref, cnt_ref, base_ref):
    step = pl.program_id(0)

    @pl.when(step == 0)
    def _():
        base_ref[...] = jnp.zeros_like(base_ref)

    xd = _rms_norm(od_ref[...].astype(F32), gd_ref[...])
    xf = _rms_norm(of_ref[...].astype(F32), gf_ref[...])
    y = _dot(xd.astype(MXU_DTYPE), wo_ref[0:D_DIL, :]) + _dot(xf.astype(MXU_DTYPE), wo_ref[D_DIL:, :])
    h1 = _layer_norm(DEEPNORM_ALPHA * h_ref[...] + y, lng_ref[...], lnb_ref[...])
    h1_ref[...] = h1
    _store_token_tiles(h1t_ref, h1)
    tm = h1.shape[0]

    h_hi, h_mid, _ = _split3(h1)
    two = _dot_nt(wr_ref[...], h_hi)
    logits = (two[:N_EXPERTS] + two[N_EXPERTS:]) + _dot_nt(wr_ref[0:N_EXPERTS, :], h_mid) + br_ref[...]
    logits = logits - jnp.max(logits, axis=0, keepdims=True)
    ex = jnp.exp(logits)
    probs = ex / jnp.sum(ex, axis=0, keepdims=True)
    pr = [probs[j:j + 1, :] for j in range(N_EXPERTS)]

    def group_score(g):
        v = pr[4 * g:4 * g + 4]
        pairs = [v[a] + v[b] for a in range(4) for b in range(a + 1, 4)]
        return functools.reduce(jnp.maximum, pairs)

    best = group_score(0)
    gsel = jnp.zeros((1, tm), jnp.int32)
    for g in range(1, N_GROUPS):
        sc = group_score(g)
        better = sc > best
        best = jnp.where(better, sc, best)
        gsel = jnp.where(better, g, gsel)
    in_grp = []
    for i in range(EXPERTS_PER_GROUP):
        v = pr[i]
        for g in range(1, N_GROUPS):
            v = jnp.where(gsel == g, pr[4 * g + i], v)
        in_grp.append(v)
    v1, i1, v2, i2 = _top2_of4(in_grp)
    e1 = gsel * EXPERTS_PER_GROUP + i1
    e2 = gsel * EXPERTS_PER_GROUP + i2
    den = v1 + v2
    e_ref[...] = jnp.concatenate([e1, e2], axis=0)
    gate_ref[...] = jnp.concatenate([v1 / den, v2 / den], axis=0)

    eidx = lax.broadcasted_iota(jnp.int32, (N_EXPERTS, tm), 0)
    oh1 = (eidx == e1).astype(F32)
    oh2 = (eidx == e2).astype(F32)
    tot = oh1 + oh2
    before = base_ref[...] + _dot(tot.astype(jnp.bfloat16), tri_ref[...])
    r1 = jnp.sum(oh1 * before, axis=0, keepdims=True)
    r2 = jnp.sum(oh2 * before, axis=0, keepdims=True)
    rank_ref[...] = jnp.concatenate([r1, r2], axis=0).astype(jnp.int32)
    base_ref[...] = base_ref[...] + jnp.sum(tot, axis=1, keepdims=True)
    cnt_ref[...] = jnp.broadcast_to(base_ref[...], cnt_ref.shape)


def _out_projection(od, of, h, w_out, g_dil, g_fox, ln_g, ln_b, wr3, b_router, tm=512):
    T = h.shape[0]
    tri = (np.arange(tm)[:, None] < np.arange(tm)[None, :]).astype(np.float32)
    rows = lambda width: pl.BlockSpec((tm, width), lambda i: (i, 0))
    full = lambda a: pl.BlockSpec(a.shape, lambda i: (0,) * a.ndim)
    tok = pl.BlockSpec((TOP_K, tm), lambda i: (0, i))
    consts = [w_out, g_dil.reshape(1, -1), g_fox.reshape(1, -1), ln_g.reshape(1, -1), ln_b.reshape(1, -1),
              wr3, b_router.reshape(-1, 1).astype(F32), jnp.asarray(tri, jnp.bfloat16)]
    return pl.pallas_call(
        _outproj_kernel,
        grid=(T // tm,),
        in_specs=[rows(D_DIL), rows(D_FOX), rows(D_MODEL)] + [full(a) for a in consts],
        out_specs=(rows(D_MODEL), pl.BlockSpec((tm * TOKEN_TILE_ROWS, LANES), lambda i: (i, 0)),
                   tok, tok, tok, pl.BlockSpec((N_EXPERTS, LANES), lambda i: (0, 0))),
        out_shape=(
            jax.ShapeDtypeStruct((T, D_MODEL), F32),
            jax.ShapeDtypeStruct((T * TOKEN_TILE_ROWS, LANES), F32),
            jax.ShapeDtypeStruct((TOP_K, T), jnp.int32),
            jax.ShapeDtypeStruct((TOP_K, T), F32),
            jax.ShapeDtypeStruct((TOP_K, T), jnp.int32),
            jax.ShapeDtypeStruct((N_EXPERTS, LANES), F32),
        ),
        scratch_shapes=[pltpu.VMEM((N_EXPERTS, 1), F32)],
        compiler_params=_cparams(1),
        name="out_projection_router",
    )(od, of, h, *consts)


def _store_token_tiles(ref, x):
    n = x.shape[0]
    for c in range(TOKEN_TILE_ROWS):
        ref[pl.ds(c, n, stride=TOKEN_TILE_ROWS), :] = x[:, c * LANES:(c + 1) * LANES]


def _load_token_tiles(ref, n):
    return jnp.concatenate([ref[pl.ds(c, n, stride=TOKEN_TILE_ROWS), :] for c in range(TOKEN_TILE_ROWS)], axis=1)


def _tile_copy(src, src_row, dst, dst_row, sem):
    return pltpu.make_async_copy(src.at[pl.ds(src_row, TOKEN_TILE_ROWS)],
                                 dst.at[pl.ds(dst_row, TOKEN_TILE_ROWS)], sem)


def _dispatch_kernel(p0_ref, p1_ref, pad0_ref, padn_ref, h_ref, xg_out, sem, ztile, *, td):
    def copies(t):
        src = pl.multiple_of(t * TOKEN_TILE_ROWS, TOKEN_TILE_ROWS)
        return (_tile_copy(h_ref, src, xg_out, pl.multiple_of(p0_ref[t], TOKEN_TILE_ROWS), sem),
                _tile_copy(h_ref, src, xg_out, pl.multiple_of(p1_ref[t], TOKEN_TILE_ROWS), sem))

    def issue(t, _):
        for thread, cp in enumerate(copies(t)):
            cp.start(priority=thread)
        return 0

    def drain(t, _):
        for cp in copies(t):
            cp.wait()
        return 0

    lax.fori_loop(0, td, issue, 0, unroll=DMA_ISSUE_UNROLL)
    lax.fori_loop(0, td, drain, 0, unroll=DMA_ISSUE_UNROLL)

    @pl.when(pl.program_id(0) == pl.num_programs(0) - 1)
    def _():
        ztile[...] = jnp.zeros(ztile.shape, ztile.dtype)

        def for_each_pad_row(action):
            def expert(e, _):
                def pad_row(r, _):
                    dst = pl.multiple_of(pad0_ref[e] + r * TOKEN_TILE_ROWS, TOKEN_TILE_ROWS)
                    action(_tile_copy(ztile, 0, xg_out, dst, sem))
                    return 0
                lax.fori_loop(0, padn_ref[e], pad_row, 0)
                return 0
            lax.fori_loop(0, pad0_ref.shape[0], expert, 0)

        for_each_pad_row(lambda cp: cp.start())
        for_each_pad_row(lambda cp: cp.wait())


def _dispatch(h1t, pos_rows, pad_rows, pad_counts, n_rows, td=256):
    T = h1t.shape[0] // TOKEN_TILE_ROWS
    smem = pl.BlockSpec((td,), lambda i: (i,), memory_space=pltpu.SMEM)
    smem_all = pl.BlockSpec(memory_space=pltpu.SMEM)
    return pl.pallas_call(
        functools.partial(_dispatch_kernel, td=td),
        grid=(T // td,),
        in_specs=[smem, smem, smem_all, smem_all, pl.BlockSpec((td * TOKEN_TILE_ROWS, LANES), lambda i: (i, 0))],
        out_specs=pl.BlockSpec(memory_space=pl.ANY),
        out_shape=jax.ShapeDtypeStruct((n_rows * TOKEN_TILE_ROWS, LANES), F32),
        scratch_shapes=[pltpu.SemaphoreType.DMA(()), pltpu.VMEM((TOKEN_TILE_ROWS, LANES), F32)],
        compiler_params=_cparams(1),
        name="moe_dispatch",
    )(pos_rows[0], pos_rows[1], pad_rows, pad_counts, h1t)


def _ffn_kernel(be_ref, nv_ref, x_ref, wg_ref, wu_ref, wd_ref, y_ref, wg_b, wu_b, wd_b, *, tmf):
    j = pl.program_id(0)

    @pl.when((j == 0) | (be_ref[j] != be_ref[jnp.maximum(j - 1, 0)]))
    def _():
        wg_b[...] = wg_ref[0, 0].astype(MXU_DTYPE)
        wu_b[...] = wu_ref[0, 0].astype(MXU_DTYPE)
        wd_b[...] = wd_ref[0, 0].astype(MXU_DTYPE)

    @pl.when(j < nv_ref[0])
    def _():
        xb = _load_token_tiles(x_ref, tmf).astype(MXU_DTYPE)
        a = _dot(xb, wg_b[...])
        u = _dot(xb, wu_b[...])
        hmid = (a * jax.nn.sigmoid(a)) * u
        _store_token_tiles(y_ref, _dot(hmid.astype(MXU_DTYPE), wd_b[...]))

    @pl.when(j >= nv_ref[0])
    def _():
        y_ref[...] = jnp.zeros_like(y_ref)


def _expert_ffn(xg, blk_e, n_valid, wg, wu, wd, layer, tmf):
    n_rows = xg.shape[0] // TOKEN_TILE_ROWS
    D = D_MODEL
    nb = n_rows // tmf
    xmap = lambda j, be, nv: (jnp.minimum(j, nv[0] - 1), 0)
    wmap = lambda j, be, nv: (layer, be[j], 0, 0)
    grid_spec = pltpu.PrefetchScalarGridSpec(
        num_scalar_prefetch=2,
        grid=(nb,),
        in_specs=[
            pl.BlockSpec((tmf * TOKEN_TILE_ROWS, LANES), xmap),
            pl.BlockSpec((1, 1, D, D), wmap),
            pl.BlockSpec((1, 1, D, D), wmap),
            pl.BlockSpec((1, 1, D, D), wmap),
        ],
        out_specs=pl.BlockSpec((tmf * TOKEN_TILE_ROWS, LANES), lambda j, be, nv: (j, 0)),
        scratch_shapes=[pltpu.VMEM((D, D), MXU_DTYPE)] * 3,
    )
    return pl.pallas_call(
        functools.partial(_ffn_kernel, tmf=tmf),
        grid_spec=grid_spec,
        out_shape=jax.ShapeDtypeStruct(xg.shape, F32),
        compiler_params=_cparams(1),
        name="moe_expert_ffn",
    )(blk_e, n_valid, xg, wg, wu, wd)


def _combine_kernel(p0_ref, p1_ref, gate_ref, h_ref, lng_ref, lnb_ref, y_hbm, o_ref, buf0, buf1, sem, *, tc):
    def copies(t):
        dst = pl.multiple_of(t * TOKEN_TILE_ROWS, TOKEN_TILE_ROWS)
        return (_tile_copy(y_hbm, pl.multiple_of(p0_ref[t], TOKEN_TILE_ROWS), buf0, dst, sem),
                _tile_copy(y_hbm, pl.multiple_of(p1_ref[t], TOKEN_TILE_ROWS), buf1, dst, sem))

    def issue(t, _):
        for thread, cp in enumerate(copies(t)):
            cp.start(priority=thread)
        return 0

    def drain(t, _):
        for cp in copies(t):
            cp.wait()
        return 0

    lax.fori_loop(0, tc, issue, 0, unroll=DMA_ISSUE_UNROLL)
    lax.fori_loop(0, tc, drain, 0, unroll=DMA_ISSUE_UNROLL)
    g = gate_ref[...]
    y = _load_token_tiles(buf0, tc) * g[:, 0:1] + _load_token_tiles(buf1, tc) * g[:, 1:2]
    o_ref[...] = _layer_norm(DEEPNORM_ALPHA * h_ref[...] + y, lng_ref[...], lnb_ref[...])


def _combine(yg, pos_rows, gates_col, h1, ln_g, ln_b, tc=256):
    T, D = h1.shape
    smem = pl.BlockSpec((tc,), lambda i: (i,), memory_space=pltpu.SMEM)
    rows = lambda width: pl.BlockSpec((tc, width), lambda i: (i, 0))
    vec = pl.BlockSpec((1, D), lambda i: (0, 0))
    buf = pltpu.VMEM((tc * TOKEN_TILE_ROWS, LANES), F32)
    return pl.pallas_call(
        functools.partial(_combine_kernel, tc=tc),
        grid=(T // tc,),
        in_specs=[smem, smem, rows(TOP_K), rows(D), vec, vec, pl.BlockSpec(memory_space=pl.ANY)],
        out_specs=rows(D),
        out_shape=jax.ShapeDtypeStruct((T, D), F32),
        scratch_shapes=[buf, buf, pltpu.SemaphoreType.DMA(())],
        compiler_params=_cparams(1),
        name="moe_combine_ln",
    )(pos_rows[0], pos_rows[1], gates_col, h1, ln_g.reshape(1, -1), ln_b.reshape(1, -1), yg)


def _grouped_moe(h1, h1t, experts, gates, ranks, counts, wg, wu, wd, layer, ln_g, ln_b, tmf=256):
    T = h1.shape[0]
    nb = (T * TOP_K) // tmf + N_EXPERTS
    cnt = counts[:, 0].astype(jnp.int32)
    pcnt = (cnt + tmf - 1) // tmf * tmf
    pends = jnp.cumsum(pcnt)
    pstart = pends - pcnt
    eids = jnp.arange(N_EXPERTS, dtype=jnp.int32)
    seg = jnp.sum(jnp.where(experts[None] == eids[:, None, None], pstart[:, None, None], 0), axis=0)
    pos_rows = (seg + ranks) * TOKEN_TILE_ROWS
    blk_start = jnp.arange(nb, dtype=jnp.int32) * tmf
    blk_e = jnp.minimum(jnp.sum((pends[None, :] <= blk_start[:, None]).astype(jnp.int32), axis=1), N_EXPERTS - 1)
    n_valid = (pends[-1:] // tmf).astype(jnp.int32)
    hole_start = jnp.concatenate([pstart + cnt, pends[-1:]])
    hole_rows = jnp.concatenate([pcnt - cnt, nb * tmf - pends[-1:]])
    xg = _dispatch(h1t, pos_rows, hole_start * TOKEN_TILE_ROWS, hole_rows, nb * tmf)
    yg = _expert_ffn(xg, blk_e, n_valid, wg, wu, wd, layer, tmf)
    return _combine(yg, pos_rows, gates.T, h1, ln_g, ln_b)


def _rope_tables(S):
    half = HEAD_DIM // 2
    inv = ROPE_THETA ** (-jnp.arange(half, dtype=F32) / half)
    ang = jnp.arange(S, dtype=F32)[:, None] * inv[None, :]
    cos, sin = jnp.cos(ang), jnp.sin(ang)
    reps = LANES // HEAD_DIM
    cos_tab = jnp.tile(jnp.concatenate([cos, cos], axis=1), (1, reps))
    sin_tab = jnp.tile(jnp.concatenate([-sin, sin], axis=1), (1, reps))
    return cos_tab, sin_tab


def _pad_w_in(w):
    pad = jnp.zeros((D_MODEL, D_PROJ_PAD - w.shape[1]), w.dtype)
    return jnp.concatenate([w, pad], axis=1).astype(MXU_DTYPE)


def kernel(x, w_in, b_forget, g_dil, g_fox, w_out, ln1_g, ln1_b, w_router, b_router,
           w_gate, w_up, w_down, ln2_g, ln2_b):
    B, S, D = x.shape
    T = B * S
    cos_tab, sin_tab = _rope_tables(S)
    wr_hi, wr_mid, _ = _split3(w_router.astype(F32).T)
    wr3 = jnp.concatenate([wr_hi, wr_mid], axis=0)
    h = x
    for l in range(DEPTH):
        qd, kd, vd, qf, kf, vf, zf = _in_projection(h, _pad_w_in(w_in[l]), cos_tab, sin_tab)
        c = _forget_cumsum(zf, b_forget[l])[:, :N_HEADS_FOX]
        o_fox = _fox_attention(qf, kf, vf, c)
        o_dil = _dilated_attention(qd, kd, vd)
        h1, h1t, experts, gates, ranks, counts = _out_projection(
            o_dil.reshape(T, D_DIL), o_fox.reshape(T, D_FOX), h.reshape(T, D),
            w_out[l].astype(MXU_DTYPE), g_dil[l], g_fox[l], ln1_g[l], ln1_b[l], wr3, b_router)
        h2 = _grouped_moe(h1, h1t, experts, gates, ranks, counts,
                          w_gate, w_up, w_down, l,
                          ln2_g[l], ln2_b[l])
        h = h2.reshape(B, S, D)
    return h
```

```python
import functools

import numpy as np
import jax
import jax.numpy as jnp
from jax import lax
from jax.experimental import pallas as pl
from jax.experimental.pallas import tpu as pltpu

D_MODEL = 1024
DEPTH = 2
HEAD_DIM = 64
N_HEADS_DIL = 12
N_HEADS_FOX = 4
D_DIL = N_HEADS_DIL * HEAD_DIM
D_FOX = N_HEADS_FOX * HEAD_DIM
DILATIONS = (1, 4, 16)
BLOCK = 128
ROPE_THETA = 10000.0
N_EXPERTS = 16
N_GROUPS = 4
EXPERTS_PER_GROUP = 4
TOP_K = 2
DEEPNORM_ALPHA = (2.0 * DEPTH) ** 0.25
LN_EPS = 1e-5
RMS_EPS = 1e-6

LANES = 128
N_PAIRS_DIL = D_DIL // LANES
D_PROJ_PAD = 3 * D_DIL + 3 * D_FOX + LANES
VMEM_LIMIT = 48 * 1024 * 1024
TOKEN_TILE_ROWS = D_MODEL // LANES
DMA_ISSUE_UNROLL = 8
ZERO_CHUNK_TOKENS = 32
DIL_PAD = max(DILATIONS) * BLOCK
MASK_BIAS = -1e30
LOG2_E = 1.4426950408889634
DIL_UNITS_PER_BODY = 16
FOX_TK = 256
FOX_BLOCKS_PER_BODY = 4
ZF_ROWS = 8

MXU_DTYPE = jnp.bfloat16
F32 = jnp.float32
NEG_INF = float("-inf")


def _cparams(n_axes):
    return pltpu.CompilerParams(dimension_semantics=("arbitrary",) * n_axes,
                                vmem_limit_bytes=VMEM_LIMIT)


def _dot(a, b):
    return jnp.dot(a, b, preferred_element_type=F32)


def _dot_nt(a, b):
    return lax.dot_general(a, b, (((1,), (1,)), ((), ())), preferred_element_type=F32)


def _inproj_kernel(x_ref, w_ref, cos_ref, sin_ref,
                   qd_ref, kd_ref, vd_ref, qf_ref, kf_ref, vf_ref, zf_ref):
    xb = x_ref[0].astype(MXU_DTYPE)
    tm = xb.shape[0]
    cos = cos_ref[...]
    sin = sin_ref[...]
    lane = lax.broadcasted_iota(jnp.int32, (tm, LANES), 1)
    first_half = (lane % HEAD_DIM) < (HEAD_DIM // 2)

    def rope(z):
        rot = jnp.where(first_half, pltpu.roll(z, LANES - 32, 1), pltpu.roll(z, 32, 1))
        return z * cos + rot * sin

    def proj(col, width):
        return _dot(xb, w_ref[:, col:col + width])

    scale = HEAD_DIM ** -0.5
    col = 0
    for out_ref, roped, mul in ((qd_ref, True, scale * LOG2_E), (kd_ref, True, 1.0)):
        for c in range(D_DIL // 256):
            z = proj(col, 256)
            for half in range(2):
                zz = rope(z[:, half * LANES:(half + 1) * LANES]) * mul
                out_ref[0, :, c * 256 + half * LANES:c * 256 + (half + 1) * LANES] = zz.astype(out_ref.dtype)
            col += 256
    for c in range(D_DIL // 256):
        vd_ref[0, :, c * 256:(c + 1) * 256] = proj(col, 256).astype(vd_ref.dtype)
        col += 256
    qf_ref[0] = (proj(col, D_FOX) * (scale * LOG2_E)).astype(qf_ref.dtype)
    col += D_FOX
    kf_ref[0] = proj(col, D_FOX).astype(kf_ref.dtype)
    col += D_FOX
    vf_ref[0] = proj(col, D_FOX).astype(vf_ref.dtype)
    col += D_FOX
    zf_ref[0] = proj(col, LANES).T[0:ZF_ROWS, :]


def _in_projection(h, w_pad, cos_tab, sin_tab, tm=512):
    B, S, D = h.shape
    grid = (B, S // tm)
    row = lambda width: pl.BlockSpec((1, tm, width), lambda b, i: (b, i, 0))
    out_shape = (
        jax.ShapeDtypeStruct((B, S, D_DIL), MXU_DTYPE),
        jax.ShapeDtypeStruct((B, S, D_DIL), MXU_DTYPE),
        jax.ShapeDtypeStruct((B, S, D_DIL), MXU_DTYPE),
        jax.ShapeDtypeStruct((B, S, D_FOX), MXU_DTYPE),
        jax.ShapeDtypeStruct((B, S, D_FOX), MXU_DTYPE),
        jax.ShapeDtypeStruct((B, S, D_FOX), MXU_DTYPE),
        jax.ShapeDtypeStruct((B, ZF_ROWS, S), F32),
    )
    return pl.pallas_call(
        _inproj_kernel,
        grid=grid,
        in_specs=[
            row(D),
            pl.BlockSpec((D, D_PROJ_PAD), lambda b, i: (0, 0)),
            pl.BlockSpec((tm, LANES), lambda b, i: (i, 0)),
            pl.BlockSpec((tm, LANES), lambda b, i: (i, 0)),
        ],
        out_specs=(row(D_DIL), row(D_DIL), row(D_DIL), row(D_FOX), row(D_FOX), row(D_FOX),
                   pl.BlockSpec((1, ZF_ROWS, tm), lambda b, i: (b, 0, i))),
        out_shape=out_shape,
        compiler_params=_cparams(2),
        name="in_projection",
    )(h, w_pad, cos_tab, sin_tab)


def _split3(x):
    hi = x.astype(jnp.bfloat16)
    r1 = x - hi.astype(F32)
    mid = r1.astype(jnp.bfloat16)
    lo = (r1 - mid.astype(F32)).astype(jnp.bfloat16)
    return hi, mid, lo


def _forget_cumsum_kernel(z_ref, b_ref, tri_ref, c_ref):
    x = z_ref[0] + b_ref[...]
    logf = jnp.minimum(x, 0.0) - jnp.log1p(jnp.exp(-jnp.abs(x)))
    tri = tri_ref[...]
    S = x.shape[1]
    carry = jnp.zeros((x.shape[0], 1), F32)
    for blk in range(S // LANES):
        seg = logf[:, blk * LANES:(blk + 1) * LANES]
        hi, mid, lo = _split3(seg)
        cs = (_dot(hi, tri) + _dot(mid, tri)) + _dot(lo, tri) + carry
        c_ref[0, :, blk * LANES:(blk + 1) * LANES] = cs * LOG2_E
        carry = cs[:, LANES - 1:LANES]


def _forget_cumsum(zt, b_forget):
    B, Hf, S = zt.shape
    tri = (np.arange(LANES)[:, None] <= np.arange(LANES)[None, :]).astype(np.float32)
    return pl.pallas_call(
        _forget_cumsum_kernel,
        grid=(B,),
        in_specs=[
            pl.BlockSpec((1, Hf, S), lambda b: (b, 0, 0)),
            pl.BlockSpec((Hf, 1), lambda b: (0, 0)),
            pl.BlockSpec((LANES, LANES), lambda b: (0, 0)),
        ],
        out_specs=pl.BlockSpec((1, Hf, S), lambda b: (b, 0, 0)),
        out_shape=jax.ShapeDtypeStruct((B, Hf, S), F32),
        compiler_params=_cparams(1),
        name="forget_cumsum",
    )(zt, jnp.pad(b_forget.astype(F32), (0, Hf - b_forget.shape[0])).reshape(Hf, 1), jnp.asarray(tri, jnp.bfloat16))


def _pair_rows(xp, is_a):
    zero = jnp.zeros_like(xp)
    return jnp.concatenate([jnp.where(is_a, xp, zero), jnp.where(is_a, zero, xp)], axis=0)


def _causal_bias():
    qi = np.arange(BLOCK)[:, None]
    ki = np.arange(FOX_TK)[None, :]
    vis = np.stack([ki <= qi, ki <= qi + BLOCK])
    return np.where(vis, 0.0, MASK_BIAS).astype(np.float32)


def _fox_kernel(q_ref, k_ref, v_ref, crow_ref, cbias_ref, o_ref):
    S = q_ref.shape[1]
    tk = FOX_TK
    lane = lax.broadcasted_iota(jnp.int32, (BLOCK, LANES), 1)
    is_a = lane < HEAD_DIM
    ones = jnp.ones((tk, LANES), MXU_DTYPE)

    def unit(q2, kv, bias):
        kb, v_ones, cr = kv
        s = _dot_nt(q2, kb)
        sa = s[:BLOCK] - cr[0:1]
        sb = s[BLOCK:] - cr[1:2]
        if bias is not None:
            sa = sa + bias
            sb = sb + bias
        s = jnp.concatenate([sa, sb], axis=0)
        m = jnp.max(s, axis=1, keepdims=True)
        p = jnp.exp2(s - m).astype(MXU_DTYPE)
        o = _dot(p, v_ones)
        acc = jnp.where(is_a, o[:BLOCK, :LANES], o[BLOCK:, :LANES])
        l = jnp.where(is_a, o[:BLOCK, LANES:], o[BLOCK:, LANES:])
        return acc, l, jnp.where(is_a, m[:BLOCK], m[BLOCK:])

    def key_block(j):
        k0 = pl.multiple_of(j * tk, tk)
        return (k_ref[0, pl.ds(k0, tk), :],
                jnp.concatenate([v_ref[0, pl.ds(k0, tk), :], ones], axis=1),
                crow_ref[0, 0, :, pl.ds(k0, tk)])

    def fold(state, units):
        num, den, top_old = state
        top = functools.reduce(jnp.maximum, [top_old] + [m for _, _, m in units])
        scale = jnp.exp2(top_old - top)
        num, den = scale * num, scale * den
        for acc, l, m in units:
            w = jnp.exp2(m - top)
            num = num + w * acc
            den = den + w * l
        return num, den, top

    def q_block(i, _):
        r0 = pl.multiple_of(i * tk, tk)
        q2 = [_pair_rows(q_ref[0, pl.ds(r0 + h * BLOCK, BLOCK), :], is_a) for h in range(2)]
        zero = jnp.zeros((BLOCK, LANES), F32)
        empty = (zero, zero, jnp.full((BLOCK, LANES), MASK_BIAS, F32))

        def fold_blocks(states, blocks):
            units = [[unit(q2[h], kv, None if bias is None else bias[h]) for h in range(2)]
                     for kv, bias in ((key_block(j), bias) for j, bias in blocks)]
            return tuple(fold(states[h], [u[h] for u in units]) for h in range(2))

        def full_blocks(g, states):
            return fold_blocks(states, [(FOX_BLOCKS_PER_BODY * g + u, None) for u in range(FOX_BLOCKS_PER_BODY)])

        states = lax.fori_loop(0, i // FOX_BLOCKS_PER_BODY, full_blocks, (empty, empty))
        diag = (i, (cbias_ref[0], cbias_ref[1]))

        def tail(n_left):
            return lambda st: fold_blocks(st, [(i - n_left + u, None) for u in range(n_left)] + [diag])

        states = lax.switch(i % FOX_BLOCKS_PER_BODY, [tail(r) for r in range(FOX_BLOCKS_PER_BODY)], states)
        for h in range(2):
            num, den, _ = states[h]
            o_ref[0, pl.ds(r0 + h * BLOCK, BLOCK), :] = (num / den).astype(o_ref.dtype)
        return 0

    lax.fori_loop(0, S // tk, q_block, 0)


def _fox_attention(qf, kf, vf, c2):
    B, S, _ = qf.shape
    n_pairs = D_FOX // LANES
    c4 = c2.reshape(B, n_pairs, 2, S)
    blk = pl.BlockSpec((1, S, LANES), lambda b, p: (b, 0, p))
    return pl.pallas_call(
        _fox_kernel,
        grid=(B, n_pairs),
        in_specs=[blk, blk, blk,
                  pl.BlockSpec((1, 1, 2, S), lambda b, p: (b, p, 0, 0)),
                  pl.BlockSpec((2, BLOCK, FOX_TK), lambda b, p: (0, 0, 0))],
        out_specs=blk,
        out_shape=jax.ShapeDtypeStruct((B, S, D_FOX), MXU_DTYPE),
        compiler_params=_cparams(2),
        name="fox_attention",
    )(qf, kf, vf, c4, jnp.asarray(_causal_bias()))


def _band_bias():
    qi = np.arange(BLOCK)[:, None]
    ki = np.arange(2 * BLOCK)[None, :]
    delta = BLOCK + qi - ki
    band = (delta >= 0) & (delta <= BLOCK)
    first = band & (ki >= BLOCK)
    return np.where(np.stack([first, band]), 0.0, MASK_BIAS).astype(np.float32)


def _dil_kernel(q_ref, k_ref, v_ref, bias_ref, o_ref, q32, k32, v32, onorm_s, mu_s):
    S = q_ref.shape[1]
    q32[...] = q_ref[0].astype(F32)
    zeros = jnp.zeros((DIL_PAD, LANES), F32)
    k32[0:DIL_PAD, :] = zeros
    v32[0:DIL_PAD, :] = zeros
    k32[DIL_PAD:, :] = k_ref[0].astype(F32)
    v32[DIL_PAD:, :] = v_ref[0].astype(F32)

    lane = lax.broadcasted_iota(jnp.int32, (BLOCK, LANES), 1)
    is_a = lane < HEAD_DIM
    ones = jnp.ones((2 * BLOCK, LANES), MXU_DTYPE)

    def rows(ref, start, count, dil):
        if dil == 1:
            return ref[pl.ds(start, count), :]
        return ref[pl.ds(start, count, stride=dil), :]

    def unit(dil, rho, n):
        q0 = rho + dil * BLOCK * n
        qb = rows(q32, q0, BLOCK, dil).astype(MXU_DTYPE)
        kw = rows(k32, q0 + DIL_PAD - dil * BLOCK, 2 * BLOCK, dil).astype(MXU_DTYPE)
        vw = rows(v32, q0 + DIL_PAD - dil * BLOCK, 2 * BLOCK, dil).astype(MXU_DTYPE)
        bias = bias_ref[jnp.minimum(n, 1)]
        s = _dot_nt(_pair_rows(qb, is_a), kw)
        s = jnp.concatenate([s[:BLOCK] + bias, s[BLOCK:] + bias], axis=0)
        m = jnp.max(s, axis=1, keepdims=True)
        p = jnp.exp2(s - m).astype(MXU_DTYPE)
        o = _dot(p, jnp.concatenate([vw, ones], axis=1))
        acc = jnp.where(is_a, o[:BLOCK, :LANES], o[BLOCK:, :LANES])
        l = jnp.where(is_a, o[:BLOCK, LANES:], o[BLOCK:, LANES:])
        mu = jnp.where(is_a, m[:BLOCK], m[BLOCK:]) + jnp.log2(l)
        return acc / l, mu

    n_units = S // BLOCK

    for slab, dil in enumerate(d for d in DILATIONS if d != 1):
        blk_bits = (n_units // dil).bit_length() - 1

        def several_units(g, _, slab=slab, dil=dil, blk_bits=blk_bits):
            for u in range(DIL_UNITS_PER_BODY):
                t = g * DIL_UNITS_PER_BODY + u
                rho = lax.shift_right_logical(t, blk_bits)
                n = t & ((1 << blk_bits) - 1)
                o_n, mu = unit(dil, rho, n)
                q0 = rho + dil * BLOCK * n
                onorm_s[slab, pl.ds(q0, BLOCK, stride=dil), :] = o_n
                mu_s[slab, pl.ds(q0, BLOCK, stride=dil), :] = mu
            return 0

        lax.fori_loop(0, n_units // DIL_UNITS_PER_BODY, several_units, 0)

    def merge_blocks(g, _):
        for u in range(DIL_UNITS_PER_BODY):
            n = g * DIL_UNITS_PER_BODY + u
            o_1, mu_1 = unit(1, 0, n)
            r0 = pl.multiple_of(n * BLOCK, BLOCK)
            others = [(onorm_s[sl, pl.ds(r0, BLOCK), :], mu_s[sl, pl.ds(r0, BLOCK), :])
                      for sl in range(len(DILATIONS) - 1)]
            top = functools.reduce(jnp.maximum, [mu_1] + [mu for _, mu in others])
            w = jnp.exp2(mu_1 - top)
            num, den = w * o_1, w
            for o_p, mu_p in others:
                w = jnp.exp2(mu_p - top)
                num = num + w * o_p
                den = den + w
            o_ref[0, pl.ds(r0, BLOCK), :] = (num / den).astype(o_ref.dtype)
        return 0

    lax.fori_loop(0, n_units // DIL_UNITS_PER_BODY, merge_blocks, 0)


def _dilated_attention(qd, kd, vd):
    B, S, _ = qd.shape
    blk = pl.BlockSpec((1, S, LANES), lambda b, p: (b, 0, p))
    n_slabs = len(DILATIONS) - 1
    return pl.pallas_call(
        _dil_kernel,
        grid=(B, N_PAIRS_DIL),
        in_specs=[blk, blk, blk, pl.BlockSpec((2, BLOCK, 2 * BLOCK), lambda b, p: (0, 0, 0))],
        out_specs=blk,
        out_shape=jax.ShapeDtypeStruct((B, S, D_DIL), MXU_DTYPE),
        scratch_shapes=[
            pltpu.VMEM((S, LANES), F32),
            pltpu.VMEM((DIL_PAD + S, LANES), F32),
            pltpu.VMEM((DIL_PAD + S, LANES), F32),
            pltpu.VMEM((n_slabs, S, LANES), F32),
            pltpu.VMEM((n_slabs, S, LANES), F32),
        ],
        compiler_params=_cparams(2),
        name="dilated_attention",
    )(qd, kd, vd, jnp.asarray(_band_bias()))


def _layer_norm(u, g, b):
    mu = jnp.mean(u, axis=1, keepdims=True)
    d = u - mu
    var = jnp.mean(d * d, axis=1, keepdims=True)
    return d * lax.rsqrt(var + LN_EPS) * g + b


def _rms_norm(x, g):
    ms = jnp.mean(x * x, axis=1, keepdims=True)
    return x * lax.rsqrt(ms + RMS_EPS) * g


def _top2_of4(vals):
    v1, i1 = vals[0], jnp.zeros(vals[0].shape, jnp.int32)
    for i in range(1, 4):
        better = vals[i] > v1
        v1 = jnp.where(better, vals[i], v1)
        i1 = jnp.where(better, i, i1)
    v2 = jnp.full(vals[0].shape, -1.0, F32)
    i2 = jnp.zeros(vals[0].shape, jnp.int32)
    for i in range(4):
        better = (vals[i] > v2) & (i1 != i)
        v2 = jnp.where(better, vals[i], v2)
        i2 = jnp.where(better, i, i2)
    return v1, i1, v2, i2


def _outproj_kernel(od_ref, of_ref, h_ref, wo_ref, gd_ref, gf_ref, lng_ref, lnb_ref,
                    wr_ref, br_ref, tri_ref,
                    h1_ref, h1t_ref, e_ref, gate_ref, rank_ref, cnt_ref, base_ref):
    step = pl.program_id(0)

    @pl.when(step == 0)
    def _():
        base_ref[...] = jnp.zeros_like(base_ref)

    xd = _rms_norm(od_ref[...].astype(F32), gd_ref[...])
    xf = _rms_norm(of_ref[...].astype(F32), gf_ref[...])
    y = _dot(xd.astype(MXU_DTYPE), wo_ref[0:D_DIL, :]) + _dot(xf.astype(MXU_DTYPE), wo_ref[D_DIL:, :])
    h1 = _layer_norm(DEEPNORM_ALPHA * h_ref[...] + y, lng_ref[...], lnb_ref[...])
    h1_ref[...] = h1
    _store_token_tiles(h1t_ref, h1)
    tm = h1.shape[0]

    h_hi, h_mid, _ = _split3(h1)
    two = _dot_nt(wr_ref[...], h_hi)
    logits = (two[:N_EXPERTS] + two[N_EXPERTS:]) + _dot_nt(wr_ref[0:N_EXPERTS, :], h_mid) + br_ref[...]
    logits = logits - jnp.max(logits, axis=0, keepdims=True)
    ex = jnp.exp(logits)
    probs = ex / jnp.sum(ex, axis=0, keepdims=True)
    pr = [probs[j:j + 1, :] for j in range(N_EXPERTS)]

    def group_score(g):
        v = pr[4 * g:4 * g + 4]
        pairs = [v[a] + v[b] for a in range(4) for b in range(a + 1, 4)]
        return functools.reduce(jnp.maximum, pairs)

    best = group_score(0)
    gsel = jnp.zeros((1, tm), jnp.int32)
    for g in range(1, N_GROUPS):
        sc = group_score(g)
        better = sc > best
        best = jnp.where(better, sc, best)
        gsel = jnp.where(better, g, gsel)
    in_grp = []
    for i in range(EXPERTS_PER_GROUP):
        v = pr[i]
        for g in range(1, N_GROUPS):
            v = jnp.where(gsel == g, pr[4 * g + i], v)
        in_grp.append(v)
    v1, i1, v2, i2 = _top2_of4(in_grp)
    e1 = gsel * EXPERTS_PER_GROUP + i1
    e2 = gsel * EXPERTS_PER_GROUP + i2
    den = v1 + v2
    e_ref[...] = jnp.concatenate([e1, e2], axis=0)
    gate_ref[...] = jnp.concatenate([v1 / den, v2 / den], axis=0)

    eidx = lax.broadcasted_iota(jnp.int32, (N_EXPERTS, tm), 0)
    oh1 = (eidx == e1).astype(F32)
    oh2 = (eidx == e2).astype(F32)
    tot = oh1 + oh2
    before = base_ref[...] + _dot(tot.astype(jnp.bfloat16), tri_ref[...])
    r1 = jnp.sum(oh1 * before, axis=0, keepdims=True)
    r2 = jnp.sum(oh2 * before, axis=0, keepdims=True)
    rank_ref[...] = jnp.concatenate([r1, r2], axis=0).astype(jnp.int32)
    base_ref[...] = base_ref[...] + jnp.sum(tot, axis=1, keepdims=True)
    cnt_ref[...] = jnp.broadcast_to(base_ref[...], cnt_ref.shape)


def _out_projection(od, of, h, w_out, g_dil, g_fox, ln_g, ln_b, wr3, b_router, tm=512):
    T = h.shape[0]
    tri = (np.arange(tm)[:, None] < np.arange(tm)[None, :]).astype(np.float32)
    rows = lambda width: pl.BlockSpec((tm, width), lambda i: (i, 0))
    full = lambda a: pl.BlockSpec(a.shape, lambda i: (0,) * a.ndim)
    tok = pl.BlockSpec((TOP_K, tm), lambda i: (0, i))
    consts = [w_out, g_dil.reshape(1, -1), g_fox.reshape(1, -1), ln_g.reshape(1, -1), ln_b.reshape(1, -1),
              wr3, b_router.reshape(-1, 1).astype(F32), jnp.asarray(tri, jnp.bfloat16)]
    return pl.pallas_call(
        _outproj_kernel,
        grid=(T // tm,),
        in_specs=[rows(D_DIL), rows(D_FOX), rows(D_MODEL)] + [full(a) for a in consts],
        out_specs=(rows(D_MODEL), pl.BlockSpec((tm * TOKEN_TILE_ROWS, LANES), lambda i: (i, 0)),
                   tok, tok, tok, pl.BlockSpec((N_EXPERTS, LANES), lambda i: (0, 0))),
        out_shape=(
            jax.ShapeDtypeStruct((T, D_MODEL), F32),
            jax.ShapeDtypeStruct((T * TOKEN_TILE_ROWS, LANES), F32),
            jax.ShapeDtypeStruct((TOP_K, T), jnp.int32),
            jax.ShapeDtypeStruct((TOP_K, T), F32),
            jax.ShapeDtypeStruct((TOP_K, T), jnp.int32),
            jax.ShapeDtypeStruct((N_EXPERTS, LANES), F32),
        ),
        scratch_shapes=[pltpu.VMEM((N_EXPERTS, 1), F32)],
        compiler_params=_cparams(1),
        name="out_projection_router",
    )(od, of, h, *consts)


def _store_token_tiles(ref, x):
    n = x.shape[0]
    for c in range(TOKEN_TILE_ROWS):
        ref[pl.ds(c, n, stride=TOKEN_TILE_ROWS), :] = x[:, c * LANES:(c + 1) * LANES]


def _load_token_tiles(ref, n):
    return jnp.concatenate([ref[pl.ds(c, n, stride=TOKEN_TILE_ROWS), :] for c in range(TOKEN_TILE_ROWS)], axis=1)


def _tile_copy(src, src_row, dst, dst_row, sem):
    return pltpu.make_async_copy(src.at[pl.ds(src_row, TOKEN_TILE_ROWS)],
                                 dst.at[pl.ds(dst_row, TOKEN_TILE_ROWS)], sem)


def _dispatch_kernel(p0_ref, p1_ref, pad0_ref, padn_ref, h_ref, xg_out, sem, ztile, *, td):
    def copies(t):
        src = pl.multiple_of(t * TOKEN_TILE_ROWS, TOKEN_TILE_ROWS)
        return (_tile_copy(h_ref, src, xg_out, pl.multiple_of(p0_ref[t], TOKEN_TILE_ROWS), sem),
                _tile_copy(h_ref, src, xg_out, pl.multiple_of(p1_ref[t], TOKEN_TILE_ROWS), sem))

    def issue(t, _):
        for thread, cp in enumerate(copies(t)):
            cp.start(priority=thread)
        return 0

    def drain(t, _):
        for cp in copies(t):
            cp.wait()
        return 0

    lax.fori_loop(0, td, issue, 0, unroll=DMA_ISSUE_UNROLL)
    lax.fori_loop(0, td, drain, 0, unroll=DMA_ISSUE_UNROLL)

    @pl.when(pl.program_id(0) == pl.num_programs(0) - 1)
    def _():
        ztile[...] = jnp.zeros(ztile.shape, ztile.dtype)

        chunk_rows = ztile.shape[0]
        chunk_tokens = chunk_rows // TOKEN_TILE_ROWS

        def for_each_zero_copy(action):
            def hole(e, _):
                n_chunks = padn_ref[e] // chunk_tokens

                def chunk(c, _):
                    dst = pl.multiple_of(pad0_ref[e] + c * chunk_rows, TOKEN_TILE_ROWS)
                    action(pltpu.make_async_copy(ztile, xg_out.at[pl.ds(dst, chunk_rows)], sem))
                    return 0

                def single(r, _):
                    dst = pl.multiple_of(pad0_ref[e] + r * TOKEN_TILE_ROWS, TOKEN_TILE_ROWS)
                    action(_tile_copy(ztile, 0, xg_out, dst, sem))
                    return 0

                lax.fori_loop(0, n_chunks, chunk, 0)
                lax.fori_loop(n_chunks * chunk_tokens, padn_ref[e], single, 0)
                return 0
            lax.fori_loop(0, pad0_ref.shape[0], hole, 0)

        for_each_zero_copy(lambda cp: cp.start())
        for_each_zero_copy(lambda cp: cp.wait())


def _dispatch(h1t, pos_rows, pad_rows, pad_counts, n_rows, td=256):
    T = h1t.shape[0] // TOKEN_TILE_ROWS
    smem = pl.BlockSpec((td,), lambda i: (i,), memory_space=pltpu.SMEM)
    smem_all = pl.BlockSpec(memory_space=pltpu.SMEM)
    return pl.pallas_call(
        functools.partial(_dispatch_kernel, td=td),
        grid=(T // td,),
        in_specs=[smem, smem, smem_all, smem_all, pl.BlockSpec((td * TOKEN_TILE_ROWS, LANES), lambda i: (i, 0))],
        out_specs=pl.BlockSpec(memory_space=pl.ANY),
        out_shape=jax.ShapeDtypeStruct((n_rows * TOKEN_TILE_ROWS, LANES), F32),
        scratch_shapes=[pltpu.SemaphoreType.DMA(()), pltpu.VMEM((ZERO_CHUNK_TOKENS * TOKEN_TILE_ROWS, LANES), F32)],
        compiler_params=_cparams(1),
        name="moe_dispatch",
    )(pos_rows[0], pos_rows[1], pad_rows, pad_counts, h1t)


def _ffn_kernel(be_ref, nv_ref, slot_ref, nxt_ref, x_ref, wg_hbm, wu_hbm, wd_hbm, y_ref, wstage, wb, sem,
                *, tmf, layer):
    j = pl.program_id(0)
    valid = j < nv_ref[0]

    def fetch(expert, slot):
        return [pltpu.make_async_copy(w.at[layer, expert], wstage.at[slot, k], sem.at[slot])
                for k, w in enumerate((wg_hbm, wu_hbm, wd_hbm))]

    @pl.when(j == 0)
    def _():
        for cp in fetch(be_ref[0], 0):
            cp.start()

    @pl.when(valid & ((j == 0) | (be_ref[j] != be_ref[jnp.maximum(j - 1, 0)])))
    def _():
        slot = slot_ref[j]
        for cp in fetch(be_ref[j], slot):
            cp.wait()
        for k in range(3):
            wb[k] = wstage[slot, k].astype(MXU_DTYPE)

        @pl.when(nxt_ref[j] >= 0)
        def _():
            for cp in fetch(nxt_ref[j], 1 - slot):
                cp.start()

    @pl.when(valid)
    def _():
        xb = _load_token_tiles(x_ref, tmf).astype(MXU_DTYPE)
        a = _dot(xb, wb[0])
        u = _dot(xb, wb[1])
        hmid = (a * jax.nn.sigmoid(a)) * u
        _store_token_tiles(y_ref, _dot(hmid.astype(MXU_DTYPE), wb[2]))

    @pl.when(j >= nv_ref[0])
    def _():
        y_ref[...] = jnp.zeros_like(y_ref)


def _expert_ffn(xg, blk_e, n_valid, present, wg, wu, wd, layer, tmf):
    n_rows = xg.shape[0] // TOKEN_TILE_ROWS
    D = D_MODEL
    nb = n_rows // tmf
    eids = jnp.arange(N_EXPERTS, dtype=jnp.int32)
    order = jnp.cumsum(present.astype(jnp.int32)) - 1
    later = (eids[None, :] > eids[:, None]) & present[None, :]
    nxt_e = jnp.min(jnp.where(later, eids[None, :], N_EXPERTS), axis=1)
    nxt_e = jnp.where(nxt_e == N_EXPERTS, -1, nxt_e)
    slot = (order & 1)[blk_e].astype(jnp.int32)
    nxt = nxt_e[blk_e].astype(jnp.int32)
    xmap = lambda j, be, nv, sl, nx: (jnp.minimum(j, nv[0] - 1), 0)
    anyspec = pl.BlockSpec(memory_space=pl.ANY)
    grid_spec = pltpu.PrefetchScalarGridSpec(
        num_scalar_prefetch=4,
        grid=(nb,),
        in_specs=[pl.BlockSpec((tmf * TOKEN_TILE_ROWS, LANES), xmap), anyspec, anyspec, anyspec],
        out_specs=pl.BlockSpec((tmf * TOKEN_TILE_ROWS, LANES), lambda j, be, nv, sl, nx: (j, 0)),
        scratch_shapes=[pltpu.VMEM((2, 3, D, D), F32), pltpu.VMEM((3, D, D), MXU_DTYPE),
                        pltpu.SemaphoreType.DMA((2,))],
    )
    return pl.pallas_call(
        functools.partial(_ffn_kernel, tmf=tmf, layer=layer),
        grid_spec=grid_spec,
        out_shape=jax.ShapeDtypeStruct(xg.shape, F32),
        compiler_params=_cparams(1),
        name="moe_expert_ffn",
    )(blk_e, n_valid, slot, nxt, xg, wg, wu, wd)


def _combine_kernel(p0_ref, p1_ref, gate_ref, h_ref, lng_ref, lnb_ref, y_hbm, o_ref, buf0, buf1, sem, *, tc):
    def copies(t):
        dst = pl.multiple_of(t * TOKEN_TILE_ROWS, TOKEN_TILE_ROWS)
        return (_tile_copy(y_hbm, pl.multiple_of(p0_ref[t], TOKEN_TILE_ROWS), buf0, dst, sem),
                _tile_copy(y_hbm, pl.multiple_of(p1_ref[t], TOKEN_TILE_ROWS), buf1, dst, sem))

    def issue(t, _):
        for thread, cp in enumerate(copies(t)):
            cp.start(priority=thread)
        return 0

    def drain(t, _):
        for cp in copies(t):
            cp.wait()
        return 0

    lax.fori_loop(0, tc, issue, 0, unroll=DMA_ISSUE_UNROLL)
    lax.fori_loop(0, tc, drain, 0, unroll=DMA_ISSUE_UNROLL)
    g = gate_ref[...]
    y = _load_token_tiles(buf0, tc) * g[:, 0:1] + _load_token_tiles(buf1, tc) * g[:, 1:2]
    o_ref[...] = _layer_norm(DEEPNORM_ALPHA * h_ref[...] + y, lng_ref[...], lnb_ref[...])


def _combine(yg, pos_rows, gates_col, h1, ln_g, ln_b, tc=256):
    T, D = h1.shape
    smem = pl.BlockSpec((tc,), lambda i: (i,), memory_space=pltpu.SMEM)
    rows = lambda width: pl.BlockSpec((tc, width), lambda i: (i, 0))
    vec = pl.BlockSpec((1, D), lambda i: (0, 0))
    buf = pltpu.VMEM((tc * TOKEN_TILE_ROWS, LANES), F32)
    return pl.pallas_call(
        functools.partial(_combine_kernel, tc=tc),
        grid=(T // tc,),
        in_specs=[smem, smem, rows(TOP_K), rows(D), vec, vec, pl.BlockSpec(memory_space=pl.ANY)],
        out_specs=rows(D),
        out_shape=jax.ShapeDtypeStruct((T, D), F32),
        scratch_shapes=[buf, buf, pltpu.SemaphoreType.DMA(())],
        compiler_params=_cparams(1),
        name="moe_combine_ln",
    )(pos_rows[0], pos_rows[1], gates_col, h1, ln_g.reshape(1, -1), ln_b.reshape(1, -1), yg)


def _grouped_moe(h1, h1t, experts, gates, ranks, counts, wg, wu, wd, layer, ln_g, ln_b, tmf=256):
    T = h1.shape[0]
    nb = (T * TOP_K) // tmf + N_EXPERTS
    cnt = counts[:, 0].astype(jnp.int32)
    pcnt = (cnt + tmf - 1) // tmf * tmf
    pends = jnp.cumsum(pcnt)
    pstart = pends - pcnt
    eids = jnp.arange(N_EXPERTS, dtype=jnp.int32)
    seg = jnp.sum(jnp.where(experts[None] == eids[:, None, None], pstart[:, None, None], 0), axis=0)
    pos_rows = (seg + ranks) * TOKEN_TILE_ROWS
    blk_start = jnp.arange(nb, dtype=jnp.int32) * tmf
    blk_e = jnp.minimum(jnp.sum((pends[None, :] <= blk_start[:, None]).astype(jnp.int32), axis=1), N_EXPERTS - 1)
    n_valid = (pends[-1:] // tmf).astype(jnp.int32)
    hole_start = jnp.concatenate([pstart + cnt, pends[-1:]])
    hole_rows = jnp.concatenate([pcnt - cnt, nb * tmf - pends[-1:]])
    xg = _dispatch(h1t, pos_rows, hole_start * TOKEN_TILE_ROWS, hole_rows, nb * tmf)
    yg = _expert_ffn(xg, blk_e, n_valid, cnt > 0, wg, wu, wd, layer, tmf)
    return _combine(yg, pos_rows, gates.T, h1, ln_g, ln_b)


def _rope_tables(S):
    half = HEAD_DIM // 2
    inv = ROPE_THETA ** (-jnp.arange(half, dtype=F32) / half)
    ang = jnp.arange(S, dtype=F32)[:, None] * inv[None, :]
    cos, sin = jnp.cos(ang), jnp.sin(ang)
    reps = LANES // HEAD_DIM
    cos_tab = jnp.tile(jnp.concatenate([cos, cos], axis=1), (1, reps))
    sin_tab = jnp.tile(jnp.concatenate([-sin, sin], axis=1), (1, reps))
    return cos_tab, sin_tab


def _pad_w_in(w):
    pad = jnp.zeros((D_MODEL, D_PROJ_PAD - w.shape[1]), w.dtype)
    return jnp.concatenate([w, pad], axis=1).astype(MXU_DTYPE)


def kernel(x, w_in, b_forget, g_dil, g_fox, w_out, ln1_g, ln1_b, w_router, b_router,
           w_gate, w_up, w_down, ln2_g, ln2_b):
    B, S, D = x.shape
    T = B * S
    cos_tab, sin_tab = _rope_tables(S)
    wr_hi, wr_mid, _ = _split3(w_router.astype(F32).T)
    wr3 = jnp.concatenate([wr_hi, wr_mid], axis=0)
    h = x
    for l in range(DEPTH):
        qd, kd, vd, qf, kf, vf, zf = _in_projection(h, _pad_w_in(w_in[l]), cos_tab, sin_tab)
        c = _forget_cumsum(zf, b_forget[l])[:, :N_HEADS_FOX]
        o_fox = _fox_attention(qf, kf, vf, c)
        o_dil = _dilated_attention(qd, kd, vd)
        h1, h1t, experts, gates, ranks, counts = _out_projection(
            o_dil.reshape(T, D_DIL), o_fox.reshape(T, D_FOX), h.reshape(T, D),
            w_out[l].astype(MXU_DTYPE), g_dil[l], g_fox[l], ln1_g[l], ln1_b[l], wr3, b_router)
        h2 = _grouped_moe(h1, h1t, experts, gates, ranks, counts,
                          w_gate, w_up, w_down, l,
                          ln2_g[l], ln2_b[l])
        h = h2.reshape(B, S, D)
    return h
```

```python
import functools

import numpy as np
import jax
import jax.numpy as jnp
from jax import lax
from jax.experimental import pallas as pl
from jax.experimental.pallas import tpu as pltpu

D_MODEL = 1024
DEPTH = 2
HEAD_DIM = 64
N_HEADS_DIL = 12
N_HEADS_FOX = 4
D_DIL = N_HEADS_DIL * HEAD_DIM
D_FOX = N_HEADS_FOX * HEAD_DIM
DILATIONS = (1, 4, 16)
BLOCK = 128
ROPE_THETA = 10000.0
N_EXPERTS = 16
N_GROUPS = 4
EXPERTS_PER_GROUP = 4
TOP_K = 2
DEEPNORM_ALPHA = (2.0 * DEPTH) ** 0.25
LN_EPS = 1e-5
RMS_EPS = 1e-6

LANES = 128
N_PAIRS_DIL = D_DIL // LANES
D_PROJ_PAD = 3 * D_DIL + 3 * D_FOX + LANES
VMEM_LIMIT = 48 * 1024 * 1024
TOKEN_TILE_ROWS = D_MODEL // LANES
DMA_ISSUE_UNROLL = 8
ZERO_CHUNK_TOKENS = 32
SKEW_DIL = 16
SKEW_PITCH = SKEW_DIL + 1
DIL_PAD = max(d for d in DILATIONS if d != SKEW_DIL) * BLOCK
SKEW_PAD = SKEW_PITCH * BLOCK
MASK_BIAS = -1e30
LOG2_E = 1.4426950408889634
DIL_UNITS_PER_BODY = 16
FOX_TK = 256
FOX_BLOCKS_PER_BODY = 4
ZF_ROWS = 8

MXU_DTYPE = jnp.bfloat16
F32 = jnp.float32
NEG_INF = float("-inf")


def _cparams(n_axes):
    return pltpu.CompilerParams(dimension_semantics=("arbitrary",) * n_axes,
                                vmem_limit_bytes=VMEM_LIMIT)


def _dot(a, b):
    return jnp.dot(a, b, preferred_element_type=F32)


def _dot_nt(a, b):
    return lax.dot_general(a, b, (((1,), (1,)), ((), ())), preferred_element_type=F32)


def _inproj_kernel(x_ref, w_ref, cos_ref, sin_ref,
                   qd_ref, kd_ref, vd_ref, qf_ref, kf_ref, vf_ref, zf_ref):
    xb = x_ref[0].astype(MXU_DTYPE)
    tm = xb.shape[0]
    cos = cos_ref[...]
    sin = sin_ref[...]
    lane = lax.broadcasted_iota(jnp.int32, (tm, LANES), 1)
    first_half = (lane % HEAD_DIM) < (HEAD_DIM // 2)

    def rope(z):
        rot = jnp.where(first_half, pltpu.roll(z, LANES - 32, 1), pltpu.roll(z, 32, 1))
        return z * cos + rot * sin

    def proj(col, width):
        return _dot(xb, w_ref[:, col:col + width])

    scale = HEAD_DIM ** -0.5
    col = 0
    for out_ref, roped, mul in ((qd_ref, True, scale * LOG2_E), (kd_ref, True, 1.0)):
        for c in range(D_DIL // 256):
            z = proj(col, 256)
            for half in range(2):
                zz = rope(z[:, half * LANES:(half + 1) * LANES]) * mul
                out_ref[0, :, c * 256 + half * LANES:c * 256 + (half + 1) * LANES] = zz.astype(out_ref.dtype)
            col += 256
    for c in range(D_DIL // 256):
        vd_ref[0, :, c * 256:(c + 1) * 256] = proj(col, 256).astype(vd_ref.dtype)
        col += 256
    qf_ref[0] = (proj(col, D_FOX) * (scale * LOG2_E)).astype(qf_ref.dtype)
    col += D_FOX
    kf_ref[0] = proj(col, D_FOX).astype(kf_ref.dtype)
    col += D_FOX
    vf_ref[0] = proj(col, D_FOX).astype(vf_ref.dtype)
    col += D_FOX
    zf_ref[0] = proj(col, LANES).T[0:ZF_ROWS, :]


def _in_projection(h, w_pad, cos_tab, sin_tab, tm=512):
    B, S, D = h.shape
    grid = (B, S // tm)
    row = lambda width: pl.BlockSpec((1, tm, width), lambda b, i: (b, i, 0))
    out_shape = (
        jax.ShapeDtypeStruct((B, S, D_DIL), MXU_DTYPE),
        jax.ShapeDtypeStruct((B, S, D_DIL), MXU_DTYPE),
        jax.ShapeDtypeStruct((B, S, D_DIL), MXU_DTYPE),
        jax.ShapeDtypeStruct((B, S, D_FOX), MXU_DTYPE),
        jax.ShapeDtypeStruct((B, S, D_FOX), MXU_DTYPE),
        jax.ShapeDtypeStruct((B, S, D_FOX), MXU_DTYPE),
        jax.ShapeDtypeStruct((B, ZF_ROWS, S), F32),
    )
    return pl.pallas_call(
        _inproj_kernel,
        grid=grid,
        in_specs=[
            row(D),
            pl.BlockSpec((D, D_PROJ_PAD), lambda b, i: (0, 0)),
            pl.BlockSpec((tm, LANES), lambda b, i: (i, 0)),
            pl.BlockSpec((tm, LANES), lambda b, i: (i, 0)),
        ],
        out_specs=(row(D_DIL), row(D_DIL), row(D_DIL), row(D_FOX), row(D_FOX), row(D_FOX),
                   pl.BlockSpec((1, ZF_ROWS, tm), lambda b, i: (b, 0, i))),
        out_shape=out_shape,
        compiler_params=_cparams(2),
        name="in_projection",
    )(h, w_pad, cos_tab, sin_tab)


def _split3(x):
    hi = x.astype(jnp.bfloat16)
    r1 = x - hi.astype(F32)
    mid = r1.astype(jnp.bfloat16)
    lo = (r1 - mid.astype(F32)).astype(jnp.bfloat16)
    return hi, mid, lo


def _forget_cumsum_kernel(z_ref, b_ref, tri_ref, c_ref):
    x = z_ref[0] + b_ref[...]
    logf = jnp.minimum(x, 0.0) - jnp.log1p(jnp.exp(-jnp.abs(x)))
    tri = tri_ref[...]
    S = x.shape[1]
    carry = jnp.zeros((x.shape[0], 1), F32)
    for blk in range(S // LANES):
        seg = logf[:, blk * LANES:(blk + 1) * LANES]
        hi, mid, lo = _split3(seg)
        cs = (_dot(hi, tri) + _dot(mid, tri)) + _dot(lo, tri) + carry
        c_ref[0, :, blk * LANES:(blk + 1) * LANES] = cs * LOG2_E
        carry = cs[:, LANES - 1:LANES]


def _forget_cumsum(zt, b_forget):
    B, Hf, S = zt.shape
    tri = (np.arange(LANES)[:, None] <= np.arange(LANES)[None, :]).astype(np.float32)
    return pl.pallas_call(
        _forget_cumsum_kernel,
        grid=(B,),
        in_specs=[
            pl.BlockSpec((1, Hf, S), lambda b: (b, 0, 0)),
            pl.BlockSpec((Hf, 1), lambda b: (0, 0)),
            pl.BlockSpec((LANES, LANES), lambda b: (0, 0)),
        ],
        out_specs=pl.BlockSpec((1, Hf, S), lambda b: (b, 0, 0)),
        out_shape=jax.ShapeDtypeStruct((B, Hf, S), F32),
        compiler_params=_cparams(1),
        name="forget_cumsum",
    )(zt, jnp.pad(b_forget.astype(F32), (0, Hf - b_forget.shape[0])).reshape(Hf, 1), jnp.asarray(tri, jnp.bfloat16))


def _pair_rows(xp, is_a):
    zero = jnp.zeros_like(xp)
    return jnp.concatenate([jnp.where(is_a, xp, zero), jnp.where(is_a, zero, xp)], axis=0)


def _causal_bias():
    qi = np.arange(BLOCK)[:, None]
    ki = np.arange(FOX_TK)[None, :]
    vis = np.stack([ki <= qi, ki <= qi + BLOCK])
    return np.where(vis, 0.0, MASK_BIAS).astype(np.float32)


def _fox_kernel(q_ref, k_ref, v_ref, crow_ref, cbias_ref, o_ref):
    S = q_ref.shape[1]
    tk = FOX_TK
    lane = lax.broadcasted_iota(jnp.int32, (BLOCK, LANES), 1)
    is_a = lane < HEAD_DIM
    ones = jnp.ones((tk, LANES), MXU_DTYPE)

    def unit(q2, kv, bias):
        kb, v_ones, cr = kv
        s = _dot_nt(q2, kb)
        sa = s[:BLOCK] - cr[0:1]
        sb = s[BLOCK:] - cr[1:2]
        if bias is not None:
            sa = sa + bias
            sb = sb + bias
        s = jnp.concatenate([sa, sb], axis=0)
        m = jnp.max(s, axis=1, keepdims=True)
        p = jnp.exp2(s - m).astype(MXU_DTYPE)
        o = _dot(p, v_ones)
        acc = jnp.where(is_a, o[:BLOCK, :LANES], o[BLOCK:, :LANES])
        l = jnp.where(is_a, o[:BLOCK, LANES:], o[BLOCK:, LANES:])
        return acc, l, jnp.where(is_a, m[:BLOCK], m[BLOCK:])

    def key_block(j):
        k0 = pl.multiple_of(j * tk, tk)
        return (k_ref[0, pl.ds(k0, tk), :],
                jnp.concatenate([v_ref[0, pl.ds(k0, tk), :], ones], axis=1),
                crow_ref[0, 0, :, pl.ds(k0, tk)])

    def fold(state, units):
        num, den, top_old = state
        top = functools.reduce(jnp.maximum, [top_old] + [m for _, _, m in units])
        scale = jnp.exp2(top_old - top)
        num, den = scale * num, scale * den
        for acc, l, m in units:
            w = jnp.exp2(m - top)
            num = num + w * acc
            den = den + w * l
        return num, den, top

    def q_block(i, _):
        r0 = pl.multiple_of(i * tk, tk)
        q2 = [_pair_rows(q_ref[0, pl.ds(r0 + h * BLOCK, BLOCK), :], is_a) for h in range(2)]
        zero = jnp.zeros((BLOCK, LANES), F32)
        empty = (zero, zero, jnp.full((BLOCK, LANES), MASK_BIAS, F32))

        def fold_blocks(states, blocks):
            units = [[unit(q2[h], kv, None if bias is None else bias[h]) for h in range(2)]
                     for kv, bias in ((key_block(j), bias) for j, bias in blocks)]
            return tuple(fold(states[h], [u[h] for u in units]) for h in range(2))

        def full_blocks(g, states):
            return fold_blocks(states, [(FOX_BLOCKS_PER_BODY * g + u, None) for u in range(FOX_BLOCKS_PER_BODY)])

        states = lax.fori_loop(0, i // FOX_BLOCKS_PER_BODY, full_blocks, (empty, empty))
        diag = (i, (cbias_ref[0], cbias_ref[1]))

        def tail(n_left):
            return lambda st: fold_blocks(st, [(i - n_left + u, None) for u in range(n_left)] + [diag])

        states = lax.switch(i % FOX_BLOCKS_PER_BODY, [tail(r) for r in range(FOX_BLOCKS_PER_BODY)], states)
        for h in range(2):
            num, den, _ = states[h]
            o_ref[0, pl.ds(r0 + h * BLOCK, BLOCK), :] = (num / den).astype(o_ref.dtype)
        return 0

    lax.fori_loop(0, S // tk, q_block, 0)


def _fox_attention(qf, kf, vf, c2):
    B, S, _ = qf.shape
    n_pairs = D_FOX // LANES
    c4 = c2.reshape(B, n_pairs, 2, S)
    blk = pl.BlockSpec((1, S, LANES), lambda b, p: (b, 0, p))
    return pl.pallas_call(
        _fox_kernel,
        grid=(B, n_pairs),
        in_specs=[blk, blk, blk,
                  pl.BlockSpec((1, 1, 2, S), lambda b, p: (b, p, 0, 0)),
                  pl.BlockSpec((2, BLOCK, FOX_TK), lambda b, p: (0, 0, 0))],
        out_specs=blk,
        out_shape=jax.ShapeDtypeStruct((B, S, D_FOX), MXU_DTYPE),
        compiler_params=_cparams(2),
        name="fox_attention",
    )(qf, kf, vf, c4, jnp.asarray(_causal_bias()))


def _band_bias():
    qi = np.arange(BLOCK)[:, None]
    ki = np.arange(2 * BLOCK)[None, :]
    delta = BLOCK + qi - ki
    band = (delta >= 0) & (delta <= BLOCK)
    first = band & (ki >= BLOCK)
    return np.where(np.stack([first, band]), 0.0, MASK_BIAS).astype(np.float32)


def _dil_kernel(q_ref, k_ref, v_ref, bias_ref, o_ref, q32, k32, v32, q17, k17, v17, onorm_s, mu_s):
    S = q_ref.shape[1]
    q32[...] = q_ref[0].astype(F32)
    zeros = jnp.zeros((SKEW_PAD, LANES), F32)
    for plain, skewed, src in ((k32, k17, k_ref), (v32, v17, v_ref)):
        plain[0:DIL_PAD, :] = zeros[0:DIL_PAD]
        skewed[0:SKEW_PAD, :] = zeros
        plain[DIL_PAD:, :] = src[0].astype(F32)

    def skew_rows(g8, _):
        for u in range(8):
            src = pl.multiple_of(g8 * (8 * SKEW_DIL), 8 * SKEW_DIL) + u * SKEW_DIL
            dst = pl.multiple_of(g8 * (8 * SKEW_PITCH), 8) + u * SKEW_PITCH
            q17[pl.ds(dst, SKEW_DIL), :] = q32[pl.ds(src, SKEW_DIL), :]
            k17[pl.ds(SKEW_PAD + dst, SKEW_DIL), :] = k32[pl.ds(DIL_PAD + src, SKEW_DIL), :]
            v17[pl.ds(SKEW_PAD + dst, SKEW_DIL), :] = v32[pl.ds(DIL_PAD + src, SKEW_DIL), :]
        return 0

    lax.fori_loop(0, S // (8 * SKEW_DIL), skew_rows, 0)

    lane = lax.broadcasted_iota(jnp.int32, (BLOCK, LANES), 1)
    is_a = lane < HEAD_DIM
    ones = jnp.ones((2 * BLOCK, LANES), MXU_DTYPE)

    def rows(ref, start, count, stride):
        if stride == 1:
            return ref[pl.ds(start, count), :]
        return ref[pl.ds(start, count, stride=stride), :]

    def unit(dil, rho, n):
        if dil == SKEW_DIL:
            (qs, ks, vs), stride, pad = (q17, k17, v17), SKEW_PITCH, SKEW_PAD
        else:
            (qs, ks, vs), stride, pad = (q32, k32, v32), dil, DIL_PAD
        q0 = rho + stride * BLOCK * n
        qb = rows(qs, q0, BLOCK, stride).astype(MXU_DTYPE)
        kw = rows(ks, q0 + pad - stride * BLOCK, 2 * BLOCK, stride).astype(MXU_DTYPE)
        vw = rows(vs, q0 + pad - stride * BLOCK, 2 * BLOCK, stride).astype(MXU_DTYPE)
        bias = bias_ref[jnp.minimum(n, 1)]
        s = _dot_nt(_pair_rows(qb, is_a), kw)
        s = jnp.concatenate([s[:BLOCK] + bias, s[BLOCK:] + bias], axis=0)
        m = jnp.max(s, axis=1, keepdims=True)
        p = jnp.exp2(s - m).astype(MXU_DTYPE)
        o = _dot(p, jnp.concatenate([vw, ones], axis=1))
        acc = jnp.where(is_a, o[:BLOCK, :LANES], o[BLOCK:, :LANES])
        l = jnp.where(is_a, o[:BLOCK, LANES:], o[BLOCK:, LANES:])
        mu = jnp.where(is_a, m[:BLOCK], m[BLOCK:]) + jnp.log2(l)
        return acc / l, mu

    n_units = S // BLOCK

    for slab, dil in enumerate(d for d in DILATIONS if d != 1):
        blk_bits = (n_units // dil).bit_length() - 1

        def several_units(g, _, slab=slab, dil=dil, blk_bits=blk_bits):
            for u in range(DIL_UNITS_PER_BODY):
                t = g * DIL_UNITS_PER_BODY + u
                rho = lax.shift_right_logical(t, blk_bits)
                n = t & ((1 << blk_bits) - 1)
                o_n, mu = unit(dil, rho, n)
                q0 = rho + dil * BLOCK * n
                onorm_s[slab, pl.ds(q0, BLOCK, stride=dil), :] = o_n
                mu_s[slab, pl.ds(q0, BLOCK, stride=dil), :] = mu
            return 0

        lax.fori_loop(0, n_units // DIL_UNITS_PER_BODY, several_units, 0)

    def merge_blocks(g, _):
        for u in range(DIL_UNITS_PER_BODY):
            n = g * DIL_UNITS_PER_BODY + u
            o_1, mu_1 = unit(1, 0, n)
            r0 = pl.multiple_of(n * BLOCK, BLOCK)
            others = [(onorm_s[sl, pl.ds(r0, BLOCK), :], mu_s[sl, pl.ds(r0, BLOCK), :])
                      for sl in range(len(DILATIONS) - 1)]
            top = functools.reduce(jnp.maximum, [mu_1] + [mu for _, mu in others])
            w = jnp.exp2(mu_1 - top)
            num, den = w * o_1, w
            for o_p, mu_p in others:
                w = jnp.exp2(mu_p - top)
                num = num + w * o_p
                den = den + w
            o_ref[0, pl.ds(r0, BLOCK), :] = (num / den).astype(o_ref.dtype)
        return 0

    lax.fori_loop(0, n_units // DIL_UNITS_PER_BODY, merge_blocks, 0)


def _dilated_attention(qd, kd, vd):
    B, S, _ = qd.shape
    blk = pl.BlockSpec((1, S, LANES), lambda b, p: (b, 0, p))
    n_slabs = len(DILATIONS) - 1
    skewed = S // SKEW_DIL * SKEW_PITCH
    return pl.pallas_call(
        _dil_kernel,
        grid=(B, N_PAIRS_DIL),
        in_specs=[blk, blk, blk, pl.BlockSpec((2, BLOCK, 2 * BLOCK), lambda b, p: (0, 0, 0))],
        out_specs=blk,
        out_shape=jax.ShapeDtypeStruct((B, S, D_DIL), MXU_DTYPE),
        scratch_shapes=[
            pltpu.VMEM((S, LANES), F32),
            pltpu.VMEM((DIL_PAD + S, LANES), F32),
            pltpu.VMEM((DIL_PAD + S, LANES), F32),
            pltpu.VMEM((skewed, LANES), F32),
            pltpu.VMEM((SKEW_PAD + skewed, LANES), F32),
            pltpu.VMEM((SKEW_PAD + skewed, LANES), F32),
            pltpu.VMEM((n_slabs, S, LANES), F32),
            pltpu.VMEM((n_slabs, S, LANES), F32),
        ],
        compiler_params=_cparams(2),
        name="dilated_attention",
    )(qd, kd, vd, jnp.asarray(_band_bias()))


def _layer_norm(u, g, b):
    mu = jnp.mean(u, axis=1, keepdims=True)
    d = u - mu
    var = jnp.mean(d * d, axis=1, keepdims=True)
    return d * lax.rsqrt(var + LN_EPS) * g + b


def _rms_norm(x, g):
    ms = jnp.mean(x * x, axis=1, keepdims=True)
    return x * lax.rsqrt(ms + RMS_EPS) * g


def _top2_of4(vals):
    v1, i1 = vals[0], jnp.zeros(vals[0].shape, jnp.int32)
    for i in range(1, 4):
        better = vals[i] > v1
        v1 = jnp.where(better, vals[i], v1)
        i1 = jnp.where(better, i, i1)
    v2 = jnp.full(vals[0].shape, -1.0, F32)
    i2 = jnp.zeros(vals[0].shape, jnp.int32)
    for i in range(4):
        better = (vals[i] > v2) & (i1 != i)
        v2 = jnp.where(better, vals[i], v2)
        i2 = jnp.where(better, i, i2)
    return v1, i1, v2, i2


def _outproj_kernel(od_ref, of_ref, h_ref, wo_ref, gd_ref, gf_ref, lng_ref, lnb_ref,
                    wr_ref, br_ref, tri_ref,
                    h1_ref, h1t_ref, e_ref, gate_ref, rank_ref, cnt_ref, base_ref):
    step = pl.program_id(0)

    @pl.when(step == 0)
    def _():
        base_ref[...] = jnp.zeros_like(base_ref)

    xd = _rms_norm(od_ref[...].astype(F32), gd_ref[...])
    xf = _rms_norm(of_ref[...].astype(F32), gf_ref[...])
    y = _dot(xd.astype(MXU_DTYPE), wo_ref[0:D_DIL, :]) + _dot(xf.astype(MXU_DTYPE), wo_ref[D_DIL:, :])
    h1 = _layer_norm(DEEPNORM_ALPHA * h_ref[...] + y, lng_ref[...], lnb_ref[...])
    h1_ref[...] = h1
    _store_token_tiles(h1t_ref, h1)
    tm = h1.shape[0]

    h_hi, h_mid, _ = _split3(h1)
    two = _dot_nt(wr_ref[...], h_hi)
    logits = (two[:N_EXPERTS] + two[N_EXPERTS:]) + _dot_nt(wr_ref[0:N_EXPERTS, :], h_mid) + br_ref[...]
    logits = logits - jnp.max(logits, axis=0, keepdims=True)
    ex = jnp.exp(logits)
    probs = ex / jnp.sum(ex, axis=0, keepdims=True)
    pr = [probs[j:j + 1, :] for j in range(N_EXPERTS)]

    def group_score(g):
        v = pr[4 * g:4 * g + 4]
        pairs = [v[a] + v[b] for a in range(4) for b in range(a + 1, 4)]
        return functools.reduce(jnp.maximum, pairs)

    best = group_score(0)
    gsel = jnp.zeros((1, tm), jnp.int32)
    for g in range(1, N_GROUPS):
        sc = group_score(g)
        better = sc > best
        best = jnp.where(better, sc, best)
        gsel = jnp.where(better, g, gsel)
    in_grp = []
    for i in range(EXPERTS_PER_GROUP):
        v = pr[i]
        for g in range(1, N_GROUPS):
            v = jnp.where(gsel == g, pr[4 * g + i], v)
        in_grp.append(v)
    v1, i1, v2, i2 = _top2_of4(in_grp)
    e1 = gsel * EXPERTS_PER_GROUP + i1
    e2 = gsel * EXPERTS_PER_GROUP + i2
    den = v1 + v2
    e_ref[...] = jnp.concatenate([e1, e2], axis=0)
    gate_ref[...] = jnp.concatenate([v1 / den, v2 / den], axis=0)

    eidx = lax.broadcasted_iota(jnp.int32, (N_EXPERTS, tm), 0)
    oh1 = (eidx == e1).astype(F32)
    oh2 = (eidx == e2).astype(F32)
    tot = oh1 + oh2
    before = base_ref[...] + _dot(tot.astype(jnp.bfloat16), tri_ref[...])
    r1 = jnp.sum(oh1 * before, axis=0, keepdims=True)
    r2 = jnp.sum(oh2 * before, axis=0, keepdims=True)
    rank_ref[...] = jnp.concatenate([r1, r2], axis=0).astype(jnp.int32)
    base_ref[...] = base_ref[...] + jnp.sum(tot, axis=1, keepdims=True)
    cnt_ref[...] = jnp.broadcast_to(base_ref[...], cnt_ref.shape)


def _out_projection(od, of, h, w_out, g_dil, g_fox, ln_g, ln_b, wr3, b_router, tm=512):
    T = h.shape[0]
    tri = (np.arange(tm)[:, None] < np.arange(tm)[None, :]).astype(np.float32)
    rows = lambda width: pl.BlockSpec((tm, width), lambda i: (i, 0))
    full = lambda a: pl.BlockSpec(a.shape, lambda i: (0,) * a.ndim)
    tok = pl.BlockSpec((TOP_K, tm), lambda i: (0, i))
    consts = [w_out, g_dil.reshape(1, -1), g_fox.reshape(1, -1), ln_g.reshape(1, -1), ln_b.reshape(1, -1),
              wr3, b_router.reshape(-1, 1).astype(F32), jnp.asarray(tri, jnp.bfloat16)]
    return pl.pallas_call(
        _outproj_kernel,
        grid=(T // tm,),
        in_specs=[rows(D_DIL), rows(D_FOX), rows(D_MODEL)] + [full(a) for a in consts],
        out_specs=(rows(D_MODEL), pl.BlockSpec((tm * TOKEN_TILE_ROWS, LANES), lambda i: (i, 0)),
                   tok, tok, tok, pl.BlockSpec((N_EXPERTS, LANES), lambda i: (0, 0))),
        out_shape=(
            jax.ShapeDtypeStruct((T, D_MODEL), F32),
            jax.ShapeDtypeStruct((T * TOKEN_TILE_ROWS, LANES), F32),
            jax.ShapeDtypeStruct((TOP_K, T), jnp.int32),
            jax.ShapeDtypeStruct((TOP_K, T), F32),
            jax.ShapeDtypeStruct((TOP_K, T), jnp.int32),
            jax.ShapeDtypeStruct((N_EXPERTS, LANES), F32),
        ),
        scratch_shapes=[pltpu.VMEM((N_EXPERTS, 1), F32)],
        compiler_params=_cparams(1),
        name="out_projection_router",
    )(od, of, h, *consts)


def _store_token_tiles(ref, x):
    n = x.shape[0]
    for c in range(TOKEN_TILE_ROWS):
        ref[pl.ds(c, n, stride=TOKEN_TILE_ROWS), :] = x[:, c * LANES:(c + 1) * LANES]


def _load_token_tiles(ref, n):
    return jnp.concatenate([ref[pl.ds(c, n, stride=TOKEN_TILE_ROWS), :] for c in range(TOKEN_TILE_ROWS)], axis=1)


def _tile_copy(src, src_row, dst, dst_row, sem):
    return pltpu.make_async_copy(src.at[pl.ds(src_row, TOKEN_TILE_ROWS)],
                                 dst.at[pl.ds(dst_row, TOKEN_TILE_ROWS)], sem)


def _dispatch_kernel(p0_ref, p1_ref, pad0_ref, padn_ref, h_ref, xg_out, sem, ztile, *, td):
    def copies(t):
        src = pl.multiple_of(t * TOKEN_TILE_ROWS, TOKEN_TILE_ROWS)
        return (_tile_copy(h_ref, src, xg_out, pl.multiple_of(p0_ref[t], TOKEN_TILE_ROWS), sem),
                _tile_copy(h_ref, src, xg_out, pl.multiple_of(p1_ref[t], TOKEN_TILE_ROWS), sem))

    def issue(t, _):
        for thread, cp in enumerate(copies(t)):
            cp.start(priority=thread)
        return 0

    def drain(t, _):
        for cp in copies(t):
            cp.wait()
        return 0

    lax.fori_loop(0, td, issue, 0, unroll=DMA_ISSUE_UNROLL)
    lax.fori_loop(0, td, drain, 0, unroll=DMA_ISSUE_UNROLL)

    @pl.when(pl.program_id(0) == pl.num_programs(0) - 1)
    def _():
        ztile[...] = jnp.zeros(ztile.shape, ztile.dtype)

        chunk_rows = ztile.shape[0]
        chunk_tokens = chunk_rows // TOKEN_TILE_ROWS

        def for_each_zero_copy(action):
            def hole(e, _):
                n_chunks = padn_ref[e] // chunk_tokens

                def chunk(c, _):
                    dst = pl.multiple_of(pad0_ref[e] + c * chunk_rows, TOKEN_TILE_ROWS)
                    action(pltpu.make_async_copy(ztile, xg_out.at[pl.ds(dst, chunk_rows)], sem))
                    return 0

                def single(r, _):
                    dst = pl.multiple_of(pad0_ref[e] + r * TOKEN_TILE_ROWS, TOKEN_TILE_ROWS)
                    action(_tile_copy(ztile, 0, xg_out, dst, sem))
                    return 0

                lax.fori_loop(0, n_chunks, chunk, 0)
                lax.fori_loop(n_chunks * chunk_tokens, padn_ref[e], single, 0)
                return 0
            lax.fori_loop(0, pad0_ref.shape[0], hole, 0)

        for_each_zero_copy(lambda cp: cp.start())
        for_each_zero_copy(lambda cp: cp.wait())


def _dispatch(h1t, pos_rows, pad_rows, pad_counts, n_rows, td=256):
    T = h1t.shape[0] // TOKEN_TILE_ROWS
    smem = pl.BlockSpec((td,), lambda i: (i,), memory_space=pltpu.SMEM)
    smem_all = pl.BlockSpec(memory_space=pltpu.SMEM)
    return pl.pallas_call(
        functools.partial(_dispatch_kernel, td=td),
        grid=(T // td,),
        in_specs=[smem, smem, smem_all, smem_all, pl.BlockSpec((td * TOKEN_TILE_ROWS, LANES), lambda i: (i, 0))],
        out_specs=pl.BlockSpec(memory_space=pl.ANY),
        out_shape=jax.ShapeDtypeStruct((n_rows * TOKEN_TILE_ROWS, LANES), F32),
        scratch_shapes=[pltpu.SemaphoreType.DMA(()), pltpu.VMEM((ZERO_CHUNK_TOKENS * TOKEN_TILE_ROWS, LANES), F32)],
        compiler_params=_cparams(1),
        name="moe_dispatch",
    )(pos_rows[0], pos_rows[1], pad_rows, pad_counts, h1t)


def _ffn_kernel(be_ref, nv_ref, slot_ref, nxt_ref, x_ref, wg_hbm, wu_hbm, wd_hbm, y_ref, wstage, wb, sem,
                *, tmf, layer):
    j = pl.program_id(0)
    valid = j < nv_ref[0]

    def fetch(expert, slot):
        return [pltpu.make_async_copy(w.at[layer, expert], wstage.at[slot, k], sem.at[slot])
                for k, w in enumerate((wg_hbm, wu_hbm, wd_hbm))]

    @pl.when(j == 0)
    def _():
        for cp in fetch(be_ref[0], 0):
            cp.start()

    @pl.when(valid & ((j == 0) | (be_ref[j] != be_ref[jnp.maximum(j - 1, 0)])))
    def _():
        slot = slot_ref[j]
        for cp in fetch(be_ref[j], slot):
            cp.wait()
        for k in range(3):
            wb[k] = wstage[slot, k].astype(MXU_DTYPE)

        @pl.when(nxt_ref[j] >= 0)
        def _():
            for cp in fetch(nxt_ref[j], 1 - slot):
                cp.start()

    @pl.when(valid)
    def _():
        xb = _load_token_tiles(x_ref, tmf).astype(MXU_DTYPE)
        a = _dot(xb, wb[0])
        u = _dot(xb, wb[1])
        hmid = (a * jax.nn.sigmoid(a)) * u
        _store_token_tiles(y_ref, _dot(hmid.astype(MXU_DTYPE), wb[2]))

    @pl.when(j >= nv_ref[0])
    def _():
        y_ref[...] = jnp.zeros_like(y_ref)


def _expert_ffn(xg, blk_e, n_valid, present, wg, wu, wd, layer, tmf):
    n_rows = xg.shape[0] // TOKEN_TILE_ROWS
    D = D_MODEL
    nb = n_rows // tmf
    eids = jnp.arange(N_EXPERTS, dtype=jnp.int32)
    order = jnp.cumsum(present.astype(jnp.int32)) - 1
    later = (eids[None, :] > eids[:, None]) & present[None, :]
    nxt_e = jnp.min(jnp.where(later, eids[None, :], N_EXPERTS), axis=1)
    nxt_e = jnp.where(nxt_e == N_EXPERTS, -1, nxt_e)
    slot = (order & 1)[blk_e].astype(jnp.int32)
    nxt = nxt_e[blk_e].astype(jnp.int32)
    xmap = lambda j, be, nv, sl, nx: (jnp.minimum(j, nv[0] - 1), 0)
    anyspec = pl.BlockSpec(memory_space=pl.ANY)
    grid_spec = pltpu.PrefetchScalarGridSpec(
        num_scalar_prefetch=4,
        grid=(nb,),
        in_specs=[pl.BlockSpec((tmf * TOKEN_TILE_ROWS, LANES), xmap), anyspec, anyspec, anyspec],
        out_specs=pl.BlockSpec((tmf * TOKEN_TILE_ROWS, LANES), lambda j, be, nv, sl, nx: (j, 0)),
        scratch_shapes=[pltpu.VMEM((2, 3, D, D), F32), pltpu.VMEM((3, D, D), MXU_DTYPE),
                        pltpu.SemaphoreType.DMA((2,))],
    )
    return pl.pallas_call(
        functools.partial(_ffn_kernel, tmf=tmf, layer=layer),
        grid_spec=grid_spec,
        out_shape=jax.ShapeDtypeStruct(xg.shape, F32),
        compiler_params=_cparams(1),
        name="moe_expert_ffn",
    )(blk_e, n_valid, slot, nxt, xg, wg, wu, wd)


def _combine_kernel(p0_ref, p1_ref, gate_ref, h_ref, lng_ref, lnb_ref, y_hbm, o_ref, buf0, buf1, sem, *, tc):
    def copies(t):
        dst = pl.multiple_of(t * TOKEN_TILE_ROWS, TOKEN_TILE_ROWS)
        return (_tile_copy(y_hbm, pl.multiple_of(p0_ref[t], TOKEN_TILE_ROWS), buf0, dst, sem),
                _tile_copy(y_hbm, pl.multiple_of(p1_ref[t], TOKEN_TILE_ROWS), buf1, dst, sem))

    def issue(t, _):
        for thread, cp in enumerate(copies(t)):
            cp.start(priority=thread)
        return 0

    def drain(t, _):
        for cp in copies(t):
            cp.wait()
        return 0

    lax.fori_loop(0, tc, issue, 0, unroll=DMA_ISSUE_UNROLL)
    lax.fori_loop(0, tc, drain, 0, unroll=DMA_ISSUE_UNROLL)
    g = gate_ref[...]
    y = _load_token_tiles(buf0, tc) * g[:, 0:1] + _load_token_tiles(buf1, tc) * g[:, 1:2]
    o_ref[...] = _layer_norm(DEEPNORM_ALPHA * h_ref[...] + y, lng_ref[...], lnb_ref[...])


def _combine(yg, pos_rows, gates_col, h1, ln_g, ln_b, tc=256):
    T, D = h1.shape
    smem = pl.BlockSpec((tc,), lambda i: (i,), memory_space=pltpu.SMEM)
    rows = lambda width: pl.BlockSpec((tc, width), lambda i: (i, 0))
    vec = pl.BlockSpec((1, D), lambda i: (0, 0))
    buf = pltpu.VMEM((tc * TOKEN_TILE_ROWS, LANES), F32)
    return pl.pallas_call(
        functools.partial(_combine_kernel, tc=tc),
        grid=(T // tc,),
        in_specs=[smem, smem, rows(TOP_K), rows(D), vec, vec, pl.BlockSpec(memory_space=pl.ANY)],
        out_specs=rows(D),
        out_shape=jax.ShapeDtypeStruct((T, D), F32),
        scratch_shapes=[buf, buf, pltpu.SemaphoreType.DMA(())],
        compiler_params=_cparams(1),
        name="moe_combine_ln",
    )(pos_rows[0], pos_rows[1], gates_col, h1, ln_g.reshape(1, -1), ln_b.reshape(1, -1), yg)


def _grouped_moe(h1, h1t, experts, gates, ranks, counts, wg, wu, wd, layer, ln_g, ln_b, tmf=512):
    T = h1.shape[0]
    nb = (T * TOP_K) // tmf + N_EXPERTS
    cnt = counts[:, 0].astype(jnp.int32)
    pcnt = (cnt + tmf - 1) // tmf * tmf
    pends = jnp.cumsum(pcnt)
    pstart = pends - pcnt
    eids = jnp.arange(N_EXPERTS, dtype=jnp.int32)
    seg = jnp.sum(jnp.where(experts[None] == eids[:, None, None], pstart[:, None, None], 0), axis=0)
    pos_rows = (seg + ranks) * TOKEN_TILE_ROWS
    blk_start = jnp.arange(nb, dtype=jnp.int32) * tmf
    blk_e = jnp.minimum(jnp.sum((pends[None, :] <= blk_start[:, None]).astype(jnp.int32), axis=1), N_EXPERTS - 1)
    n_valid = (pends[-1:] // tmf).astype(jnp.int32)
    hole_start = jnp.concatenate([pstart + cnt, pends[-1:]])
    hole_rows = jnp.concatenate([pcnt - cnt, nb * tmf - pends[-1:]])
    xg = _dispatch(h1t, pos_rows, hole_start * TOKEN_TILE_ROWS, hole_rows, nb * tmf)
    yg = _expert_ffn(xg, blk_e, n_valid, cnt > 0, wg, wu, wd, layer, tmf)
    return _combine(yg, pos_rows, gates.T, h1, ln_g, ln_b)


def _rope_tables(S):
    half = HEAD_DIM // 2
    inv = ROPE_THETA ** (-jnp.arange(half, dtype=F32) / half)
    ang = jnp.arange(S, dtype=F32)[:, None] * inv[None, :]
    cos, sin = jnp.cos(ang), jnp.sin(ang)
    reps = LANES // HEAD_DIM
    cos_tab = jnp.tile(jnp.concatenate([cos, cos], axis=1), (1, reps))
    sin_tab = jnp.tile(jnp.concatenate([-sin, sin], axis=1), (1, reps))
    return cos_tab, sin_tab


def _pad_w_in(w):
    pad = jnp.zeros((D_MODEL, D_PROJ_PAD - w.shape[1]), w.dtype)
    return jnp.concatenate([w, pad], axis=1).astype(MXU_DTYPE)


def kernel(x, w_in, b_forget, g_dil, g_fox, w_out, ln1_g, ln1_b, w_router, b_router,
           w_gate, w_up, w_down, ln2_g, ln2_b):
    B, S, D = x.shape
    T = B * S
    cos_tab, sin_tab = _rope_tables(S)
    wr_hi, wr_mid, _ = _split3(w_router.astype(F32).T)
    wr3 = jnp.concatenate([wr_hi, wr_mid], axis=0)
    h = x
    for l in range(DEPTH):
        qd, kd, vd, qf, kf, vf, zf = _in_projection(h, _pad_w_in(w_in[l]), cos_tab, sin_tab)
        c = _forget_cumsum(zf, b_forget[l])[:, :N_HEADS_FOX]
        o_fox = _fox_attention(qf, kf, vf, c)
        o_dil = _dilated_attention(qd, kd, vd)
        h1, h1t, experts, gates, ranks, counts = _out_projection(
            o_dil.reshape(T, D_DIL), o_fox.reshape(T, D_FOX), h.reshape(T, D),
            w_out[l].astype(MXU_DTYPE), g_dil[l], g_fox[l], ln1_g[l], ln1_b[l], wr3, b_router)
        h2 = _grouped_moe(h1, h1t, experts, gates, ranks, counts,
                          w_gate, w_up, w_down, l,
                          ln2_g[l], ln2_b[l])
        h = h2.reshape(B, S, D)
    return h
```

```python
import functools

import numpy as np
import jax
import jax.numpy as jnp
from jax import lax
from jax.experimental import pallas as pl
from jax.experimental.pallas import tpu as pltpu

D_MODEL = 1024
DEPTH = 2
HEAD_DIM = 64
N_HEADS_DIL = 12
N_HEADS_FOX = 4
D_DIL = N_HEADS_DIL * HEAD_DIM
D_FOX = N_HEADS_FOX * HEAD_DIM
DILATIONS = (1, 4, 16)
BLOCK = 128
ROPE_THETA = 10000.0
N_EXPERTS = 16
N_GROUPS = 4
EXPERTS_PER_GROUP = 4
TOP_K = 2
DEEPNORM_ALPHA = (2.0 * DEPTH) ** 0.25
LN_EPS = 1e-5
RMS_EPS = 1e-6

LANES = 128
N_PAIRS_DIL = D_DIL // LANES
D_PROJ_PAD = 3 * D_DIL + 3 * D_FOX + LANES
VMEM_LIMIT = 48 * 1024 * 1024
TOKEN_TILE_ROWS = D_MODEL // LANES
DMA_ISSUE_UNROLL = 8
COMBINE_PARTS = 4
ZERO_CHUNK_TOKENS = 32
SKEW_DIL = 16
SKEW_PITCH = SKEW_DIL + 1
DIL_PAD = max(d for d in DILATIONS if d != SKEW_DIL) * BLOCK
SKEW_PAD = SKEW_PITCH * BLOCK
MASK_BIAS = -1e30
LOG2_E = 1.4426950408889634
DIL_UNITS_PER_BODY = 16
FOX_TK = 256
FOX_BLOCKS_PER_BODY = 4
ZF_ROWS = 8

MXU_DTYPE = jnp.bfloat16
F32 = jnp.float32
NEG_INF = float("-inf")


def _cparams(n_axes):
    return pltpu.CompilerParams(dimension_semantics=("arbitrary",) * n_axes,
                                vmem_limit_bytes=VMEM_LIMIT)


def _dot(a, b):
    return jnp.dot(a, b, preferred_element_type=F32)


def _dot_nt(a, b):
    return lax.dot_general(a, b, (((1,), (1,)), ((), ())), preferred_element_type=F32)


def _inproj_kernel(x_ref, w_ref, cos_ref, sin_ref,
                   qd_ref, kd_ref, vd_ref, qf_ref, kf_ref, vf_ref, zf_ref):
    xb = x_ref[0].astype(MXU_DTYPE)
    tm = xb.shape[0]
    cos = cos_ref[...]
    sin = sin_ref[...]
    lane = lax.broadcasted_iota(jnp.int32, (tm, LANES), 1)
    first_half = (lane % HEAD_DIM) < (HEAD_DIM // 2)

    def rope(z):
        rot = jnp.where(first_half, pltpu.roll(z, LANES - 32, 1), pltpu.roll(z, 32, 1))
        return z * cos + rot * sin

    def proj(col, width):
        return _dot(xb, w_ref[:, col:col + width])

    scale = HEAD_DIM ** -0.5
    col = 0
    for out_ref, roped, mul in ((qd_ref, True, scale * LOG2_E), (kd_ref, True, 1.0)):
        for c in range(D_DIL // 256):
            z = proj(col, 256)
            for half in range(2):
                zz = rope(z[:, half * LANES:(half + 1) * LANES]) * mul
                out_ref[0, :, c * 256 + half * LANES:c * 256 + (half + 1) * LANES] = zz.astype(out_ref.dtype)
            col += 256
    for c in range(D_DIL // 256):
        vd_ref[0, :, c * 256:(c + 1) * 256] = proj(col, 256).astype(vd_ref.dtype)
        col += 256
    qf_ref[0] = (proj(col, D_FOX) * (scale * LOG2_E)).astype(qf_ref.dtype)
    col += D_FOX
    kf_ref[0] = proj(col, D_FOX).astype(kf_ref.dtype)
    col += D_FOX
    vf_ref[0] = proj(col, D_FOX).astype(vf_ref.dtype)
    col += D_FOX
    zf_ref[0] = proj(col, LANES).T[0:ZF_ROWS, :]


def _in_projection(h, w_pad, cos_tab, sin_tab, tm=1024):
    B, S, D = h.shape
    grid = (B, S // tm)
    row = lambda width: pl.BlockSpec((1, tm, width), lambda b, i: (b, i, 0))
    out_shape = (
        jax.ShapeDtypeStruct((B, S, D_DIL), MXU_DTYPE),
        jax.ShapeDtypeStruct((B, S, D_DIL), MXU_DTYPE),
        jax.ShapeDtypeStruct((B, S, D_DIL), MXU_DTYPE),
        jax.ShapeDtypeStruct((B, S, D_FOX), MXU_DTYPE),
        jax.ShapeDtypeStruct((B, S, D_FOX), MXU_DTYPE),
        jax.ShapeDtypeStruct((B, S, D_FOX), MXU_DTYPE),
        jax.ShapeDtypeStruct((B, ZF_ROWS, S), F32),
    )
    return pl.pallas_call(
        _inproj_kernel,
        grid=grid,
        in_specs=[
            row(D),
            pl.BlockSpec((D, D_PROJ_PAD), lambda b, i: (0, 0)),
            pl.BlockSpec((tm, LANES), lambda b, i: (i, 0)),
            pl.BlockSpec((tm, LANES), lambda b, i: (i, 0)),
        ],
        out_specs=(row(D_DIL), row(D_DIL), row(D_DIL), row(D_FOX), row(D_FOX), row(D_FOX),
                   pl.BlockSpec((1, ZF_ROWS, tm), lambda b, i: (b, 0, i))),
        out_shape=out_shape,
        compiler_params=_cparams(2),
        name="in_projection",
    )(h, w_pad, cos_tab, sin_tab)


def _split3(x):
    hi = x.astype(jnp.bfloat16)
    r1 = x - hi.astype(F32)
    mid = r1.astype(jnp.bfloat16)
    lo = (r1 - mid.astype(F32)).astype(jnp.bfloat16)
    return hi, mid, lo


def _forget_cumsum_kernel(z_ref, b_ref, tri_ref, c_ref):
    x = z_ref[0] + b_ref[...]
    logf = jnp.minimum(x, 0.0) - jnp.log1p(jnp.exp(-jnp.abs(x)))
    tri = tri_ref[...]
    S = x.shape[1]
    carry = jnp.zeros((x.shape[0], 1), F32)
    for blk in range(S // LANES):
        seg = logf[:, blk * LANES:(blk + 1) * LANES]
        hi, mid, lo = _split3(seg)
        cs = (_dot(hi, tri) + _dot(mid, tri)) + _dot(lo, tri) + carry
        c_ref[0, :, blk * LANES:(blk + 1) * LANES] = cs * LOG2_E
        carry = cs[:, LANES - 1:LANES]


def _forget_cumsum(zt, b_forget):
    B, Hf, S = zt.shape
    tri = (np.arange(LANES)[:, None] <= np.arange(LANES)[None, :]).astype(np.float32)
    return pl.pallas_call(
        _forget_cumsum_kernel,
        grid=(B,),
        in_specs=[
            pl.BlockSpec((1, Hf, S), lambda b: (b, 0, 0)),
            pl.BlockSpec((Hf, 1), lambda b: (0, 0)),
            pl.BlockSpec((LANES, LANES), lambda b: (0, 0)),
        ],
        out_specs=pl.BlockSpec((1, Hf, S), lambda b: (b, 0, 0)),
        out_shape=jax.ShapeDtypeStruct((B, Hf, S), F32),
        compiler_params=_cparams(1),
        name="forget_cumsum",
    )(zt, jnp.pad(b_forget.astype(F32), (0, Hf - b_forget.shape[0])).reshape(Hf, 1), jnp.asarray(tri, jnp.bfloat16))


def _pair_rows(xp, is_a):
    zero = jnp.zeros_like(xp)
    return jnp.concatenate([jnp.where(is_a, xp, zero), jnp.where(is_a, zero, xp)], axis=0)


def _causal_bias():
    qi = np.arange(BLOCK)[:, None]
    ki = np.arange(FOX_TK)[None, :]
    vis = np.stack([ki <= qi, ki <= qi + BLOCK])
    return np.where(vis, 0.0, MASK_BIAS).astype(np.float32)


def _fox_kernel(q_ref, k_ref, v_ref, crow_ref, cbias_ref, o_ref):
    S = q_ref.shape[1]
    tk = FOX_TK
    lane = lax.broadcasted_iota(jnp.int32, (BLOCK, LANES), 1)
    is_a = lane < HEAD_DIM
    ones = jnp.ones((tk, LANES), MXU_DTYPE)

    def unit(q2, kv, bias):
        kb, v_ones, cr = kv
        s = _dot_nt(q2, kb)
        sa = s[:BLOCK] - cr[0:1]
        sb = s[BLOCK:] - cr[1:2]
        if bias is not None:
            sa = sa + bias
            sb = sb + bias
        s = jnp.concatenate([sa, sb], axis=0)
        m = jnp.max(s, axis=1, keepdims=True)
        p = jnp.exp2(s - m).astype(MXU_DTYPE)
        o = _dot(p, v_ones)
        acc = jnp.where(is_a, o[:BLOCK, :LANES], o[BLOCK:, :LANES])
        l = jnp.where(is_a, o[:BLOCK, LANES:], o[BLOCK:, LANES:])
        return acc, l, jnp.where(is_a, m[:BLOCK], m[BLOCK:])

    def key_block(j):
        k0 = pl.multiple_of(j * tk, tk)
        return (k_ref[0, pl.ds(k0, tk), :],
                jnp.concatenate([v_ref[0, pl.ds(k0, tk), :], ones], axis=1),
                crow_ref[0, 0, :, pl.ds(k0, tk)])

    def fold(state, units):
        num, den, top_old = state
        top = functools.reduce(jnp.maximum, [top_old] + [m for _, _, m in units])
        scale = jnp.exp2(top_old - top)
        num, den = scale * num, scale * den
        for acc, l, m in units:
            w = jnp.exp2(m - top)
            num = num + w * acc
            den = den + w * l
        return num, den, top

    def q_block(i, _):
        r0 = pl.multiple_of(i * tk, tk)
        q2 = [_pair_rows(q_ref[0, pl.ds(r0 + h * BLOCK, BLOCK), :], is_a) for h in range(2)]
        zero = jnp.zeros((BLOCK, LANES), F32)
        empty = (zero, zero, jnp.full((BLOCK, LANES), MASK_BIAS, F32))

        def fold_blocks(states, blocks):
            units = [[unit(q2[h], kv, None if bias is None else bias[h]) for h in range(2)]
                     for kv, bias in ((key_block(j), bias) for j, bias in blocks)]
            return tuple(fold(states[h], [u[h] for u in units]) for h in range(2))

        def full_blocks(g, states):
            return fold_blocks(states, [(FOX_BLOCKS_PER_BODY * g + u, None) for u in range(FOX_BLOCKS_PER_BODY)])

        states = lax.fori_loop(0, i // FOX_BLOCKS_PER_BODY, full_blocks, (empty, empty))
        diag = (i, (cbias_ref[0], cbias_ref[1]))

        def tail(n_left):
            return lambda st: fold_blocks(st, [(i - n_left + u, None) for u in range(n_left)] + [diag])

        states = lax.switch(i % FOX_BLOCKS_PER_BODY, [tail(r) for r in range(FOX_BLOCKS_PER_BODY)], states)
        for h in range(2):
            num, den, _ = states[h]
            o_ref[0, pl.ds(r0 + h * BLOCK, BLOCK), :] = (num / den).astype(o_ref.dtype)
        return 0

    lax.fori_loop(0, S // tk, q_block, 0)


def _fox_attention(qf, kf, vf, c2):
    B, S, _ = qf.shape
    n_pairs = D_FOX // LANES
    c4 = c2.reshape(B, n_pairs, 2, S)
    blk = pl.BlockSpec((1, S, LANES), lambda b, p: (b, 0, p))
    return pl.pallas_call(
        _fox_kernel,
        grid=(B, n_pairs),
        in_specs=[blk, blk, blk,
                  pl.BlockSpec((1, 1, 2, S), lambda b, p: (b, p, 0, 0)),
                  pl.BlockSpec((2, BLOCK, FOX_TK), lambda b, p: (0, 0, 0))],
        out_specs=blk,
        out_shape=jax.ShapeDtypeStruct((B, S, D_FOX), MXU_DTYPE),
        compiler_params=_cparams(2),
        name="fox_attention",
    )(qf, kf, vf, c4, jnp.asarray(_causal_bias()))


def _band_bias():
    qi = np.arange(BLOCK)[:, None]
    ki = np.arange(2 * BLOCK)[None, :]
    delta = BLOCK + qi - ki
    band = (delta >= 0) & (delta <= BLOCK)
    first = band & (ki >= BLOCK)
    return np.where(np.stack([first, band]), 0.0, MASK_BIAS).astype(np.float32)


def _dil_kernel(q_ref, k_ref, v_ref, bias_ref, o_ref, q32, k32, v32, q17, k17, v17, onorm_s, mu_s):
    S = q_ref.shape[1]
    q32[...] = q_ref[0].astype(F32)
    zeros = jnp.zeros((SKEW_PAD, LANES), F32)
    for plain, skewed, src in ((k32, k17, k_ref), (v32, v17, v_ref)):
        plain[0:DIL_PAD, :] = zeros[0:DIL_PAD]
        skewed[0:SKEW_PAD, :] = zeros
        plain[DIL_PAD:, :] = src[0].astype(F32)

    def skew_rows(g8, _):
        for u in range(8):
            src = pl.multiple_of(g8 * (8 * SKEW_DIL), 8 * SKEW_DIL) + u * SKEW_DIL
            dst = pl.multiple_of(g8 * (8 * SKEW_PITCH), 8) + u * SKEW_PITCH
            q17[pl.ds(dst, SKEW_DIL), :] = q32[pl.ds(src, SKEW_DIL), :]
            k17[pl.ds(SKEW_PAD + dst, SKEW_DIL), :] = k32[pl.ds(DIL_PAD + src, SKEW_DIL), :]
            v17[pl.ds(SKEW_PAD + dst, SKEW_DIL), :] = v32[pl.ds(DIL_PAD + src, SKEW_DIL), :]
        return 0

    lax.fori_loop(0, S // (8 * SKEW_DIL), skew_rows, 0)

    lane = lax.broadcasted_iota(jnp.int32, (BLOCK, LANES), 1)
    is_a = lane < HEAD_DIM
    ones = jnp.ones((2 * BLOCK, LANES), MXU_DTYPE)

    def rows(ref, start, count, stride):
        if stride == 1:
            return ref[pl.ds(start, count), :]
        return ref[pl.ds(start, count, stride=stride), :]

    def unit(dil, rho, n):
        if dil == SKEW_DIL:
            (qs, ks, vs), stride, pad = (q17, k17, v17), SKEW_PITCH, SKEW_PAD
        else:
            (qs, ks, vs), stride, pad = (q32, k32, v32), dil, DIL_PAD
        q0 = rho + stride * BLOCK * n
        qb = rows(qs, q0, BLOCK, stride).astype(MXU_DTYPE)
        kw = rows(ks, q0 + pad - stride * BLOCK, 2 * BLOCK, stride).astype(MXU_DTYPE)
        vw = rows(vs, q0 + pad - stride * BLOCK, 2 * BLOCK, stride).astype(MXU_DTYPE)
        bias = bias_ref[jnp.minimum(n, 1)]
        s = _dot_nt(_pair_rows(qb, is_a), kw)
        s = jnp.concatenate([s[:BLOCK] + bias, s[BLOCK:] + bias], axis=0)
        m = jnp.max(s, axis=1, keepdims=True)
        p = jnp.exp2(s - m).astype(MXU_DTYPE)
        o = _dot(p, jnp.concatenate([vw, ones], axis=1))
        acc = jnp.where(is_a, o[:BLOCK, :LANES], o[BLOCK:, :LANES])
        l = jnp.where(is_a, o[:BLOCK, LANES:], o[BLOCK:, LANES:])
        mu = jnp.where(is_a, m[:BLOCK], m[BLOCK:]) + jnp.log2(l)
        return acc / l, mu

    n_units = S // BLOCK

    for slab, dil in enumerate(d for d in DILATIONS if d != 1):
        blk_bits = (n_units // dil).bit_length() - 1

        def several_units(g, _, slab=slab, dil=dil, blk_bits=blk_bits):
            for u in range(DIL_UNITS_PER_BODY):
                t = g * DIL_UNITS_PER_BODY + u
                rho = lax.shift_right_logical(t, blk_bits)
                n = t & ((1 << blk_bits) - 1)
                o_n, mu = unit(dil, rho, n)
                q0 = rho + dil * BLOCK * n
                onorm_s[slab, pl.ds(q0, BLOCK, stride=dil), :] = o_n
                mu_s[slab, pl.ds(q0, BLOCK, stride=dil), :] = mu
            return 0

        lax.fori_loop(0, n_units // DIL_UNITS_PER_BODY, several_units, 0)

    def merge_blocks(g, _):
        for u in range(DIL_UNITS_PER_BODY):
            n = g * DIL_UNITS_PER_BODY + u
            o_1, mu_1 = unit(1, 0, n)
            r0 = pl.multiple_of(n * BLOCK, BLOCK)
            others = [(onorm_s[sl, pl.ds(r0, BLOCK), :], mu_s[sl, pl.ds(r0, BLOCK), :])
                      for sl in range(len(DILATIONS) - 1)]
            top = functools.reduce(jnp.maximum, [mu_1] + [mu for _, mu in others])
            w = jnp.exp2(mu_1 - top)
            num, den = w * o_1, w
            for o_p, mu_p in others:
                w = jnp.exp2(mu_p - top)
                num = num + w * o_p
                den = den + w
            o_ref[0, pl.ds(r0, BLOCK), :] = (num / den).astype(o_ref.dtype)
        return 0

    lax.fori_loop(0, n_units // DIL_UNITS_PER_BODY, merge_blocks, 0)


def _dilated_attention(qd, kd, vd):
    B, S, _ = qd.shape
    blk = pl.BlockSpec((1, S, LANES), lambda b, p: (b, 0, p))
    n_slabs = len(DILATIONS) - 1
    skewed = S // SKEW_DIL * SKEW_PITCH
    return pl.pallas_call(
        _dil_kernel,
        grid=(B, N_PAIRS_DIL),
        in_specs=[blk, blk, blk, pl.BlockSpec((2, BLOCK, 2 * BLOCK), lambda b, p: (0, 0, 0))],
        out_specs=blk,
        out_shape=jax.ShapeDtypeStruct((B, S, D_DIL), MXU_DTYPE),
        scratch_shapes=[
            pltpu.VMEM((S, LANES), F32),
            pltpu.VMEM((DIL_PAD + S, LANES), F32),
            pltpu.VMEM((DIL_PAD + S, LANES), F32),
            pltpu.VMEM((skewed, LANES), F32),
            pltpu.VMEM((SKEW_PAD + skewed, LANES), F32),
            pltpu.VMEM((SKEW_PAD + skewed, LANES), F32),
            pltpu.VMEM((n_slabs, S, LANES), F32),
            pltpu.VMEM((n_slabs, S, LANES), F32),
        ],
        compiler_params=_cparams(2),
        name="dilated_attention",
    )(qd, kd, vd, jnp.asarray(_band_bias()))


def _layer_norm(u, g, b):
    mu = jnp.mean(u, axis=1, keepdims=True)
    d = u - mu
    var = jnp.mean(d * d, axis=1, keepdims=True)
    return d * lax.rsqrt(var + LN_EPS) * g + b


def _rms_norm(x, g):
    ms = jnp.mean(x * x, axis=1, keepdims=True)
    return x * lax.rsqrt(ms + RMS_EPS) * g


def _top2_of4(vals):
    v1, i1 = vals[0], jnp.zeros(vals[0].shape, jnp.int32)
    for i in range(1, 4):
        better = vals[i] > v1
        v1 = jnp.where(better, vals[i], v1)
        i1 = jnp.where(better, i, i1)
    v2 = jnp.full(vals[0].shape, -1.0, F32)
    i2 = jnp.zeros(vals[0].shape, jnp.int32)
    for i in range(4):
        better = (vals[i] > v2) & (i1 != i)
        v2 = jnp.where(better, vals[i], v2)
        i2 = jnp.where(better, i, i2)
    return v1, i1, v2, i2


def _outproj_kernel(od_ref, of_ref, h_ref, wo_ref, gd_ref, gf_ref, lng_ref, lnb_ref,
                    wr_ref, br_ref, tri_ref,
                    h1_ref, h1t_ref, e_ref, gate_ref, rank_ref, cnt_ref, base_ref):
    step = pl.program_id(0)

    @pl.when(step == 0)
    def _():
        base_ref[...] = jnp.zeros_like(base_ref)

    xd = _rms_norm(od_ref[...].astype(F32), gd_ref[...])
    xf = _rms_norm(of_ref[...].astype(F32), gf_ref[...])
    y = _dot(xd.astype(MXU_DTYPE), wo_ref[0:D_DIL, :]) + _dot(xf.astype(MXU_DTYPE), wo_ref[D_DIL:, :])
    h1 = _layer_norm(DEEPNORM_ALPHA * h_ref[...] + y, lng_ref[...], lnb_ref[...])
    h1_ref[...] = h1
    _store_token_tiles(h1t_ref, h1)
    tm = h1.shape[0]

    h_hi, h_mid, _ = _split3(h1)
    two = _dot_nt(wr_ref[...], h_hi)
    logits = (two[:N_EXPERTS] + two[N_EXPERTS:]) + _dot_nt(wr_ref[0:N_EXPERTS, :], h_mid) + br_ref[...]
    logits = logits - jnp.max(logits, axis=0, keepdims=True)
    ex = jnp.exp(logits)
    probs = ex / jnp.sum(ex, axis=0, keepdims=True)
    pr = [probs[j:j + 1, :] for j in range(N_EXPERTS)]

    def group_score(g):
        v = pr[4 * g:4 * g + 4]
        pairs = [v[a] + v[b] for a in range(4) for b in range(a + 1, 4)]
        return functools.reduce(jnp.maximum, pairs)

    best = group_score(0)
    gsel = jnp.zeros((1, tm), jnp.int32)
    for g in range(1, N_GROUPS):
        sc = group_score(g)
        better = sc > best
        best = jnp.where(better, sc, best)
        gsel = jnp.where(better, g, gsel)
    in_grp = []
    for i in range(EXPERTS_PER_GROUP):
        v = pr[i]
        for g in range(1, N_GROUPS):
            v = jnp.where(gsel == g, pr[4 * g + i], v)
        in_grp.append(v)
    v1, i1, v2, i2 = _top2_of4(in_grp)
    e1 = gsel * EXPERTS_PER_GROUP + i1
    e2 = gsel * EXPERTS_PER_GROUP + i2
    den = v1 + v2
    e_ref[...] = jnp.concatenate([e1, e2], axis=0)
    gate_ref[...] = jnp.concatenate([v1 / den, v2 / den], axis=0)

    eidx = lax.broadcasted_iota(jnp.int32, (N_EXPERTS, tm), 0)
    oh1 = (eidx == e1).astype(F32)
    oh2 = (eidx == e2).astype(F32)
    tot = oh1 + oh2
    before = base_ref[...] + _dot(tot.astype(jnp.bfloat16), tri_ref[...])
    r1 = jnp.sum(oh1 * before, axis=0, keepdims=True)
    r2 = jnp.sum(oh2 * before, axis=0, keepdims=True)
    rank_ref[...] = jnp.concatenate([r1, r2], axis=0).astype(jnp.int32)
    base_ref[...] = base_ref[...] + jnp.sum(tot, axis=1, keepdims=True)
    cnt_ref[...] = jnp.broadcast_to(base_ref[...], cnt_ref.shape)


def _out_projection(od, of, h, w_out, g_dil, g_fox, ln_g, ln_b, wr3, b_router, tm=1024):
    T = h.shape[0]
    tri = (np.arange(tm)[:, None] < np.arange(tm)[None, :]).astype(np.float32)
    rows = lambda width: pl.BlockSpec((tm, width), lambda i: (i, 0))
    full = lambda a: pl.BlockSpec(a.shape, lambda i: (0,) * a.ndim)
    tok = pl.BlockSpec((TOP_K, tm), lambda i: (0, i))
    consts = [w_out, g_dil.reshape(1, -1), g_fox.reshape(1, -1), ln_g.reshape(1, -1), ln_b.reshape(1, -1),
              wr3, b_router.reshape(-1, 1).astype(F32), jnp.asarray(tri, jnp.bfloat16)]
    return pl.pallas_call(
        _outproj_kernel,
        grid=(T // tm,),
        in_specs=[rows(D_DIL), rows(D_FOX), rows(D_MODEL)] + [full(a) for a in consts],
        out_specs=(rows(D_MODEL), pl.BlockSpec((tm * TOKEN_TILE_ROWS, LANES), lambda i: (i, 0)),
                   tok, tok, tok, pl.BlockSpec((N_EXPERTS, LANES), lambda i: (0, 0))),
        out_shape=(
            jax.ShapeDtypeStruct((T, D_MODEL), F32),
            jax.ShapeDtypeStruct((T * TOKEN_TILE_ROWS, LANES), F32),
            jax.ShapeDtypeStruct((TOP_K, T), jnp.int32),
            jax.ShapeDtypeStruct((TOP_K, T), F32),
            jax.ShapeDtypeStruct((TOP_K, T), jnp.int32),
            jax.ShapeDtypeStruct((N_EXPERTS, LANES), F32),
        ),
        scratch_shapes=[pltpu.VMEM((N_EXPERTS, 1), F32)],
        compiler_params=_cparams(1),
        name="out_projection_router",
    )(od, of, h, *consts)


def _store_token_tiles(ref, x):
    n = x.shape[0]
    for c in range(TOKEN_TILE_ROWS):
        ref[pl.ds(c, n, stride=TOKEN_TILE_ROWS), :] = x[:, c * LANES:(c + 1) * LANES]


def _load_token_tiles(ref, n, first=0):
    return jnp.concatenate([ref[pl.ds(first * TOKEN_TILE_ROWS + c, n, stride=TOKEN_TILE_ROWS), :]
                            for c in range(TOKEN_TILE_ROWS)], axis=1)


def _tile_copy(src, src_row, dst, dst_row, sem):
    return pltpu.make_async_copy(src.at[pl.ds(src_row, TOKEN_TILE_ROWS)],
                                 dst.at[pl.ds(dst_row, TOKEN_TILE_ROWS)], sem)


def _dispatch_kernel(p0_ref, p1_ref, pad0_ref, padn_ref, h_ref, xg_out, sem, ztile, *, td):
    def copies(t):
        src = pl.multiple_of(t * TOKEN_TILE_ROWS, TOKEN_TILE_ROWS)
        return (_tile_copy(h_ref, src, xg_out, pl.multiple_of(p0_ref[t], TOKEN_TILE_ROWS), sem),
                _tile_copy(h_ref, src, xg_out, pl.multiple_of(p1_ref[t], TOKEN_TILE_ROWS), sem))

    def issue(t, _):
        for thread, cp in enumerate(copies(t)):
            cp.start(priority=thread)
        return 0

    def drain(t, _):
        for cp in copies(t):
            cp.wait()
        return 0

    lax.fori_loop(0, td, issue, 0, unroll=DMA_ISSUE_UNROLL)
    lax.fori_loop(0, td, drain, 0, unroll=DMA_ISSUE_UNROLL)

    @pl.when(pl.program_id(0) == pl.num_programs(0) - 1)
    def _():
        ztile[...] = jnp.zeros(ztile.shape, ztile.dtype)

        chunk_rows = ztile.shape[0]
        chunk_tokens = chunk_rows // TOKEN_TILE_ROWS

        def for_each_zero_copy(action):
            def hole(e, _):
                n_chunks = padn_ref[e] // chunk_tokens

                def chunk(c, _):
                    dst = pl.multiple_of(pad0_ref[e] + c * chunk_rows, TOKEN_TILE_ROWS)
                    action(pltpu.make_async_copy(ztile, xg_out.at[pl.ds(dst, chunk_rows)], sem))
                    return 0

                def single(r, _):
                    dst = pl.multiple_of(pad0_ref[e] + r * TOKEN_TILE_ROWS, TOKEN_TILE_ROWS)
                    action(_tile_copy(ztile, 0, xg_out, dst, sem))
                    return 0

                lax.fori_loop(0, n_chunks, chunk, 0)
                lax.fori_loop(n_chunks * chunk_tokens, padn_ref[e], single, 0)
                return 0
            lax.fori_loop(0, pad0_ref.shape[0], hole, 0)

        for_each_zero_copy(lambda cp: cp.start())
        for_each_zero_copy(lambda cp: cp.wait())


def _dispatch(h1t, pos_rows, pad_rows, pad_counts, n_rows, td=256):
    T = h1t.shape[0] // TOKEN_TILE_ROWS
    smem = pl.BlockSpec((td,), lambda i: (i,), memory_space=pltpu.SMEM)
    smem_all = pl.BlockSpec(memory_space=pltpu.SMEM)
    return pl.pallas_call(
        functools.partial(_dispatch_kernel, td=td),
        grid=(T // td,),
        in_specs=[smem, smem, smem_all, smem_all, pl.BlockSpec((td * TOKEN_TILE_ROWS, LANES), lambda i: (i, 0))],
        out_specs=pl.BlockSpec(memory_space=pl.ANY),
        out_shape=jax.ShapeDtypeStruct((n_rows * TOKEN_TILE_ROWS, LANES), F32),
        scratch_shapes=[pltpu.SemaphoreType.DMA(()), pltpu.VMEM((ZERO_CHUNK_TOKENS * TOKEN_TILE_ROWS, LANES), F32)],
        compiler_params=_cparams(1),
        name="moe_dispatch",
    )(pos_rows[0], pos_rows[1], pad_rows, pad_counts, h1t)


def _ffn_kernel(be_ref, nv_ref, slot_ref, nxt_ref, x_ref, wg_hbm, wu_hbm, wd_hbm, y_ref, wstage, wb, sem,
                *, tmf, layer):
    j = pl.program_id(0)
    valid = j < nv_ref[0]

    def fetch(expert, slot):
        return [pltpu.make_async_copy(w.at[layer, expert], wstage.at[slot, k], sem.at[slot])
                for k, w in enumerate((wg_hbm, wu_hbm, wd_hbm))]

    @pl.when(j == 0)
    def _():
        for cp in fetch(be_ref[0], 0):
            cp.start()

    @pl.when(valid & ((j == 0) | (be_ref[j] != be_ref[jnp.maximum(j - 1, 0)])))
    def _():
        slot = slot_ref[j]
        for cp in fetch(be_ref[j], slot):
            cp.wait()
        for k in range(3):
            wb[k] = wstage[slot, k].astype(MXU_DTYPE)

        @pl.when(nxt_ref[j] >= 0)
        def _():
            for cp in fetch(nxt_ref[j], 1 - slot):
                cp.start()

    @pl.when(valid)
    def _():
        xb = _load_token_tiles(x_ref, tmf).astype(MXU_DTYPE)
        a = _dot(xb, wb[0])
        u = _dot(xb, wb[1])
        hmid = (a * jax.nn.sigmoid(a)) * u
        _store_token_tiles(y_ref, _dot(hmid.astype(MXU_DTYPE), wb[2]))

    @pl.when(j >= nv_ref[0])
    def _():
        y_ref[...] = jnp.zeros_like(y_ref)


def _expert_ffn(xg, blk_e, n_valid, present, wg, wu, wd, layer, tmf):
    n_rows = xg.shape[0] // TOKEN_TILE_ROWS
    D = D_MODEL
    nb = n_rows // tmf
    eids = jnp.arange(N_EXPERTS, dtype=jnp.int32)
    order = jnp.cumsum(present.astype(jnp.int32)) - 1
    later = (eids[None, :] > eids[:, None]) & present[None, :]
    nxt_e = jnp.min(jnp.where(later, eids[None, :], N_EXPERTS), axis=1)
    nxt_e = jnp.where(nxt_e == N_EXPERTS, -1, nxt_e)
    slot = (order & 1)[blk_e].astype(jnp.int32)
    nxt = nxt_e[blk_e].astype(jnp.int32)
    xmap = lambda j, be, nv, sl, nx: (jnp.minimum(j, nv[0] - 1), 0)
    anyspec = pl.BlockSpec(memory_space=pl.ANY)
    grid_spec = pltpu.PrefetchScalarGridSpec(
        num_scalar_prefetch=4,
        grid=(nb,),
        in_specs=[pl.BlockSpec((tmf * TOKEN_TILE_ROWS, LANES), xmap), anyspec, anyspec, anyspec],
        out_specs=pl.BlockSpec((tmf * TOKEN_TILE_ROWS, LANES), lambda j, be, nv, sl, nx: (j, 0)),
        scratch_shapes=[pltpu.VMEM((2, 3, D, D), F32), pltpu.VMEM((3, D, D), MXU_DTYPE),
                        pltpu.SemaphoreType.DMA((2,))],
    )
    return pl.pallas_call(
        functools.partial(_ffn_kernel, tmf=tmf, layer=layer),
        grid_spec=grid_spec,
        out_shape=jax.ShapeDtypeStruct(xg.shape, F32),
        compiler_params=_cparams(1),
        name="moe_expert_ffn",
    )(blk_e, n_valid, slot, nxt, xg, wg, wu, wd)


def _combine_kernel(p0_ref, p1_ref, gate_ref, h_ref, lng_ref, lnb_ref, y_hbm, o_ref, buf0, buf1, sem, *, tc):
    part_len = tc // COMBINE_PARTS

    def copies(t, part):
        dst = pl.multiple_of(t * TOKEN_TILE_ROWS, TOKEN_TILE_ROWS)
        return (_tile_copy(y_hbm, pl.multiple_of(p0_ref[t], TOKEN_TILE_ROWS), buf0, dst, sem.at[part]),
                _tile_copy(y_hbm, pl.multiple_of(p1_ref[t], TOKEN_TILE_ROWS), buf1, dst, sem.at[part]))

    def issue(part):
        def body(t, _):
            for thread, cp in enumerate(copies(t, part)):
                cp.start(priority=thread)
            return 0
        return body

    def drain(part):
        def body(t, _):
            for cp in copies(t, part):
                cp.wait()
            return 0
        return body

    for part in range(COMBINE_PARTS):
        lax.fori_loop(part * part_len, (part + 1) * part_len, issue(part), 0, unroll=DMA_ISSUE_UNROLL)
    for part in range(COMBINE_PARTS):
        first = part * part_len
        lax.fori_loop(first, first + part_len, drain(part), 0, unroll=DMA_ISSUE_UNROLL)
        g = gate_ref[first:first + part_len, :]
        y = (_load_token_tiles(buf0, part_len, first) * g[:, 0:1]
             + _load_token_tiles(buf1, part_len, first) * g[:, 1:2])
        o_ref[first:first + part_len, :] = _layer_norm(DEEPNORM_ALPHA * h_ref[first:first + part_len, :] + y,
                                                       lng_ref[...], lnb_ref[...])


def _combine(yg, pos_rows, gates_col, h1, ln_g, ln_b, tc=256):
    T, D = h1.shape
    smem = pl.BlockSpec((tc,), lambda i: (i,), memory_space=pltpu.SMEM)
    rows = lambda width: pl.BlockSpec((tc, width), lambda i: (i, 0))
    vec = pl.BlockSpec((1, D), lambda i: (0, 0))
    buf = pltpu.VMEM((tc * TOKEN_TILE_ROWS, LANES), F32)
    return pl.pallas_call(
        functools.partial(_combine_kernel, tc=tc),
        grid=(T // tc,),
        in_specs=[smem, smem, rows(TOP_K), rows(D), vec, vec, pl.BlockSpec(memory_space=pl.ANY)],
        out_specs=rows(D),
        out_shape=jax.ShapeDtypeStruct((T, D), F32),
        scratch_shapes=[buf, buf, pltpu.SemaphoreType.DMA((COMBINE_PARTS,))],
        compiler_params=_cparams(1),
        name="moe_combine_ln",
    )(pos_rows[0], pos_rows[1], gates_col, h1, ln_g.reshape(1, -1), ln_b.reshape(1, -1), yg)


def _grouped_moe(h1, h1t, experts, gates, ranks, counts, wg, wu, wd, layer, ln_g, ln_b, tmf=512):
    T = h1.shape[0]
    nb = (T * TOP_K) // tmf + N_EXPERTS
    cnt = counts[:, 0].astype(jnp.int32)
    pcnt = (cnt + tmf - 1) // tmf * tmf
    pends = jnp.cumsum(pcnt)
    pstart = pends - pcnt
    eids = jnp.arange(N_EXPERTS, dtype=jnp.int32)
    seg = jnp.sum(jnp.where(experts[None] == eids[:, None, None], pstart[:, None, None], 0), axis=0)
    pos_rows = (seg + ranks) * TOKEN_TILE_ROWS
    blk_start = jnp.arange(nb, dtype=jnp.int32) * tmf
    blk_e = jnp.minimum(jnp.sum((pends[None, :] <= blk_start[:, None]).astype(jnp.int32), axis=1), N_EXPERTS - 1)
    n_valid = (pends[-1:] // tmf).astype(jnp.int32)
    hole_start = jnp.concatenate([pstart + cnt, pends[-1:]])
    hole_rows = jnp.concatenate([pcnt - cnt, nb * tmf - pends[-1:]])
    xg = _dispatch(h1t, pos_rows, hole_start * TOKEN_TILE_ROWS, hole_rows, nb * tmf)
    yg = _expert_ffn(xg, blk_e, n_valid, cnt > 0, wg, wu, wd, layer, tmf)
    return _combine(yg, pos_rows, gates.T, h1, ln_g, ln_b)


def _rope_tables(S):
    half = HEAD_DIM // 2
    inv = ROPE_THETA ** (-jnp.arange(half, dtype=F32) / half)
    ang = jnp.arange(S, dtype=F32)[:, None] * inv[None, :]
    cos, sin = jnp.cos(ang), jnp.sin(ang)
    reps = LANES // HEAD_DIM
    cos_tab = jnp.tile(jnp.concatenate([cos, cos], axis=1), (1, reps))
    sin_tab = jnp.tile(jnp.concatenate([-sin, sin], axis=1), (1, reps))
    return cos_tab, sin_tab


def _pad_w_in(w):
    pad = jnp.zeros((D_MODEL, D_PROJ_PAD - w.shape[1]), w.dtype)
    return jnp.concatenate([w, pad], axis=1).astype(MXU_DTYPE)


def kernel(x, w_in, b_forget, g_dil, g_fox, w_out, ln1_g, ln1_b, w_router, b_router,
           w_gate, w_up, w_down, ln2_g, ln2_b):
    B, S, D = x.shape
    T = B * S
    cos_tab, sin_tab = _rope_tables(S)
    wr_hi, wr_mid, _ = _split3(w_router.astype(F32).T)
    wr3 = jnp.concatenate([wr_hi, wr_mid], axis=0)
    h = x
    for l in range(DEPTH):
        qd, kd, vd, qf, kf, vf, zf = _in_projection(h, _pad_w_in(w_in[l]), cos_tab, sin_tab)
        c = _forget_cumsum(zf, b_forget[l])[:, :N_HEADS_FOX]
        o_fox = _fox_attention(qf, kf, vf, c)
        o_dil = _dilated_attention(qd, kd, vd)
        h1, h1t, experts, gates, ranks, counts = _out_projection(
            o_dil.reshape(T, D_DIL), o_fox.reshape(T, D_FOX), h.reshape(T, D),
            w_out[l].astype(MXU_DTYPE), g_dil[l], g_fox[l], ln1_g[l], ln1_b[l], wr3, b_router)
        h2 = _grouped_moe(h1, h1t, experts, gates, ranks, counts,
                          w_gate, w_up, w_down, l,
                          ln2_g[l], ln2_b[l])
        h = h2.reshape(B, S, D)
    return h
```

```python
import functools

import numpy as np
import jax
import jax.numpy as jnp
from jax import lax
from jax.experimental import pallas as pl
from jax.experimental.pallas import tpu as pltpu

D_MODEL = 1024
DEPTH = 2
HEAD_DIM = 64
N_HEADS_DIL = 12
N_HEADS_FOX = 4
D_DIL = N_HEADS_DIL * HEAD_DIM
D_FOX = N_HEADS_FOX * HEAD_DIM
DILATIONS = (1, 4, 16)
BLOCK = 128
ROPE_THETA = 10000.0
N_EXPERTS = 16
N_GROUPS = 4
EXPERTS_PER_GROUP = 4
TOP_K = 2
DEEPNORM_ALPHA = (2.0 * DEPTH) ** 0.25
LN_EPS = 1e-5
RMS_EPS = 1e-6

LANES = 128
N_PAIRS_DIL = D_DIL // LANES
D_PROJ_PAD = 3 * D_DIL + 3 * D_FOX + LANES
VMEM_LIMIT = 48 * 1024 * 1024
TOKEN_TILE_ROWS = D_MODEL // LANES
DMA_ISSUE_UNROLL = 8
ZERO_CHUNK_TOKENS = 32
SKEW_DIL = 16
SKEW_PITCH = SKEW_DIL + 1
DIL_PAD = max(d for d in DILATIONS if d != SKEW_DIL) * BLOCK
SKEW_PAD = SKEW_PITCH * BLOCK
MASK_BIAS = -1e30
LOG2_E = 1.4426950408889634
DIL_UNITS_PER_BODY = 16
FOX_TK = 256
FOX_BLOCKS_PER_BODY = 4
ZF_ROWS = 8

MXU_DTYPE = jnp.bfloat16
F32 = jnp.float32
NEG_INF = float("-inf")


def _cparams(n_axes):
    return pltpu.CompilerParams(dimension_semantics=("arbitrary",) * n_axes,
                                vmem_limit_bytes=VMEM_LIMIT)


def _dot(a, b):
    return jnp.dot(a, b, preferred_element_type=F32)


def _dot_nt(a, b):
    return lax.dot_general(a, b, (((1,), (1,)), ((), ())), preferred_element_type=F32)


def _inproj_kernel(x_ref, w_ref, cos_ref, sin_ref,
                   qd_ref, kd_ref, vd_ref, qf_ref, kf_ref, vf_ref, zf_ref):
    xb = x_ref[0].astype(MXU_DTYPE)
    tm = xb.shape[0]
    cos = cos_ref[...]
    sin = sin_ref[...]
    lane = lax.broadcasted_iota(jnp.int32, (tm, LANES), 1)
    first_half = (lane % HEAD_DIM) < (HEAD_DIM // 2)

    def rope(z):
        rot = jnp.where(first_half, pltpu.roll(z, LANES - 32, 1), pltpu.roll(z, 32, 1))
        return z * cos + rot * sin

    def proj(col, width):
        return _dot(xb, w_ref[:, col:col + width])

    scale = HEAD_DIM ** -0.5
    col = 0
    for out_ref, roped, mul in ((qd_ref, True, scale * LOG2_E), (kd_ref, True, 1.0)):
        for c in range(D_DIL // 256):
            z = proj(col, 256)
            for half in range(2):
                zz = rope(z[:, half * LANES:(half + 1) * LANES]) * mul
                out_ref[0, :, c * 256 + half * LANES:c * 256 + (half + 1) * LANES] = zz.astype(out_ref.dtype)
            col += 256
    for c in range(D_DIL // 256):
        vd_ref[0, :, c * 256:(c + 1) * 256] = proj(col, 256).astype(vd_ref.dtype)
        col += 256
    qf_ref[0] = (proj(col, D_FOX) * (scale * LOG2_E)).astype(qf_ref.dtype)
    col += D_FOX
    kf_ref[0] = proj(col, D_FOX).astype(kf_ref.dtype)
    col += D_FOX
    vf_ref[0] = proj(col, D_FOX).astype(vf_ref.dtype)
    col += D_FOX
    zf_ref[0] = proj(col, LANES).T[0:ZF_ROWS, :]


def _in_projection(h, w_pad, cos_tab, sin_tab, tm=1024):
    B, S, D = h.shape
    grid = (B, S // tm)
    row = lambda width: pl.BlockSpec((1, tm, width), lambda b, i: (b, i, 0))
    out_shape = (
        jax.ShapeDtypeStruct((B, S, D_DIL), MXU_DTYPE),
        jax.ShapeDtypeStruct((B, S, D_DIL), MXU_DTYPE),
        jax.ShapeDtypeStruct((B, S, D_DIL), MXU_DTYPE),
        jax.ShapeDtypeStruct((B, S, D_FOX), MXU_DTYPE),
        jax.ShapeDtypeStruct((B, S, D_FOX), MXU_DTYPE),
        jax.ShapeDtypeStruct((B, S, D_FOX), MXU_DTYPE),
        jax.ShapeDtypeStruct((B, ZF_ROWS, S), F32),
    )
    return pl.pallas_call(
        _inproj_kernel,
        grid=grid,
        in_specs=[
            row(D),
            pl.BlockSpec((D, D_PROJ_PAD), lambda b, i: (0, 0)),
            pl.BlockSpec((tm, LANES), lambda b, i: (i, 0)),
            pl.BlockSpec((tm, LANES), lambda b, i: (i, 0)),
        ],
        out_specs=(row(D_DIL), row(D_DIL), row(D_DIL), row(D_FOX), row(D_FOX), row(D_FOX),
                   pl.BlockSpec((1, ZF_ROWS, tm), lambda b, i: (b, 0, i))),
        out_shape=out_shape,
        compiler_params=_cparams(2),
        name="in_projection",
    )(h, w_pad, cos_tab, sin_tab)


def _split3(x):
    hi = x.astype(jnp.bfloat16)
    r1 = x - hi.astype(F32)
    mid = r1.astype(jnp.bfloat16)
    lo = (r1 - mid.astype(F32)).astype(jnp.bfloat16)
    return hi, mid, lo


def _forget_cumsum_kernel(z_ref, b_ref, tri_ref, c_ref):
    x = z_ref[0] + b_ref[...]
    logf = jnp.minimum(x, 0.0) - jnp.log1p(jnp.exp(-jnp.abs(x)))
    tri = tri_ref[...]
    S = x.shape[1]
    carry = jnp.zeros((x.shape[0], 1), F32)
    for blk in range(S // LANES):
        seg = logf[:, blk * LANES:(blk + 1) * LANES]
        hi, mid, lo = _split3(seg)
        cs = (_dot(hi, tri) + _dot(mid, tri)) + _dot(lo, tri) + carry
        c_ref[0, :, blk * LANES:(blk + 1) * LANES] = cs * LOG2_E
        carry = cs[:, LANES - 1:LANES]


def _forget_cumsum(zt, b_forget):
    B, Hf, S = zt.shape
    tri = (np.arange(LANES)[:, None] <= np.arange(LANES)[None, :]).astype(np.float32)
    return pl.pallas_call(
        _forget_cumsum_kernel,
        grid=(B,),
        in_specs=[
            pl.BlockSpec((1, Hf, S), lambda b: (b, 0, 0)),
            pl.BlockSpec((Hf, 1), lambda b: (0, 0)),
            pl.BlockSpec((LANES, LANES), lambda b: (0, 0)),
        ],
        out_specs=pl.BlockSpec((1, Hf, S), lambda b: (b, 0, 0)),
        out_shape=jax.ShapeDtypeStruct((B, Hf, S), F32),
        compiler_params=_cparams(1),
        name="forget_cumsum",
    )(zt, jnp.pad(b_forget.astype(F32), (0, Hf - b_forget.shape[0])).reshape(Hf, 1), jnp.asarray(tri, jnp.bfloat16))


def _pair_rows(xp, is_a):
    zero = jnp.zeros_like(xp)
    return jnp.concatenate([jnp.where(is_a, xp, zero), jnp.where(is_a, zero, xp)], axis=0)


def _causal_bias():
    qi = np.arange(BLOCK)[:, None]
    ki = np.arange(FOX_TK)[None, :]
    vis = np.stack([ki <= qi, ki <= qi + BLOCK])
    return np.where(vis, 0.0, MASK_BIAS).astype(np.float32)


def _fox_kernel(q_ref, k_ref, v_ref, crow_ref, cbias_ref, o_ref):
    S = q_ref.shape[1]
    tk = FOX_TK
    lane = lax.broadcasted_iota(jnp.int32, (BLOCK, LANES), 1)
    is_a = lane < HEAD_DIM
    ones = jnp.ones((tk, LANES), MXU_DTYPE)

    def unit(q2, kv, bias):
        kb, v_ones, cr = kv
        s = _dot_nt(q2, kb)
        sa = s[:BLOCK] - cr[0:1]
        sb = s[BLOCK:] - cr[1:2]
        if bias is not None:
            sa = sa + bias
            sb = sb + bias
        s = jnp.concatenate([sa, sb], axis=0)
        m = jnp.max(s, axis=1, keepdims=True)
        p = jnp.exp2(s - m).astype(MXU_DTYPE)
        o = _dot(p, v_ones)
        acc = jnp.where(is_a, o[:BLOCK, :LANES], o[BLOCK:, :LANES])
        l = jnp.where(is_a, o[:BLOCK, LANES:], o[BLOCK:, LANES:])
        return acc, l, jnp.where(is_a, m[:BLOCK], m[BLOCK:])

    def key_block(j):
        k0 = pl.multiple_of(j * tk, tk)
        return (k_ref[0, pl.ds(k0, tk), :],
                jnp.concatenate([v_ref[0, pl.ds(k0, tk), :], ones], axis=1),
                crow_ref[0, 0, :, pl.ds(k0, tk)])

    def fold(state, units):
        num, den, top_old = state
        top = functools.reduce(jnp.maximum, [top_old] + [m for _, _, m in units])
        scale = jnp.exp2(top_old - top)
        num, den = scale * num, scale * den
        for acc, l, m in units:
            w = jnp.exp2(m - top)
            num = num + w * acc
            den = den + w * l
        return num, den, top

    def q_block(i, _):
        r0 = pl.multiple_of(i * tk, tk)
        q2 = [_pair_rows(q_ref[0, pl.ds(r0 + h * BLOCK, BLOCK), :], is_a) for h in range(2)]
        zero = jnp.zeros((BLOCK, LANES), F32)
        empty = (zero, zero, jnp.full((BLOCK, LANES), MASK_BIAS, F32))

        def fold_blocks(states, blocks):
            units = [[unit(q2[h], kv, None if bias is None else bias[h]) for h in range(2)]
                     for kv, bias in ((key_block(j), bias) for j, bias in blocks)]
            return tuple(fold(states[h], [u[h] for u in units]) for h in range(2))

        def full_blocks(g, states):
            return fold_blocks(states, [(FOX_BLOCKS_PER_BODY * g + u, None) for u in range(FOX_BLOCKS_PER_BODY)])

        states = lax.fori_loop(0, i // FOX_BLOCKS_PER_BODY, full_blocks, (empty, empty))
        diag = (i, (cbias_ref[0], cbias_ref[1]))

        def tail(n_left):
            return lambda st: fold_blocks(st, [(i - n_left + u, None) for u in range(n_left)] + [diag])

        states = lax.switch(i % FOX_BLOCKS_PER_BODY, [tail(r) for r in range(FOX_BLOCKS_PER_BODY)], states)
        for h in range(2):
            num, den, _ = states[h]
            o_ref[0, pl.ds(r0 + h * BLOCK, BLOCK), :] = (num / den).astype(o_ref.dtype)
        return 0

    lax.fori_loop(0, S // tk, q_block, 0)


def _fox_attention(qf, kf, vf, c2):
    B, S, _ = qf.shape
    n_pairs = D_FOX // LANES
    c4 = c2.reshape(B, n_pairs, 2, S)
    blk = pl.BlockSpec((1, S, LANES), lambda b, p: (b, 0, p))
    return pl.pallas_call(
        _fox_kernel,
        grid=(B, n_pairs),
        in_specs=[blk, blk, blk,
                  pl.BlockSpec((1, 1, 2, S), lambda b, p: (b, p, 0, 0)),
                  pl.BlockSpec((2, BLOCK, FOX_TK), lambda b, p: (0, 0, 0))],
        out_specs=blk,
        out_shape=jax.ShapeDtypeStruct((B, S, D_FOX), MXU_DTYPE),
        compiler_params=_cparams(2),
        name="fox_attention",
    )(qf, kf, vf, c4, jnp.asarray(_causal_bias()))


def _band_bias():
    qi = np.arange(BLOCK)[:, None]
    ki = np.arange(2 * BLOCK)[None, :]
    delta = BLOCK + qi - ki
    band = (delta >= 0) & (delta <= BLOCK)
    first = band & (ki >= BLOCK)
    return np.where(np.stack([first, band]), 0.0, MASK_BIAS).astype(np.float32)


def _dil_kernel(q_ref, k_ref, v_ref, bias_ref, o_ref, q32, k32, v32, q17, k17, v17, onorm_s, mu_s):
    S = q_ref.shape[1]
    q32[...] = q_ref[0].astype(F32)
    zeros = jnp.zeros((SKEW_PAD, LANES), F32)
    for plain, skewed, src in ((k32, k17, k_ref), (v32, v17, v_ref)):
        plain[0:DIL_PAD, :] = zeros[0:DIL_PAD]
        skewed[0:SKEW_PAD, :] = zeros
        plain[DIL_PAD:, :] = src[0].astype(F32)

    def skew_rows(g8, _):
        for u in range(8):
            src = pl.multiple_of(g8 * (8 * SKEW_DIL), 8 * SKEW_DIL) + u * SKEW_DIL
            dst = pl.multiple_of(g8 * (8 * SKEW_PITCH), 8) + u * SKEW_PITCH
            q17[pl.ds(dst, SKEW_DIL), :] = q32[pl.ds(src, SKEW_DIL), :]
            k17[pl.ds(SKEW_PAD + dst, SKEW_DIL), :] = k32[pl.ds(DIL_PAD + src, SKEW_DIL), :]
            v17[pl.ds(SKEW_PAD + dst, SKEW_DIL), :] = v32[pl.ds(DIL_PAD + src, SKEW_DIL), :]
        return 0

    lax.fori_loop(0, S // (8 * SKEW_DIL), skew_rows, 0)

    lane = lax.broadcasted_iota(jnp.int32, (BLOCK, LANES), 1)
    is_a = lane < HEAD_DIM
    ones = jnp.ones((2 * BLOCK, LANES), MXU_DTYPE)

    def rows(ref, start, count, stride):
        if stride == 1:
            return ref[pl.ds(start, count), :]
        return ref[pl.ds(start, count, stride=stride), :]

    def unit(dil, rho, n):
        if dil == SKEW_DIL:
            (qs, ks, vs), stride, pad = (q17, k17, v17), SKEW_PITCH, SKEW_PAD
        else:
            (qs, ks, vs), stride, pad = (q32, k32, v32), dil, DIL_PAD
        q0 = rho + stride * BLOCK * n
        qb = rows(qs, q0, BLOCK, stride).astype(MXU_DTYPE)
        kw = rows(ks, q0 + pad - stride * BLOCK, 2 * BLOCK, stride).astype(MXU_DTYPE)
        vw = rows(vs, q0 + pad - stride * BLOCK, 2 * BLOCK, stride).astype(MXU_DTYPE)
        bias = bias_ref[jnp.minimum(n, 1)]
        s = _dot_nt(_pair_rows(qb, is_a), kw)
        s = jnp.concatenate([s[:BLOCK] + bias, s[BLOCK:] + bias], axis=0)
        m = jnp.max(s, axis=1, keepdims=True)
        p = jnp.exp2(s - m).astype(MXU_DTYPE)
        o = _dot(p, jnp.concatenate([vw, ones], axis=1))
        acc = jnp.where(is_a, o[:BLOCK, :LANES], o[BLOCK:, :LANES])
        l = jnp.where(is_a, o[:BLOCK, LANES:], o[BLOCK:, LANES:])
        mu = jnp.where(is_a, m[:BLOCK], m[BLOCK:]) + jnp.log2(l)
        return acc / l, mu

    n_units = S // BLOCK

    for slab, dil in enumerate(d for d in DILATIONS if d != 1):
        blk_bits = (n_units // dil).bit_length() - 1

        def several_units(g, _, slab=slab, dil=dil, blk_bits=blk_bits):
            for u in range(DIL_UNITS_PER_BODY):
                t = g * DIL_UNITS_PER_BODY + u
                rho = lax.shift_right_logical(t, blk_bits)
                n = t & ((1 << blk_bits) - 1)
                o_n, mu = unit(dil, rho, n)
                q0 = rho + dil * BLOCK * n
                onorm_s[slab, pl.ds(q0, BLOCK, stride=dil), :] = o_n
                mu_s[slab, pl.ds(q0, BLOCK, stride=dil), :] = mu
            return 0

        lax.fori_loop(0, n_units // DIL_UNITS_PER_BODY, several_units, 0)

    def merge_blocks(g, _):
        for u in range(DIL_UNITS_PER_BODY):
            n = g * DIL_UNITS_PER_BODY + u
            o_1, mu_1 = unit(1, 0, n)
            r0 = pl.multiple_of(n * BLOCK, BLOCK)
            others = [(onorm_s[sl, pl.ds(r0, BLOCK), :], mu_s[sl, pl.ds(r0, BLOCK), :])
                      for sl in range(len(DILATIONS) - 1)]
            top = functools.reduce(jnp.maximum, [mu_1] + [mu for _, mu in others])
            w = jnp.exp2(mu_1 - top)
            num, den = w * o_1, w
            for o_p, mu_p in others:
                w = jnp.exp2(mu_p - top)
                num = num + w * o_p
                den = den + w
            o_ref[0, pl.ds(r0, BLOCK), :] = (num / den).astype(o_ref.dtype)
        return 0

    lax.fori_loop(0, n_units // DIL_UNITS_PER_BODY, merge_blocks, 0)


def _dilated_attention(qd, kd, vd):
    B, S, _ = qd.shape
    blk = pl.BlockSpec((1, S, LANES), lambda b, p: (b, 0, p))
    n_slabs = len(DILATIONS) - 1
    skewed = S // SKEW_DIL * SKEW_PITCH
    return pl.pallas_call(
        _dil_kernel,
        grid=(B, N_PAIRS_DIL),
        in_specs=[blk, blk, blk, pl.BlockSpec((2, BLOCK, 2 * BLOCK), lambda b, p: (0, 0, 0))],
        out_specs=blk,
        out_shape=jax.ShapeDtypeStruct((B, S, D_DIL), MXU_DTYPE),
        scratch_shapes=[
            pltpu.VMEM((S, LANES), F32),
            pltpu.VMEM((DIL_PAD + S, LANES), F32),
            pltpu.VMEM((DIL_PAD + S, LANES), F32),
            pltpu.VMEM((skewed, LANES), F32),
            pltpu.VMEM((SKEW_PAD + skewed, LANES), F32),
            pltpu.VMEM((SKEW_PAD + skewed, LANES), F32),
            pltpu.VMEM((n_slabs, S, LANES), F32),
            pltpu.VMEM((n_slabs, S, LANES), F32),
        ],
        compiler_params=_cparams(2),
        name="dilated_attention",
    )(qd, kd, vd, jnp.asarray(_band_bias()))


def _layer_norm(u, g, b):
    mu = jnp.mean(u, axis=1, keepdims=True)
    d = u - mu
    var = jnp.mean(d * d, axis=1, keepdims=True)
    return d * lax.rsqrt(var + LN_EPS) * g + b


def _rms_norm(x, g):
    ms = jnp.mean(x * x, axis=1, keepdims=True)
    return x * lax.rsqrt(ms + RMS_EPS) * g


def _top2_of4(vals):
    v1, i1 = vals[0], jnp.zeros(vals[0].shape, jnp.int32)
    for i in range(1, 4):
        better = vals[i] > v1
        v1 = jnp.where(better, vals[i], v1)
        i1 = jnp.where(better, i, i1)
    v2 = jnp.full(vals[0].shape, -1.0, F32)
    i2 = jnp.zeros(vals[0].shape, jnp.int32)
    for i in range(4):
        better = (vals[i] > v2) & (i1 != i)
        v2 = jnp.where(better, vals[i], v2)
        i2 = jnp.where(better, i, i2)
    return v1, i1, v2, i2


def _outproj_kernel(od_ref, of_ref, h_ref, wo_ref, gd_ref, gf_ref, lng_ref, lnb_ref,
                    wr_ref, br_ref, tri_ref,
                    h1_ref, h1t_ref, e_ref, gate_ref, rank_ref, cnt_ref, base_ref):
    step = pl.program_id(0)

    @pl.when(step == 0)
    def _():
        base_ref[...] = jnp.zeros_like(base_ref)

    xd = _rms_norm(od_ref[...].astype(F32), gd_ref[...])
    xf = _rms_norm(of_ref[...].astype(F32), gf_ref[...])
    y = _dot(xd.astype(MXU_DTYPE), wo_ref[0:D_DIL, :]) + _dot(xf.astype(MXU_DTYPE), wo_ref[D_DIL:, :])
    h1 = _layer_norm(DEEPNORM_ALPHA * h_ref[...] + y, lng_ref[...], lnb_ref[...])
    h1_ref[...] = h1
    _store_token_tiles(h1t_ref, h1)
    tm = h1.shape[0]

    h_hi, h_mid, _ = _split3(h1)
    two = _dot_nt(wr_ref[...], h_hi)
    logits = (two[:N_EXPERTS] + two[N_EXPERTS:]) + _dot_nt(wr_ref[0:N_EXPERTS, :], h_mid) + br_ref[...]
    logits = logits - jnp.max(logits, axis=0, keepdims=True)
    ex = jnp.exp(logits)
    probs = ex / jnp.sum(ex, axis=0, keepdims=True)
    pr = [probs[j:j + 1, :] for j in range(N_EXPERTS)]

    def group_score(g):
        v = pr[4 * g:4 * g + 4]
        pairs = [v[a] + v[b] for a in range(4) for b in range(a + 1, 4)]
        return functools.reduce(jnp.maximum, pairs)

    best = group_score(0)
    gsel = jnp.zeros((1, tm), jnp.int32)
    for g in range(1, N_GROUPS):
        sc = group_score(g)
        better = sc > best
        best = jnp.where(better, sc, best)
        gsel = jnp.where(better, g, gsel)
    in_grp = []
    for i in range(EXPERTS_PER_GROUP):
        v = pr[i]
        for g in range(1, N_GROUPS):
            v = jnp.where(gsel == g, pr[4 * g + i], v)
        in_grp.append(v)
    v1, i1, v2, i2 = _top2_of4(in_grp)
    e1 = gsel * EXPERTS_PER_GROUP + i1
    e2 = gsel * EXPERTS_PER_GROUP + i2
    den = v1 + v2
    e_ref[...] = jnp.concatenate([e1, e2], axis=0)
    gate_ref[...] = jnp.concatenate([v1 / den, v2 / den], axis=0)

    eidx = lax.broadcasted_iota(jnp.int32, (N_EXPERTS, tm), 0)
    oh1 = (eidx == e1).astype(F32)
    oh2 = (eidx == e2).astype(F32)
    tot = oh1 + oh2
    before = base_ref[...] + _dot(tot.astype(jnp.bfloat16), tri_ref[...])
    r1 = jnp.sum(oh1 * before, axis=0, keepdims=True)
    r2 = jnp.sum(oh2 * before, axis=0, keepdims=True)
    rank_ref[...] = jnp.concatenate([r1, r2], axis=0).astype(jnp.int32)
    base_ref[...] = base_ref[...] + jnp.sum(tot, axis=1, keepdims=True)
    cnt_ref[...] = jnp.broadcast_to(base_ref[...], cnt_ref.shape)


def _out_projection(od, of, h, w_out, g_dil, g_fox, ln_g, ln_b, wr3, b_router, tm=1024):
    T = h.shape[0]
    tri = (np.arange(tm)[:, None] < np.arange(tm)[None, :]).astype(np.float32)
    rows = lambda width: pl.BlockSpec((tm, width), lambda i: (i, 0))
    full = lambda a: pl.BlockSpec(a.shape, lambda i: (0,) * a.ndim)
    tok = pl.BlockSpec((TOP_K, tm), lambda i: (0, i))
    consts = [w_out, g_dil.reshape(1, -1), g_fox.reshape(1, -1), ln_g.reshape(1, -1), ln_b.reshape(1, -1),
              wr3, b_router.reshape(-1, 1).astype(F32), jnp.asarray(tri, jnp.bfloat16)]
    return pl.pallas_call(
        _outproj_kernel,
        grid=(T // tm,),
        in_specs=[rows(D_DIL), rows(D_FOX), rows(D_MODEL)] + [full(a) for a in consts],
        out_specs=(rows(D_MODEL), pl.BlockSpec((tm * TOKEN_TILE_ROWS, LANES), lambda i: (i, 0)),
                   tok, tok, tok, pl.BlockSpec((N_EXPERTS, LANES), lambda i: (0, 0))),
        out_shape=(
            jax.ShapeDtypeStruct((T, D_MODEL), F32),
            jax.ShapeDtypeStruct((T * TOKEN_TILE_ROWS, LANES), F32),
            jax.ShapeDtypeStruct((TOP_K, T), jnp.int32),
            jax.ShapeDtypeStruct((TOP_K, T), F32),
            jax.ShapeDtypeStruct((TOP_K, T), jnp.int32),
            jax.ShapeDtypeStruct((N_EXPERTS, LANES), F32),
        ),
        scratch_shapes=[pltpu.VMEM((N_EXPERTS, 1), F32)],
        compiler_params=_cparams(1),
        name="out_projection_router",
    )(od, of, h, *consts)


def _store_token_tiles(ref, x):
    n = x.shape[0]
    for c in range(TOKEN_TILE_ROWS):
        ref[pl.ds(c, n, stride=TOKEN_TILE_ROWS), :] = x[:, c * LANES:(c + 1) * LANES]


def _load_token_tiles(ref, n, first=0):
    return jnp.concatenate([ref[pl.ds(first * TOKEN_TILE_ROWS + c, n, stride=TOKEN_TILE_ROWS), :]
                            for c in range(TOKEN_TILE_ROWS)], axis=1)


def _tile_copy(src, src_row, dst, dst_row, sem):
    return pltpu.make_async_copy(src.at[pl.ds(src_row, TOKEN_TILE_ROWS)],
                                 dst.at[pl.ds(dst_row, TOKEN_TILE_ROWS)], sem)


def _dispatch_kernel(p0_ref, p1_ref, pad0_ref, padn_ref, h_ref, xg_out, sem, ztile, *, td):
    def copies(t):
        src = pl.multiple_of(t * TOKEN_TILE_ROWS, TOKEN_TILE_ROWS)
        return (_tile_copy(h_ref, src, xg_out, pl.multiple_of(p0_ref[t], TOKEN_TILE_ROWS), sem),
                _tile_copy(h_ref, src, xg_out, pl.multiple_of(p1_ref[t], TOKEN_TILE_ROWS), sem))

    def issue(t, _):
        for thread, cp in enumerate(copies(t)):
            cp.start(priority=thread)
        return 0

    def drain(t, _):
        for cp in copies(t):
            cp.wait()
        return 0

    lax.fori_loop(0, td, issue, 0, unroll=DMA_ISSUE_UNROLL)
    lax.fori_loop(0, td, drain, 0, unroll=DMA_ISSUE_UNROLL)

    @pl.when(pl.program_id(0) == pl.num_programs(0) - 1)
    def _():
        ztile[...] = jnp.zeros(ztile.shape, ztile.dtype)

        chunk_rows = ztile.shape[0]
        chunk_tokens = chunk_rows // TOKEN_TILE_ROWS

        def for_each_zero_copy(action):
            def hole(e, _):
                n_chunks = padn_ref[e] // chunk_tokens

                def chunk(c, _):
                    dst = pl.multiple_of(pad0_ref[e] + c * chunk_rows, TOKEN_TILE_ROWS)
                    action(pltpu.make_async_copy(ztile, xg_out.at[pl.ds(dst, chunk_rows)], sem))
                    return 0

                def single(r, _):
                    dst = pl.multiple_of(pad0_ref[e] + r * TOKEN_TILE_ROWS, TOKEN_TILE_ROWS)
                    action(_tile_copy(ztile, 0, xg_out, dst, sem))
                    return 0

                lax.fori_loop(0, n_chunks, chunk, 0)
                lax.fori_loop(n_chunks * chunk_tokens, padn_ref[e], single, 0)
                return 0
            lax.fori_loop(0, pad0_ref.shape[0], hole, 0)

        for_each_zero_copy(lambda cp: cp.start())
        for_each_zero_copy(lambda cp: cp.wait())


def _dispatch(h1t, pos_rows, pad_rows, pad_counts, n_rows, td=256):
    T = h1t.shape[0] // TOKEN_TILE_ROWS
    smem = pl.BlockSpec((td,), lambda i: (i,), memory_space=pltpu.SMEM)
    smem_all = pl.BlockSpec(memory_space=pltpu.SMEM)
    return pl.pallas_call(
        functools.partial(_dispatch_kernel, td=td),
        grid=(T // td,),
        in_specs=[smem, smem, smem_all, smem_all, pl.BlockSpec((td * TOKEN_TILE_ROWS, LANES), lambda i: (i, 0))],
        out_specs=pl.BlockSpec(memory_space=pl.ANY),
        out_shape=jax.ShapeDtypeStruct((n_rows * TOKEN_TILE_ROWS, LANES), F32),
        scratch_shapes=[pltpu.SemaphoreType.DMA(()), pltpu.VMEM((ZERO_CHUNK_TOKENS * TOKEN_TILE_ROWS, LANES), F32)],
        compiler_params=_cparams(1),
        name="moe_dispatch",
    )(pos_rows[0], pos_rows[1], pad_rows, pad_counts, h1t)


def _ffn_kernel(be_ref, nv_ref, slot_ref, nxt_ref, x_ref, wg_hbm, wu_hbm, wd_hbm, y_ref, wstage, wb, sem,
                *, tmf, layer):
    j = pl.program_id(0)
    valid = j < nv_ref[0]

    def fetch(expert, slot):
        return [pltpu.make_async_copy(w.at[layer, expert], wstage.at[slot, k], sem.at[slot])
                for k, w in enumerate((wg_hbm, wu_hbm, wd_hbm))]

    @pl.when(j == 0)
    def _():
        for cp in fetch(be_ref[0], 0):
            cp.start()

    @pl.when(valid & ((j == 0) | (be_ref[j] != be_ref[jnp.maximum(j - 1, 0)])))
    def _():
        slot = slot_ref[j]
        for cp in fetch(be_ref[j], slot):
            cp.wait()
        for k in range(3):
            wb[k] = wstage[slot, k].astype(MXU_DTYPE)

        @pl.when(nxt_ref[j] >= 0)
        def _():
            for cp in fetch(nxt_ref[j], 1 - slot):
                cp.start()

    @pl.when(valid)
    def _():
        xb = _load_token_tiles(x_ref, tmf).astype(MXU_DTYPE)
        a = _dot(xb, wb[0])
        u = _dot(xb, wb[1])
        hmid = (a * jax.nn.sigmoid(a)) * u
        _store_token_tiles(y_ref, _dot(hmid.astype(MXU_DTYPE), wb[2]))

    @pl.when(j >= nv_ref[0])
    def _():
        y_ref[...] = jnp.zeros_like(y_ref)


def _expert_ffn(xg, blk_e, n_valid, present, wg, wu, wd, layer, tmf):
    n_rows = xg.shape[0] // TOKEN_TILE_ROWS
    D = D_MODEL
    nb = n_rows // tmf
    eids = jnp.arange(N_EXPERTS, dtype=jnp.int32)
    order = jnp.cumsum(present.astype(jnp.int32)) - 1
    later = (eids[None, :] > eids[:, None]) & present[None, :]
    nxt_e = jnp.min(jnp.where(later, eids[None, :], N_EXPERTS), axis=1)
    nxt_e = jnp.where(nxt_e == N_EXPERTS, -1, nxt_e)
    slot = (order & 1)[blk_e].astype(jnp.int32)
    nxt = nxt_e[blk_e].astype(jnp.int32)
    xmap = lambda j, be, nv, sl, nx: (jnp.minimum(j, nv[0] - 1), 0)
    anyspec = pl.BlockSpec(memory_space=pl.ANY)
    grid_spec = pltpu.PrefetchScalarGridSpec(
        num_scalar_prefetch=4,
        grid=(nb,),
        in_specs=[pl.BlockSpec((tmf * TOKEN_TILE_ROWS, LANES), xmap), anyspec, anyspec, anyspec],
        out_specs=pl.BlockSpec((tmf * TOKEN_TILE_ROWS, LANES), lambda j, be, nv, sl, nx: (j, 0)),
        scratch_shapes=[pltpu.VMEM((2, 3, D, D), F32), pltpu.VMEM((3, D, D), MXU_DTYPE),
                        pltpu.SemaphoreType.DMA((2,))],
    )
    return pl.pallas_call(
        functools.partial(_ffn_kernel, tmf=tmf, layer=layer),
        grid_spec=grid_spec,
        out_shape=jax.ShapeDtypeStruct(xg.shape, F32),
        compiler_params=_cparams(1),
        name="moe_expert_ffn",
    )(blk_e, n_valid, slot, nxt, xg, wg, wu, wd)


def _combine_kernel(p0_ref, p1_ref, gate_ref, h_ref, lng_ref, lnb_ref, y_hbm, o_ref, buf0, buf1, sem, *, tc):
    def copies(t):
        dst = pl.multiple_of(t * TOKEN_TILE_ROWS, TOKEN_TILE_ROWS)
        return (_tile_copy(y_hbm, pl.multiple_of(p0_ref[t], TOKEN_TILE_ROWS), buf0, dst, sem),
                _tile_copy(y_hbm, pl.multiple_of(p1_ref[t], TOKEN_TILE_ROWS), buf1, dst, sem))

    def issue(t, _):
        for thread, cp in enumerate(copies(t)):
            cp.start(priority=thread)
        return 0

    def drain(t, _):
        for cp in copies(t):
            cp.wait()
        return 0

    lax.fori_loop(0, tc, issue, 0, unroll=DMA_ISSUE_UNROLL)
    lax.fori_loop(0, tc, drain, 0, unroll=DMA_ISSUE_UNROLL)
    g = gate_ref[...]
    y = _load_token_tiles(buf0, tc) * g[:, 0:1] + _load_token_tiles(buf1, tc) * g[:, 1:2]
    o_ref[...] = _layer_norm(DEEPNORM_ALPHA * h_ref[...] + y, lng_ref[...], lnb_ref[...])


def _combine(yg, pos_rows, gates_col, h1, ln_g, ln_b, tc=256):
    T, D = h1.shape
    smem = pl.BlockSpec((tc,), lambda i: (i,), memory_space=pltpu.SMEM)
    rows = lambda width: pl.BlockSpec((tc, width), lambda i: (i, 0))
    vec = pl.BlockSpec((1, D), lambda i: (0, 0))
    buf = pltpu.VMEM((tc * TOKEN_TILE_ROWS, LANES), F32)
    return pl.pallas_call(
        functools.partial(_combine_kernel, tc=tc),
        grid=(T // tc,),
        in_specs=[smem, smem, rows(TOP_K), rows(D), vec, vec, pl.BlockSpec(memory_space=pl.ANY)],
        out_specs=rows(D),
        out_shape=jax.ShapeDtypeStruct((T, D), F32),
        scratch_shapes=[buf, buf, pltpu.SemaphoreType.DMA(())],
        compiler_params=_cparams(1),
        name="moe_combine_ln",
    )(pos_rows[0], pos_rows[1], gates_col, h1, ln_g.reshape(1, -1), ln_b.reshape(1, -1), yg)


def _grouped_moe(h1, h1t, experts, gates, ranks, counts, wg, wu, wd, layer, ln_g, ln_b, tmf=512):
    T = h1.shape[0]
    nb = (T * TOP_K) // tmf + N_EXPERTS
    cnt = counts[:, 0].astype(jnp.int32)
    pcnt = (cnt + tmf - 1) // tmf * tmf
    pends = jnp.cumsum(pcnt)
    pstart = pends - pcnt
    eids = jnp.arange(N_EXPERTS, dtype=jnp.int32)
    seg = jnp.sum(jnp.where(experts[None] == eids[:, None, None], pstart[:, None, None], 0), axis=0)
    pos_rows = (seg + ranks) * TOKEN_TILE_ROWS
    blk_start = jnp.arange(nb, dtype=jnp.int32) * tmf
    blk_e = jnp.minimum(jnp.sum((pends[None, :] <= blk_start[:, None]).astype(jnp.int32), axis=1), N_EXPERTS - 1)
    n_valid = (pends[-1:] // tmf).astype(jnp.int32)
    hole_start = jnp.concatenate([pstart + cnt, pends[-1:]])
    hole_rows = jnp.concatenate([pcnt - cnt, nb * tmf - pends[-1:]])
    xg = _dispatch(h1t, pos_rows, hole_start * TOKEN_TILE_ROWS, hole_rows, nb * tmf)
    yg = _expert_ffn(xg, blk_e, n_valid, cnt > 0, wg, wu, wd, layer, tmf)
    return _combine(yg, pos_rows, gates.T, h1, ln_g, ln_b)


def _rope_tables(S):
    half = HEAD_DIM // 2
    inv = ROPE_THETA ** (-jnp.arange(half, dtype=F32) / half)
    ang = jnp.arange(S, dtype=F32)[:, None] * inv[None, :]
    cos, sin = jnp.cos(ang), jnp.sin(ang)
    reps = LANES // HEAD_DIM
    cos_tab = jnp.tile(jnp.concatenate([cos, cos], axis=1), (1, reps))
    sin_tab = jnp.tile(jnp.concatenate([-sin, sin], axis=1), (1, reps))
    return cos_tab, sin_tab


def _pad_w_in(w):
    pad = jnp.zeros((D_MODEL, D_PROJ_PAD - w.shape[1]), w.dtype)
    return jnp.concatenate([w, pad], axis=1).astype(MXU_DTYPE)


def kernel(x, w_in, b_forget, g_dil, g_fox, w_out, ln1_g, ln1_b, w_router, b_router,
           w_gate, w_up, w_down, ln2_g, ln2_b):
    B, S, D = x.shape
    T = B * S
    cos_tab, sin_tab = _rope_tables(S)
    wr_hi, wr_mid, _ = _split3(w_router.astype(F32).T)
    wr3 = jnp.concatenate([wr_hi, wr_mid], axis=0)
    h = x
    for l in range(DEPTH):
        qd, kd, vd, qf, kf, vf, zf = _in_projection(h, _pad_w_in(w_in[l]), cos_tab, sin_tab)
        c = _forget_cumsum(zf, b_forget[l])[:, :N_HEADS_FOX]
        o_fox = _fox_attention(qf, kf, vf, c)
        o_dil = _dilated_attention(qd, kd, vd)
        h1, h1t, experts, gates, ranks, counts = _out_projection(
            o_dil.reshape(T, D_DIL), o_fox.reshape(T, D_FOX), h.reshape(T, D),
            w_out[l].astype(MXU_DTYPE), g_dil[l], g_fox[l], ln1_g[l], ln1_b[l], wr3, b_router)
        h2 = _grouped_moe(h1, h1t, experts, gates, ranks, counts,
                          w_gate, w_up, w_down, l,
                          ln2_g[l], ln2_b[l])
        h = h2.reshape(B, S, D)
    return h
```

```python
import functools

import numpy as np
import jax
import jax.numpy as jnp
from jax import lax
from jax.experimental import pallas as pl
from jax.experimental.pallas import tpu as pltpu

D_MODEL = 1024
DEPTH = 2
HEAD_DIM = 64
N_HEADS_DIL = 12
N_HEADS_FOX = 4
D_DIL = N_HEADS_DIL * HEAD_DIM
D_FOX = N_HEADS_FOX * HEAD_DIM
DILATIONS = (1, 4, 16)
BLOCK = 128
ROPE_THETA = 10000.0
N_EXPERTS = 16
N_GROUPS = 4
EXPERTS_PER_GROUP = 4
TOP_K = 2
DEEPNORM_ALPHA = (2.0 * DEPTH) ** 0.25
LN_EPS = 1e-5
RMS_EPS = 1e-6

LANES = 128
N_PAIRS_DIL = D_DIL // LANES
D_PROJ_PAD = 3 * D_DIL + 3 * D_FOX + LANES
VMEM_LIMIT = 48 * 1024 * 1024
TOKEN_TILE_ROWS = D_MODEL // LANES
DMA_ISSUE_UNROLL = 8
ZERO_CHUNK_TOKENS = 32
SKEW_DIL = 16
SKEW_PITCH = SKEW_DIL + 1
DIL_PAD = max(d for d in DILATIONS if d != SKEW_DIL) * BLOCK
SKEW_PAD = SKEW_PITCH * BLOCK
MASK_BIAS = -1e30
LOG2_E = 1.4426950408889634
DIL_UNITS_PER_BODY = 16
FOX_TK = 256
FOX_BLOCKS_PER_BODY = 4
ZF_ROWS = 8

MXU_DTYPE = jnp.bfloat16
F32 = jnp.float32
NEG_INF = float("-inf")


def _cparams(n_axes):
    return pltpu.CompilerParams(dimension_semantics=("arbitrary",) * n_axes,
                                vmem_limit_bytes=VMEM_LIMIT)


def _dot(a, b):
    return jnp.dot(a, b, preferred_element_type=F32)


def _dot_nt(a, b):
    return lax.dot_general(a, b, (((1,), (1,)), ((), ())), preferred_element_type=F32)


def _inproj_kernel(x_ref, w_ref, cos_ref, sin_ref,
                   qd_ref, kd_ref, vd_ref, qf_ref, kf_ref, vf_ref, zf_ref):
    xb = x_ref[0].astype(MXU_DTYPE)
    tm = xb.shape[0]
    cos = cos_ref[...]
    sin = sin_ref[...]
    lane = lax.broadcasted_iota(jnp.int32, (tm, LANES), 1)
    first_half = (lane % HEAD_DIM) < (HEAD_DIM // 2)

    def rope(z):
        rot = jnp.where(first_half, pltpu.roll(z, LANES - 32, 1), pltpu.roll(z, 32, 1))
        return z * cos + rot * sin

    def proj(col, width):
        return _dot(xb, w_ref[:, col:col + width])

    scale = HEAD_DIM ** -0.5
    col = 0
    for out_ref, roped, mul in ((qd_ref, True, scale * LOG2_E), (kd_ref, True, 1.0)):
        for c in range(D_DIL // 256):
            z = proj(col, 256)
            for half in range(2):
                zz = rope(z[:, half * LANES:(half + 1) * LANES]) * mul
                out_ref[0, :, c * 256 + half * LANES:c * 256 + (half + 1) * LANES] = zz.astype(out_ref.dtype)
            col += 256
    for c in range(D_DIL // 256):
        vd_ref[0, :, c * 256:(c + 1) * 256] = proj(col, 256).astype(vd_ref.dtype)
        col += 256
    qf_ref[0] = (proj(col, D_FOX) * (scale * LOG2_E)).astype(qf_ref.dtype)
    col += D_FOX
    kf_ref[0] = proj(col, D_FOX).astype(kf_ref.dtype)
    col += D_FOX
    vf_ref[0] = proj(col, D_FOX).astype(vf_ref.dtype)
    col += D_FOX
    zf_ref[0] = proj(col, LANES).T[0:ZF_ROWS, :]


def _in_projection(h, w_pad, cos_tab, sin_tab, tm=1024):
    B, S, D = h.shape
    grid = (B, S // tm)
    row = lambda width: pl.BlockSpec((1, tm, width), lambda b, i: (b, i, 0))
    out_shape = (
        jax.ShapeDtypeStruct((B, S, D_DIL), MXU_DTYPE),
        jax.ShapeDtypeStruct((B, S, D_DIL), MXU_DTYPE),
        jax.ShapeDtypeStruct((B, S, D_DIL), MXU_DTYPE),
        jax.ShapeDtypeStruct((B, S, D_FOX), MXU_DTYPE),
        jax.ShapeDtypeStruct((B, S, D_FOX), MXU_DTYPE),
        jax.ShapeDtypeStruct((B, S, D_FOX), MXU_DTYPE),
        jax.ShapeDtypeStruct((B, ZF_ROWS, S), F32),
    )
    return pl.pallas_call(
        _inproj_kernel,
        grid=grid,
        in_specs=[
            row(D),
            pl.BlockSpec((D, D_PROJ_PAD), lambda b, i: (0, 0)),
            pl.BlockSpec((tm, LANES), lambda b, i: (i, 0)),
            pl.BlockSpec((tm, LANES), lambda b, i: (i, 0)),
        ],
        out_specs=(row(D_DIL), row(D_DIL), row(D_DIL), row(D_FOX), row(D_FOX), row(D_FOX),
                   pl.BlockSpec((1, ZF_ROWS, tm), lambda b, i: (b, 0, i))),
        out_shape=out_shape,
        compiler_params=_cparams(2),
        name="in_projection",
    )(h, w_pad, cos_tab, sin_tab)


def _split3(x):
    hi = x.astype(jnp.bfloat16)
    r1 = x - hi.astype(F32)
    mid = r1.astype(jnp.bfloat16)
    lo = (r1 - mid.astype(F32)).astype(jnp.bfloat16)
    return hi, mid, lo


def _forget_cumsum_kernel(z_ref, b_ref, tri_ref, c_ref):
    x = z_ref[0] + b_ref[...]
    logf = jnp.minimum(x, 0.0) - jnp.log1p(jnp.exp(-jnp.abs(x)))
    tri = tri_ref[...]
    S = x.shape[1]
    carry = jnp.zeros((x.shape[0], 1), F32)
    for blk in range(S // LANES):
        seg = logf[:, blk * LANES:(blk + 1) * LANES]
        hi, mid, lo = _split3(seg)
        cs = (_dot(hi, tri) + _dot(mid, tri)) + _dot(lo, tri) + carry
        c_ref[0, :, blk * LANES:(blk + 1) * LANES] = cs * LOG2_E
        carry = cs[:, LANES - 1:LANES]


def _forget_cumsum(zt, b_forget):
    B, Hf, S = zt.shape
    tri = (np.arange(LANES)[:, None] <= np.arange(LANES)[None, :]).astype(np.float32)
    return pl.pallas_call(
        _forget_cumsum_kernel,
        grid=(B,),
        in_specs=[
            pl.BlockSpec((1, Hf, S), lambda b: (b, 0, 0)),
            pl.BlockSpec((Hf, 1), lambda b: (0, 0)),
            pl.BlockSpec((LANES, LANES), lambda b: (0, 0)),
        ],
        out_specs=pl.BlockSpec((1, Hf, S), lambda b: (b, 0, 0)),
        out_shape=jax.ShapeDtypeStruct((B, Hf, S), F32),
        compiler_params=_cparams(1),
        name="forget_cumsum",
    )(zt, jnp.pad(b_forget.astype(F32), (0, Hf - b_forget.shape[0])).reshape(Hf, 1), jnp.asarray(tri, jnp.bfloat16))


def _pair_rows(xp, is_a):
    zero = jnp.zeros_like(xp)
    return jnp.concatenate([jnp.where(is_a, xp, zero), jnp.where(is_a, zero, xp)], axis=0)


def _causal_bias():
    qi = np.arange(BLOCK)[:, None]
    ki = np.arange(FOX_TK)[None, :]
    vis = np.stack([ki <= qi, ki <= qi + BLOCK])
    return np.where(vis, 0.0, MASK_BIAS).astype(np.float32)


def _fox_kernel(q_ref, k_ref, v_ref, crow_ref, cbias_ref, o_ref):
    S = q_ref.shape[1]
    tk = FOX_TK
    lane = lax.broadcasted_iota(jnp.int32, (BLOCK, LANES), 1)
    is_a = lane < HEAD_DIM
    ones = jnp.ones((tk, LANES), MXU_DTYPE)

    def unit(q2, kv, bias):
        kb, v_ones, cr = kv
        s = _dot_nt(q2, kb)
        sa = s[:BLOCK] - cr[0:1]
        sb = s[BLOCK:] - cr[1:2]
        if bias is not None:
            sa = sa + bias
            sb = sb + bias
        s = jnp.concatenate([sa, sb], axis=0)
        m = jnp.max(s, axis=1, keepdims=True)
        p = jnp.exp2(s - m).astype(MXU_DTYPE)
        o = _dot(p, v_ones)
        acc = jnp.where(is_a, o[:BLOCK, :LANES], o[BLOCK:, :LANES])
        l = jnp.where(is_a, o[:BLOCK, LANES:], o[BLOCK:, LANES:])
        return acc, l, jnp.where(is_a, m[:BLOCK], m[BLOCK:])

    def key_block(j):
        k0 = pl.multiple_of(j * tk, tk)
        return (k_ref[0, pl.ds(k0, tk), :],
                jnp.concatenate([v_ref[0, pl.ds(k0, tk), :], ones], axis=1),
                crow_ref[0, 0, :, pl.ds(k0, tk)])

    def fold(state, units):
        num, den, top_old = state
        top = functools.reduce(jnp.maximum, [top_old] + [m for _, _, m in units])
        scale = jnp.exp2(top_old - top)
        num, den = scale * num, scale * den
        for acc, l, m in units:
            w = jnp.exp2(m - top)
            num = num + w * acc
            den = den + w * l
        return num, den, top

    def q_block(i, _):
        r0 = pl.multiple_of(i * tk, tk)
        q2 = [_pair_rows(q_ref[0, pl.ds(r0 + h * BLOCK, BLOCK), :], is_a) for h in range(2)]
        zero = jnp.zeros((BLOCK, LANES), F32)
        empty = (zero, zero, jnp.full((BLOCK, LANES), MASK_BIAS, F32))

        def fold_blocks(states, blocks):
            units = [[unit(q2[h], kv, None if bias is None else bias[h]) for h in range(2)]
                     for kv, bias in ((key_block(j), bias) for j, bias in blocks)]
            return tuple(fold(states[h], [u[h] for u in units]) for h in range(2))

        def full_blocks(g, states):
            return fold_blocks(states, [(FOX_BLOCKS_PER_BODY * g + u, None) for u in range(FOX_BLOCKS_PER_BODY)])

        states = lax.fori_loop(0, i // FOX_BLOCKS_PER_BODY, full_blocks, (empty, empty))
        diag = (i, (cbias_ref[0], cbias_ref[1]))

        def tail(n_left):
            return lambda st: fold_blocks(st, [(i - n_left + u, None) for u in range(n_left)] + [diag])

        states = lax.switch(i % FOX_BLOCKS_PER_BODY, [tail(r) for r in range(FOX_BLOCKS_PER_BODY)], states)
        for h in range(2):
            num, den, _ = states[h]
            o_ref[0, pl.ds(r0 + h * BLOCK, BLOCK), :] = (num / den).astype(o_ref.dtype)
        return 0

    lax.fori_loop(0, S // tk, q_block, 0)


def _fox_attention(qf, kf, vf, c2):
    B, S, _ = qf.shape
    n_pairs = D_FOX // LANES
    c4 = c2.reshape(B, n_pairs, 2, S)
    blk = pl.BlockSpec((1, S, LANES), lambda b, p: (b, 0, p))
    return pl.pallas_call(
        _fox_kernel,
        grid=(B, n_pairs),
        in_specs=[blk, blk, blk,
                  pl.BlockSpec((1, 1, 2, S), lambda b, p: (b, p, 0, 0)),
                  pl.BlockSpec((2, BLOCK, FOX_TK), lambda b, p: (0, 0, 0))],
        out_specs=blk,
        out_shape=jax.ShapeDtypeStruct((B, S, D_FOX), MXU_DTYPE),
        compiler_params=_cparams(2),
        name="fox_attention",
    )(qf, kf, vf, c4, jnp.asarray(_causal_bias()))


def _band_bias():
    qi = np.arange(BLOCK)[:, None]
    ki = np.arange(2 * BLOCK)[None, :]
    delta = BLOCK + qi - ki
    band = (delta >= 0) & (delta <= BLOCK)
    first = band & (ki >= BLOCK)
    return np.where(np.stack([first, band]), 0.0, MASK_BIAS).astype(np.float32)


def _dil_kernel(q_ref, k_ref, v_ref, bias_ref, o_ref, q32, k32, v32, q17, k17, v17, onorm_s, mu_s):
    S = q_ref.shape[1]
    q32[...] = q_ref[0].astype(F32)
    zeros = jnp.zeros((SKEW_PAD, LANES), F32)
    for plain, skewed, src in ((k32, k17, k_ref), (v32, v17, v_ref)):
        plain[0:DIL_PAD, :] = zeros[0:DIL_PAD]
        skewed[0:SKEW_PAD, :] = zeros
        plain[DIL_PAD:, :] = src[0].astype(F32)

    def skew_rows(g8, _):
        for u in range(8):
            src = pl.multiple_of(g8 * (8 * SKEW_DIL), 8 * SKEW_DIL) + u * SKEW_DIL
            dst = pl.multiple_of(g8 * (8 * SKEW_PITCH), 8) + u * SKEW_PITCH
            q17[pl.ds(dst, SKEW_DIL), :] = q32[pl.ds(src, SKEW_DIL), :]
            k17[pl.ds(SKEW_PAD + dst, SKEW_DIL), :] = k32[pl.ds(DIL_PAD + src, SKEW_DIL), :]
            v17[pl.ds(SKEW_PAD + dst, SKEW_DIL), :] = v32[pl.ds(DIL_PAD + src, SKEW_DIL), :]
        return 0

    lax.fori_loop(0, S // (8 * SKEW_DIL), skew_rows, 0)

    lane = lax.broadcasted_iota(jnp.int32, (BLOCK, LANES), 1)
    is_a = lane < HEAD_DIM
    ones = jnp.ones((2 * BLOCK, LANES), MXU_DTYPE)

    def rows(ref, start, count, stride):
        if stride == 1:
            return ref[pl.ds(start, count), :]
        return ref[pl.ds(start, count, stride=stride), :]

    def unit(dil, rho, n):
        if dil == SKEW_DIL:
            (qs, ks, vs), stride, pad = (q17, k17, v17), SKEW_PITCH, SKEW_PAD
        else:
            (qs, ks, vs), stride, pad = (q32, k32, v32), dil, DIL_PAD
        q0 = rho + stride * BLOCK * n
        qb = rows(qs, q0, BLOCK, stride).astype(MXU_DTYPE)
        kw = rows(ks, q0 + pad - stride * BLOCK, 2 * BLOCK, stride).astype(MXU_DTYPE)
        vw = rows(vs, q0 + pad - stride * BLOCK, 2 * BLOCK, stride).astype(MXU_DTYPE)
        bias = bias_ref[jnp.minimum(n, 1)]
        s = _dot_nt(_pair_rows(qb, is_a), kw)
        s = jnp.concatenate([s[:BLOCK] + bias, s[BLOCK:] + bias], axis=0)
        m = jnp.max(s, axis=1, keepdims=True)
        p = jnp.exp2(s - m).astype(MXU_DTYPE)
        o = _dot(p, jnp.concatenate([vw, ones], axis=1))
        acc = jnp.where(is_a, o[:BLOCK, :LANES], o[BLOCK:, :LANES])
        l = jnp.where(is_a, o[:BLOCK, LANES:], o[BLOCK:, LANES:])
        mu = jnp.where(is_a, m[:BLOCK], m[BLOCK:]) + jnp.log2(l)
        return acc / l, mu

    n_units = S // BLOCK

    for slab, dil in enumerate(d for d in DILATIONS if d != 1):
        blk_bits = (n_units // dil).bit_length() - 1

        def several_units(g, _, slab=slab, dil=dil, blk_bits=blk_bits):
            for u in range(DIL_UNITS_PER_BODY):
                t = g * DIL_UNITS_PER_BODY + u
                rho = lax.shift_right_logical(t, blk_bits)
                n = t & ((1 << blk_bits) - 1)
                o_n, mu = unit(dil, rho, n)
                q0 = rho + dil * BLOCK * n
                onorm_s[slab, pl.ds(q0, BLOCK, stride=dil), :] = o_n
                mu_s[slab, pl.ds(q0, BLOCK, stride=dil), :] = mu
            return 0

        lax.fori_loop(0, n_units // DIL_UNITS_PER_BODY, several_units, 0)

    def merge_blocks(g, _):
        for u in range(DIL_UNITS_PER_BODY):
            n = g * DIL_UNITS_PER_BODY + u
            o_1, mu_1 = unit(1, 0, n)
            r0 = pl.multiple_of(n * BLOCK, BLOCK)
            others = [(onorm_s[sl, pl.ds(r0, BLOCK), :], mu_s[sl, pl.ds(r0, BLOCK), :])
                      for sl in range(len(DILATIONS) - 1)]
            top = functools.reduce(jnp.maximum, [mu_1] + [mu for _, mu in others])
            w = jnp.exp2(mu_1 - top)
            num, den = w * o_1, w
            for o_p, mu_p in others:
                w = jnp.exp2(mu_p - top)
                num = num + w * o_p
                den = den + w
            o_ref[0, pl.ds(r0, BLOCK), :] = (num / den).astype(o_ref.dtype)
        return 0

    lax.fori_loop(0, n_units // DIL_UNITS_PER_BODY, merge_blocks, 0)


def _dilated_attention(qd, kd, vd):
    B, S, _ = qd.shape
    blk = pl.BlockSpec((1, S, LANES), lambda b, p: (b, 0, p))
    n_slabs = len(DILATIONS) - 1
    skewed = S // SKEW_DIL * SKEW_PITCH
    return pl.pallas_call(
        _dil_kernel,
        grid=(B, N_PAIRS_DIL),
        in_specs=[blk, blk, blk, pl.BlockSpec((2, BLOCK, 2 * BLOCK), lambda b, p: (0, 0, 0))],
        out_specs=blk,
        out_shape=jax.ShapeDtypeStruct((B, S, D_DIL), MXU_DTYPE),
        scratch_shapes=[
            pltpu.VMEM((S, LANES), F32),
            pltpu.VMEM((DIL_PAD + S, LANES), F32),
            pltpu.VMEM((DIL_PAD + S, LANES), F32),
            pltpu.VMEM((skewed, LANES), F32),
            pltpu.VMEM((SKEW_PAD + skewed, LANES), F32),
            pltpu.VMEM((SKEW_PAD + skewed, LANES), F32),
            pltpu.VMEM((n_slabs, S, LANES), F32),
            pltpu.VMEM((n_slabs, S, LANES), F32),
        ],
        compiler_params=_cparams(2),
        name="dilated_attention",
    )(qd, kd, vd, jnp.asarray(_band_bias()))


def _layer_norm(u, g, b):
    mu = jnp.mean(u, axis=1, keepdims=True)
    d = u - mu
    var = jnp.mean(d * d, axis=1, keepdims=True)
    return d * lax.rsqrt(var + LN_EPS) * g + b


def _rms_norm(x, g):
    ms = jnp.mean(x * x, axis=1, keepdims=True)
    return x * lax.rsqrt(ms + RMS_EPS) * g


def _top2_of4(vals):
    v1, i1 = vals[0], jnp.zeros(vals[0].shape, jnp.int32)
    for i in range(1, 4):
        better = vals[i] > v1
        v1 = jnp.where(better, vals[i], v1)
        i1 = jnp.where(better, i, i1)
    v2 = jnp.full(vals[0].shape, -1.0, F32)
    i2 = jnp.zeros(vals[0].shape, jnp.int32)
    for i in range(4):
        better = (vals[i] > v2) & (i1 != i)
        v2 = jnp.where(better, vals[i], v2)
        i2 = jnp.where(better, i, i2)
    return v1, i1, v2, i2


def _outproj_kernel(od_ref, of_ref, h_ref, wo_ref, gd_ref, gf_ref, lng_ref, lnb_ref,
                    wr_ref, br_ref, tri_ref,
                    h1_ref, h1t_ref, e_ref, gate_ref, rank_ref, cnt_ref, base_ref):
    step = pl.program_id(0)

    @pl.when(step == 0)
    def _():
        base_ref[...] = jnp.zeros_like(base_ref)

    xd = _rms_norm(od_ref[...].astype(F32), gd_ref[...])
    xf = _rms_norm(of_ref[...].astype(F32), gf_ref[...])
    y = _dot(xd.astype(MXU_DTYPE), wo_ref[0:D_DIL, :]) + _dot(xf.astype(MXU_DTYPE), wo_ref[D_DIL:, :])
    h1 = _layer_norm(DEEPNORM_ALPHA * h_ref[...] + y, lng_ref[...], lnb_ref[...])
    h1_ref[...] = h1
    _store_token_tiles(h1t_ref, h1)
    tm = h1.shape[0]

    h_hi, h_mid, _ = _split3(h1)
    two = _dot_nt(wr_ref[...], h_hi)
    logits = (two[:N_EXPERTS] + two[N_EXPERTS:]) + _dot_nt(wr_ref[0:N_EXPERTS, :], h_mid) + br_ref[...]
    logits = logits - jnp.max(logits, axis=0, keepdims=True)
    ex = jnp.exp(logits)
    probs = ex / jnp.sum(ex, axis=0, keepdims=True)
    pr = [probs[j:j + 1, :] for j in range(N_EXPERTS)]

    def group_score(g):
        v = pr[4 * g:4 * g + 4]
        pairs = [v[a] + v[b] for a in range(4) for b in range(a + 1, 4)]
        return functools.reduce(jnp.maximum, pairs)

    best = group_score(0)
    gsel = jnp.zeros((1, tm), jnp.int32)
    for g in range(1, N_GROUPS):
        sc = group_score(g)
        better = sc > best
        best = jnp.where(better, sc, best)
        gsel = jnp.where(better, g, gsel)
    in_grp = []
    for i in range(EXPERTS_PER_GROUP):
        v = pr[i]
        for g in range(1, N_GROUPS):
            v = jnp.where(gsel == g, pr[4 * g + i], v)
        in_grp.append(v)
    v1, i1, v2, i2 = _top2_of4(in_grp)
    e1 = gsel * EXPERTS_PER_GROUP + i1
    e2 = gsel * EXPERTS_PER_GROUP + i2
    den = v1 + v2
    e_ref[...] = jnp.concatenate([e1, e2], axis=0)
    gate_ref[...] = jnp.concatenate([v1 / den, v2 / den], axis=0)

    eidx = lax.broadcasted_iota(jnp.int32, (N_EXPERTS, tm), 0)
    oh1 = (eidx == e1).astype(F32)
    oh2 = (eidx == e2).astype(F32)
    tot = oh1 + oh2
    before = base_ref[...] + _dot(tot.astype(jnp.bfloat16), tri_ref[...])
    r1 = jnp.sum(oh1 * before, axis=0, keepdims=True)
    r2 = jnp.sum(oh2 * before, axis=0, keepdims=True)
    rank_ref[...] = jnp.concatenate([r1, r2], axis=0).astype(jnp.int32)
    base_ref[...] = base_ref[...] + jnp.sum(tot, axis=1, keepdims=True)
    cnt_ref[...] = jnp.broadcast_to(base_ref[...], cnt_ref.shape)


def _out_projection(od, of, h, w_out, g_dil, g_fox, ln_g, ln_b, wr3, b_router, tm=1024):
    T = h.shape[0]
    tri = (np.arange(tm)[:, None] < np.arange(tm)[None, :]).astype(np.float32)
    rows = lambda width: pl.BlockSpec((tm, width), lambda i: (i, 0))
    full = lambda a: pl.BlockSpec(a.shape, lambda i: (0,) * a.ndim)
    tok = pl.BlockSpec((TOP_K, tm), lambda i: (0, i))
    consts = [w_out, g_dil.reshape(1, -1), g_fox.reshape(1, -1), ln_g.reshape(1, -1), ln_b.reshape(1, -1),
              wr3, b_router.reshape(-1, 1).astype(F32), jnp.asarray(tri, jnp.bfloat16)]
    return pl.pallas_call(
        _outproj_kernel,
        grid=(T // tm,),
        in_specs=[rows(D_DIL), rows(D_FOX), rows(D_MODEL)] + [full(a) for a in consts],
        out_specs=(rows(D_MODEL), pl.BlockSpec((tm * TOKEN_TILE_ROWS, LANES), lambda i: (i, 0)),
                   tok, tok, tok, pl.BlockSpec((N_EXPERTS, LANES), lambda i: (0, 0))),
        out_shape=(
            jax.ShapeDtypeStruct((T, D_MODEL), F32),
            jax.ShapeDtypeStruct((T * TOKEN_TILE_ROWS, LANES), F32),
            jax.ShapeDtypeStruct((TOP_K, T), jnp.int32),
            jax.ShapeDtypeStruct((TOP_K, T), F32),
            jax.ShapeDtypeStruct((TOP_K, T), jnp.int32),
            jax.ShapeDtypeStruct((N_EXPERTS, LANES), F32),
        ),
        scratch_shapes=[pltpu.VMEM((N_EXPERTS, 1), F32)],
        compiler_params=_cparams(1),
        name="out_projection_router",
    )(od, of, h, *consts)


def _store_token_tiles(ref, x):
    n = x.shape[0]
    for c in range(TOKEN_TILE_ROWS):
        ref[pl.ds(c, n, stride=TOKEN_TILE_ROWS), :] = x[:, c * LANES:(c + 1) * LANES]


def _load_token_tiles(ref, n, first=0):
    return jnp.concatenate([ref[pl.ds(first * TOKEN_TILE_ROWS + c, n, stride=TOKEN_TILE_ROWS), :]
                            for c in range(TOKEN_TILE_ROWS)], axis=1)


def _tile_copy(src, src_row, dst, dst_row, sem):
    return pltpu.make_async_copy(src.at[pl.ds(src_row, TOKEN_TILE_ROWS)],
                                 dst.at[pl.ds(dst_row, TOKEN_TILE_ROWS)], sem)


def _dispatch_kernel(p0_ref, p1_ref, pad0_ref, padn_ref, h_ref, xg_out, sem, ztile, *, td):
    def copies(t):
        src = pl.multiple_of(t * TOKEN_TILE_ROWS, TOKEN_TILE_ROWS)
        return (_tile_copy(h_ref, src, xg_out, pl.multiple_of(p0_ref[t], TOKEN_TILE_ROWS), sem),
                _tile_copy(h_ref, src, xg_out, pl.multiple_of(p1_ref[t], TOKEN_TILE_ROWS), sem))

    def issue(t, _):
        for thread, cp in enumerate(copies(t)):
            cp.start(priority=thread)
        return 0

    def drain(t, _):
        for cp in copies(t):
            cp.wait()
        return 0

    lax.fori_loop(0, td, issue, 0, unroll=DMA_ISSUE_UNROLL)
    lax.fori_loop(0, td, drain, 0, unroll=DMA_ISSUE_UNROLL)

    @pl.when(pl.program_id(0) == pl.num_programs(0) - 1)
    def _():
        ztile[...] = jnp.zeros(ztile.shape, ztile.dtype)

        chunk_rows = ztile.shape[0]
        chunk_tokens = chunk_rows // TOKEN_TILE_ROWS

        def for_each_zero_copy(action):
            def hole(e, _):
                n_chunks = padn_ref[e] // chunk_tokens

                def chunk(c, _):
                    dst = pl.multiple_of(pad0_ref[e] + c * chunk_rows, TOKEN_TILE_ROWS)
                    action(pltpu.make_async_copy(ztile, xg_out.at[pl.ds(dst, chunk_rows)], sem))
                    return 0

                def single(r, _):
                    dst = pl.multiple_of(pad0_ref[e] + r * TOKEN_TILE_ROWS, TOKEN_TILE_ROWS)
                    action(_tile_copy(ztile, 0, xg_out, dst, sem))
                    return 0

                lax.fori_loop(0, n_chunks, chunk, 0)
                lax.fori_loop(n_chunks * chunk_tokens, padn_ref[e], single, 0)
                return 0
            lax.fori_loop(0, pad0_ref.shape[0], hole, 0)

        for_each_zero_copy(lambda cp: cp.start())
        for_each_zero_copy(lambda cp: cp.wait())


def _dispatch(h1t, pos_rows, pad_rows, pad_counts, n_rows, td=256):
    T = h1t.shape[0] // TOKEN_TILE_ROWS
    smem = pl.BlockSpec((td,), lambda i: (i,), memory_space=pltpu.SMEM)
    smem_all = pl.BlockSpec(memory_space=pltpu.SMEM)
    return pl.pallas_call(
        functools.partial(_dispatch_kernel, td=td),
        grid=(T // td,),
        in_specs=[smem, smem, smem_all, smem_all, pl.BlockSpec((td * TOKEN_TILE_ROWS, LANES), lambda i: (i, 0))],
        out_specs=pl.BlockSpec(memory_space=pl.ANY),
        out_shape=jax.ShapeDtypeStruct((n_rows * TOKEN_TILE_ROWS, LANES), F32),
        scratch_shapes=[pltpu.SemaphoreType.DMA(()), pltpu.VMEM((ZERO_CHUNK_TOKENS * TOKEN_TILE_ROWS, LANES), F32)],
        compiler_params=_cparams(1),
        name="moe_dispatch",
    )(pos_rows[0], pos_rows[1], pad_rows, pad_counts, h1t)


def _ffn_kernel(be_ref, nv_ref, slot_ref, nxt_ref, x_ref, wg_hbm, wu_hbm, wd_hbm, y_ref, wstage, wb, sem,
                *, tmf, layer):
    j = pl.program_id(0)
    valid = j < nv_ref[0]

    def fetch(expert, slot):
        return [pltpu.make_async_copy(w.at[layer, expert], wstage.at[slot, k], sem.at[slot])
                for k, w in enumerate((wg_hbm, wu_hbm, wd_hbm))]

    @pl.when(j == 0)
    def _():
        for cp in fetch(be_ref[0], 0):
            cp.start()

    @pl.when(valid & ((j == 0) | (be_ref[j] != be_ref[jnp.maximum(j - 1, 0)])))
    def _():
        slot = slot_ref[j]
        for cp in fetch(be_ref[j], slot):
            cp.wait()
        for k in range(3):
            wb[k] = wstage[slot, k].astype(MXU_DTYPE)

        @pl.when(nxt_ref[j] >= 0)
        def _():
            for cp in fetch(nxt_ref[j], 1 - slot):
                cp.start()

    @pl.when(valid)
    def _():
        xb = _load_token_tiles(x_ref, tmf).astype(MXU_DTYPE)
        a = _dot(xb, wb[0])
        u = _dot(xb, wb[1])
        hmid = (a * jax.nn.sigmoid(a)) * u
        _store_token_tiles(y_ref, _dot(hmid.astype(MXU_DTYPE), wb[2]))

    @pl.when(j >= nv_ref[0])
    def _():
        y_ref[...] = jnp.zeros_like(y_ref)


def _expert_ffn(xg, blk_e, n_valid, present, wg, wu, wd, layer, tmf):
    n_rows = xg.shape[0] // TOKEN_TILE_ROWS
    D = D_MODEL
    nb = n_rows // tmf
    eids = jnp.arange(N_EXPERTS, dtype=jnp.int32)
    order = jnp.cumsum(present.astype(jnp.int32)) - 1
    later = (eids[None, :] > eids[:, None]) & present[None, :]
    nxt_e = jnp.min(jnp.where(later, eids[None, :], N_EXPERTS), axis=1)
    nxt_e = jnp.where(nxt_e == N_EXPERTS, -1, nxt_e)
    slot = (order & 1)[blk_e].astype(jnp.int32)
    nxt = nxt_e[blk_e].astype(jnp.int32)
    xmap = lambda j, be, nv, sl, nx: (jnp.minimum(j, nv[0] - 1), 0)
    anyspec = pl.BlockSpec(memory_space=pl.ANY)
    grid_spec = pltpu.PrefetchScalarGridSpec(
        num_scalar_prefetch=4,
        grid=(nb,),
        in_specs=[pl.BlockSpec((tmf * TOKEN_TILE_ROWS, LANES), xmap), anyspec, anyspec, anyspec],
        out_specs=pl.BlockSpec((tmf * TOKEN_TILE_ROWS, LANES), lambda j, be, nv, sl, nx: (j, 0)),
        scratch_shapes=[pltpu.VMEM((2, 3, D, D), F32), pltpu.VMEM((3, D, D), MXU_DTYPE),
                        pltpu.SemaphoreType.DMA((2,))],
    )
    return pl.pallas_call(
        functools.partial(_ffn_kernel, tmf=tmf, layer=layer),
        grid_spec=grid_spec,
        out_shape=jax.ShapeDtypeStruct(xg.shape, F32),
        compiler_params=_cparams(1),
        name="moe_expert_ffn",
    )(blk_e, n_valid, slot, nxt, xg, wg, wu, wd)


def _combine_kernel(pc0_ref, pc1_ref, pn0_ref, pn1_ref, gate_ref, h_ref, lng_ref, lnb_ref, y_hbm, o_ref,
                    buf_a0, buf_a1, buf_b0, buf_b1, sem, *, tc):
    j = pl.program_id(0)

    def tile(p0_ref, p1_ref, first, bufs, which):
        def copies(t):
            dst = pl.multiple_of(t * TOKEN_TILE_ROWS, TOKEN_TILE_ROWS)
            return (_tile_copy(y_hbm, pl.multiple_of(p0_ref[first + t], TOKEN_TILE_ROWS), bufs[0], dst, sem.at[which]),
                    _tile_copy(y_hbm, pl.multiple_of(p1_ref[first + t], TOKEN_TILE_ROWS), bufs[1], dst, sem.at[which]))

        def issue(t, _):
            for thread, cp in enumerate(copies(t)):
                cp.start(priority=thread)
            return 0

        def drain(t, _):
            for cp in copies(t):
                cp.wait()
            return 0

        def issue_all():
            lax.fori_loop(0, tc, issue, 0, unroll=DMA_ISSUE_UNROLL)

        def wait_all():
            lax.fori_loop(0, tc, drain, 0, unroll=DMA_ISSUE_UNROLL)

        return issue_all, wait_all

    def finish(first, bufs):
        g = gate_ref[first:first + tc, :]
        y = _load_token_tiles(bufs[0], tc) * g[:, 0:1] + _load_token_tiles(bufs[1], tc) * g[:, 1:2]
        o_ref[first:first + tc, :] = _layer_norm(DEEPNORM_ALPHA * h_ref[first:first + tc, :] + y,
                                                 lng_ref[...], lnb_ref[...])

    issue_a, wait_a = tile(pc0_ref, pc1_ref, 0, (buf_a0, buf_a1), 0)
    issue_b, wait_b = tile(pc0_ref, pc1_ref, tc, (buf_b0, buf_b1), 1)
    issue_next_a, _ = tile(pn0_ref, pn1_ref, 0, (buf_a0, buf_a1), 0)

    pl.when(j == 0)(issue_a)
    issue_b()
    wait_a()
    finish(0, (buf_a0, buf_a1))
    pl.when(j + 1 < pl.num_programs(0))(issue_next_a)
    wait_b()
    finish(tc, (buf_b0, buf_b1))


def _combine(yg, pos_rows, gates_col, h1, ln_g, ln_b, tc=256):
    T, D = h1.shape
    n_tiles = T // tc
    cur = pl.BlockSpec((2 * tc,), lambda j: (j,), memory_space=pltpu.SMEM)
    nxt = pl.BlockSpec((tc,), lambda j: (jnp.minimum(2 * j + 2, n_tiles - 1),), memory_space=pltpu.SMEM)
    rows = lambda width: pl.BlockSpec((2 * tc, width), lambda j: (j, 0))
    vec = pl.BlockSpec((1, D), lambda j: (0, 0))
    buf = pltpu.VMEM((tc * TOKEN_TILE_ROWS, LANES), F32)
    return pl.pallas_call(
        functools.partial(_combine_kernel, tc=tc),
        grid=(n_tiles // 2,),
        in_specs=[cur, cur, nxt, nxt, rows(TOP_K), rows(D), vec, vec, pl.BlockSpec(memory_space=pl.ANY)],
        out_specs=rows(D),
        out_shape=jax.ShapeDtypeStruct((T, D), F32),
        scratch_shapes=[buf, buf, buf, buf, pltpu.SemaphoreType.DMA((2,))],
        compiler_params=_cparams(1),
        name="moe_combine_ln",
    )(pos_rows[0], pos_rows[1], pos_rows[0], pos_rows[1], gates_col, h1,
      ln_g.reshape(1, -1), ln_b.reshape(1, -1), yg)


def _grouped_moe(h1, h1t, experts, gates, ranks, counts, wg, wu, wd, layer, ln_g, ln_b, tmf=512):
    T = h1.shape[0]
    nb = (T * TOP_K) // tmf + N_EXPERTS
    cnt = counts[:, 0].astype(jnp.int32)
    pcnt = (cnt + tmf - 1) // tmf * tmf
    pends = jnp.cumsum(pcnt)
    pstart = pends - pcnt
    eids = jnp.arange(N_EXPERTS, dtype=jnp.int32)
    seg = jnp.sum(jnp.where(experts[None] == eids[:, None, None], pstart[:, None, None], 0), axis=0)
    pos_rows = (seg + ranks) * TOKEN_TILE_ROWS
    blk_start = jnp.arange(nb, dtype=jnp.int32) * tmf
    blk_e = jnp.minimum(jnp.sum((pends[None, :] <= blk_start[:, None]).astype(jnp.int32), axis=1), N_EXPERTS - 1)
    n_valid = (pends[-1:] // tmf).astype(jnp.int32)
    hole_start = jnp.concatenate([pstart + cnt, pends[-1:]])
    hole_rows = jnp.concatenate([pcnt - cnt, nb * tmf - pends[-1:]])
    xg = _dispatch(h1t, pos_rows, hole_start * TOKEN_TILE_ROWS, hole_rows, nb * tmf)
    yg = _expert_ffn(xg, blk_e, n_valid, cnt > 0, wg, wu, wd, layer, tmf)
    return _combine(yg, pos_rows, gates.T, h1, ln_g, ln_b)


def _rope_tables(S):
    half = HEAD_DIM // 2
    inv = ROPE_THETA ** (-jnp.arange(half, dtype=F32) / half)
    ang = jnp.arange(S, dtype=F32)[:, None] * inv[None, :]
    cos, sin = jnp.cos(ang), jnp.sin(ang)
    reps = LANES // HEAD_DIM
    cos_tab = jnp.tile(jnp.concatenate([cos, cos], axis=1), (1, reps))
    sin_tab = jnp.tile(jnp.concatenate([-sin, sin], axis=1), (1, reps))
    return cos_tab, sin_tab


def _pad_w_in(w):
    pad = jnp.zeros((D_MODEL, D_PROJ_PAD - w.shape[1]), w.dtype)
    return jnp.concatenate([w, pad], axis=1).astype(MXU_DTYPE)


def kernel(x, w_in, b_forget, g_dil, g_fox, w_out, ln1_g, ln1_b, w_router, b_router,
           w_gate, w_up, w_down, ln2_g, ln2_b):
    B, S, D = x.shape
    T = B * S
    cos_tab, sin_tab = _rope_tables(S)
    wr_hi, wr_mid, _ = _split3(w_router.astype(F32).T)
    wr3 = jnp.concatenate([wr_hi, wr_mid], axis=0)
    h = x
    for l in range(DEPTH):
        qd, kd, vd, qf, kf, vf, zf = _in_projection(h, _pad_w_in(w_in[l]), cos_tab, sin_tab)
        c = _forget_cumsum(zf, b_forget[l])[:, :N_HEADS_FOX]
        o_fox = _fox_attention(qf, kf, vf, c)
        o_dil = _dilated_attention(qd, kd, vd)
        h1, h1t, experts, gates, ranks, counts = _out_projection(
            o_dil.reshape(T, D_DIL), o_fox.reshape(T, D_FOX), h.reshape(T, D),
            w_out[l].astype(MXU_DTYPE), g_dil[l], g_fox[l], ln1_g[l], ln1_b[l], wr3, b_router)
        h2 = _grouped_moe(h1, h1t, experts, gates, ranks, counts,
                          w_gate, w_up, w_down, l,
                          ln2_g[l], ln2_b[l])
        h = h2.reshape(B, S, D)
    return h
```

```python
import functools

import numpy as np
import jax
import jax.numpy as jnp
from jax import lax
from jax.experimental import pallas as pl
from jax.experimental.pallas import tpu as pltpu

D_MODEL = 1024
DEPTH = 2
HEAD_DIM = 64
N_HEADS_DIL = 12
N_HEADS_FOX = 4
D_DIL = N_HEADS_DIL * HEAD_DIM
D_FOX = N_HEADS_FOX * HEAD_DIM
DILATIONS = (1, 4, 16)
BLOCK = 128
ROPE_THETA = 10000.0
N_EXPERTS = 16
N_GROUPS = 4
EXPERTS_PER_GROUP = 4
TOP_K = 2
DEEPNORM_ALPHA = (2.0 * DEPTH) ** 0.25
LN_EPS = 1e-5
RMS_EPS = 1e-6

LANES = 128
N_PAIRS_DIL = D_DIL // LANES
D_PROJ_PAD = 3 * D_DIL + 3 * D_FOX + LANES
VMEM_LIMIT = 48 * 1024 * 1024
TOKEN_TILE_ROWS = D_MODEL // LANES
DMA_ISSUE_UNROLL = 8
ZERO_CHUNK_TOKENS = 32
SKEW_DIL = 16
SKEW_PITCH = SKEW_DIL + 1
DIL_PAD = max(d for d in DILATIONS if d != SKEW_DIL) * BLOCK
SKEW_PAD = SKEW_PITCH * BLOCK
MASK_BIAS = -1e30
LOG2_E = 1.4426950408889634
DIL_UNITS_PER_BODY = 16
FOX_TK = 256
FOX_BLOCKS_PER_BODY = 4
ZF_ROWS = 8

MXU_DTYPE = jnp.bfloat16
F32 = jnp.float32
NEG_INF = float("-inf")


def _cparams(n_axes):
    return pltpu.CompilerParams(dimension_semantics=("arbitrary",) * n_axes,
                                vmem_limit_bytes=VMEM_LIMIT)


def _dot(a, b):
    return jnp.dot(a, b, preferred_element_type=F32)


def _dot_nt(a, b):
    return lax.dot_general(a, b, (((1,), (1,)), ((), ())), preferred_element_type=F32)


def _inproj_kernel(x_ref, w_ref, cos_ref, sin_ref,
                   qd_ref, kd_ref, vd_ref, qf_ref, kf_ref, vf_ref, zf_ref):
    xb = x_ref[0].astype(MXU_DTYPE)
    tm = xb.shape[0]
    cos = cos_ref[...]
    sin = sin_ref[...]
    lane = lax.broadcasted_iota(jnp.int32, (tm, LANES), 1)
    first_half = (lane % HEAD_DIM) < (HEAD_DIM // 2)

    def rope(z):
        rot = jnp.where(first_half, pltpu.roll(z, LANES - 32, 1), pltpu.roll(z, 32, 1))
        return z * cos + rot * sin

    def proj(col, width):
        return _dot(xb, w_ref[:, col:col + width])

    scale = HEAD_DIM ** -0.5
    col = 0
    for out_ref, roped, mul in ((qd_ref, True, scale * LOG2_E), (kd_ref, True, 1.0)):
        for c in range(D_DIL // 256):
            z = proj(col, 256)
            for half in range(2):
                zz = rope(z[:, half * LANES:(half + 1) * LANES]) * mul
                out_ref[0, :, c * 256 + half * LANES:c * 256 + (half + 1) * LANES] = zz.astype(out_ref.dtype)
            col += 256
    for c in range(D_DIL // 256):
        vd_ref[0, :, c * 256:(c + 1) * 256] = proj(col, 256).astype(vd_ref.dtype)
        col += 256
    qf_ref[0] = (proj(col, D_FOX) * (scale * LOG2_E)).astype(qf_ref.dtype)
    col += D_FOX
    kf_ref[0] = proj(col, D_FOX).astype(kf_ref.dtype)
    col += D_FOX
    vf_ref[0] = proj(col, D_FOX).astype(vf_ref.dtype)
    col += D_FOX
    zf_ref[0] = proj(col, LANES).T[0:ZF_ROWS, :]


def _in_projection(h, w_pad, cos_tab, sin_tab, tm=1024):
    B, S, D = h.shape
    grid = (B, S // tm)
    row = lambda width: pl.BlockSpec((1, tm, width), lambda b, i: (b, i, 0))
    out_shape = (
        jax.ShapeDtypeStruct((B, S, D_DIL), MXU_DTYPE),
        jax.ShapeDtypeStruct((B, S, D_DIL), MXU_DTYPE),
        jax.ShapeDtypeStruct((B, S, D_DIL), MXU_DTYPE),
        jax.ShapeDtypeStruct((B, S, D_FOX), MXU_DTYPE),
        jax.ShapeDtypeStruct((B, S, D_FOX), MXU_DTYPE),
        jax.ShapeDtypeStruct((B, S, D_FOX), MXU_DTYPE),
        jax.ShapeDtypeStruct((B, ZF_ROWS, S), F32),
    )
    return pl.pallas_call(
        _inproj_kernel,
        grid=grid,
        in_specs=[
            row(D),
            pl.BlockSpec((D, D_PROJ_PAD), lambda b, i: (0, 0)),
            pl.BlockSpec((tm, LANES), lambda b, i: (i, 0)),
            pl.BlockSpec((tm, LANES), lambda b, i: (i, 0)),
        ],
        out_specs=(row(D_DIL), row(D_DIL), row(D_DIL), row(D_FOX), row(D_FOX), row(D_FOX),
                   pl.BlockSpec((1, ZF_ROWS, tm), lambda b, i: (b, 0, i))),
        out_shape=out_shape,
        compiler_params=_cparams(2),
        name="in_projection",
    )(h, w_pad, cos_tab, sin_tab)


def _split3(x):
    hi = x.astype(jnp.bfloat16)
    r1 = x - hi.astype(F32)
    mid = r1.astype(jnp.bfloat16)
    lo = (r1 - mid.astype(F32)).astype(jnp.bfloat16)
    return hi, mid, lo


def _forget_cumsum_kernel(z_ref, b_ref, tri_ref, c_ref):
    x = z_ref[0] + b_ref[...]
    logf = jnp.minimum(x, 0.0) - jnp.log1p(jnp.exp(-jnp.abs(x)))
    tri = tri_ref[...]
    S = x.shape[1]
    carry = jnp.zeros((x.shape[0], 1), F32)
    for blk in range(S // LANES):
        seg = logf[:, blk * LANES:(blk + 1) * LANES]
        hi, mid, lo = _split3(seg)
        cs = (_dot(hi, tri) + _dot(mid, tri)) + _dot(lo, tri) + carry
        c_ref[0, :, blk * LANES:(blk + 1) * LANES] = cs * LOG2_E
        carry = cs[:, LANES - 1:LANES]


def _forget_cumsum(zt, b_forget):
    B, Hf, S = zt.shape
    tri = (np.arange(LANES)[:, None] <= np.arange(LANES)[None, :]).astype(np.float32)
    return pl.pallas_call(
        _forget_cumsum_kernel,
        grid=(B,),
        in_specs=[
            pl.BlockSpec((1, Hf, S), lambda b: (b, 0, 0)),
            pl.BlockSpec((Hf, 1), lambda b: (0, 0)),
            pl.BlockSpec((LANES, LANES), lambda b: (0, 0)),
        ],
        out_specs=pl.BlockSpec((1, Hf, S), lambda b: (b, 0, 0)),
        out_shape=jax.ShapeDtypeStruct((B, Hf, S), F32),
        compiler_params=_cparams(1),
        name="forget_cumsum",
    )(zt, jnp.pad(b_forget.astype(F32), (0, Hf - b_forget.shape[0])).reshape(Hf, 1), jnp.asarray(tri, jnp.bfloat16))


def _pair_rows(xp, is_a):
    zero = jnp.zeros_like(xp)
    return jnp.concatenate([jnp.where(is_a, xp, zero), jnp.where(is_a, zero, xp)], axis=0)


def _causal_bias():
    qi = np.arange(BLOCK)[:, None]
    ki = np.arange(FOX_TK)[None, :]
    vis = np.stack([ki <= qi, ki <= qi + BLOCK])
    return np.where(vis, 0.0, MASK_BIAS).astype(np.float32)


def _fox_kernel(q_ref, k_ref, v_ref, crow_ref, cbias_ref, o_ref):
    S = q_ref.shape[1]
    tk = FOX_TK
    lane = lax.broadcasted_iota(jnp.int32, (BLOCK, LANES), 1)
    is_a = lane < HEAD_DIM
    ones = jnp.ones((tk, LANES), MXU_DTYPE)

    def unit(q2, kv, bias):
        kb, v_ones, cr = kv
        s = _dot_nt(q2, kb)
        sa = s[:BLOCK] - cr[0:1]
        sb = s[BLOCK:] - cr[1:2]
        if bias is not None:
            sa = sa + bias
            sb = sb + bias
        s = jnp.concatenate([sa, sb], axis=0)
        m = jnp.max(s, axis=1, keepdims=True)
        p = jnp.exp2(s - m).astype(MXU_DTYPE)
        o = _dot(p, v_ones)
        acc = jnp.where(is_a, o[:BLOCK, :LANES], o[BLOCK:, :LANES])
        l = jnp.where(is_a, o[:BLOCK, LANES:], o[BLOCK:, LANES:])
        return acc, l, jnp.where(is_a, m[:BLOCK], m[BLOCK:])

    def key_block(j):
        k0 = pl.multiple_of(j * tk, tk)
        return (k_ref[0, pl.ds(k0, tk), :],
                jnp.concatenate([v_ref[0, pl.ds(k0, tk), :], ones], axis=1),
                crow_ref[0, 0, :, pl.ds(k0, tk)])

    def fold(state, units):
        num, den, top_old = state
        top = functools.reduce(jnp.maximum, [top_old] + [m for _, _, m in units])
        scale = jnp.exp2(top_old - top)
        num, den = scale * num, scale * den
        for acc, l, m in units:
            w = jnp.exp2(m - top)
            num = num + w * acc
            den = den + w * l
        return num, den, top

    def q_block(i, _):
        r0 = pl.multiple_of(i * tk, tk)
        q2 = [_pair_rows(q_ref[0, pl.ds(r0 + h * BLOCK, BLOCK), :], is_a) for h in range(2)]
        zero = jnp.zeros((BLOCK, LANES), F32)
        empty = (zero, zero, jnp.full((BLOCK, LANES), MASK_BIAS, F32))

        def fold_blocks(states, blocks):
            units = [[unit(q2[h], kv, None if bias is None else bias[h]) for h in range(2)]
                     for kv, bias in ((key_block(j), bias) for j, bias in blocks)]
            return tuple(fold(states[h], [u[h] for u in units]) for h in range(2))

        def full_blocks(g, states):
            return fold_blocks(states, [(FOX_BLOCKS_PER_BODY * g + u, None) for u in range(FOX_BLOCKS_PER_BODY)])

        states = lax.fori_loop(0, i // FOX_BLOCKS_PER_BODY, full_blocks, (empty, empty))
        diag = (i, (cbias_ref[0], cbias_ref[1]))

        def tail(n_left):
            return lambda st: fold_blocks(st, [(i - n_left + u, None) for u in range(n_left)] + [diag])

        states = lax.switch(i % FOX_BLOCKS_PER_BODY, [tail(r) for r in range(FOX_BLOCKS_PER_BODY)], states)
        for h in range(2):
            num, den, _ = states[h]
            o_ref[0, pl.ds(r0 + h * BLOCK, BLOCK), :] = (num / den).astype(o_ref.dtype)
        return 0

    lax.fori_loop(0, S // tk, q_block, 0)


def _fox_attention(qf, kf, vf, c2):
    B, S, _ = qf.shape
    n_pairs = D_FOX // LANES
    c4 = c2.reshape(B, n_pairs, 2, S)
    blk = pl.BlockSpec((1, S, LANES), lambda b, p: (b, 0, p))
    return pl.pallas_call(
        _fox_kernel,
        grid=(B, n_pairs),
        in_specs=[blk, blk, blk,
                  pl.BlockSpec((1, 1, 2, S), lambda b, p: (b, p, 0, 0)),
                  pl.BlockSpec((2, BLOCK, FOX_TK), lambda b, p: (0, 0, 0))],
        out_specs=blk,
        out_shape=jax.ShapeDtypeStruct((B, S, D_FOX), MXU_DTYPE),
        compiler_params=_cparams(2),
        name="fox_attention",
    )(qf, kf, vf, c4, jnp.asarray(_causal_bias()))


def _band_bias():
    qi = np.arange(BLOCK)[:, None]
    ki = np.arange(2 * BLOCK)[None, :]
    delta = BLOCK + qi - ki
    band = (delta >= 0) & (delta <= BLOCK)
    first = band & (ki >= BLOCK)
    return np.where(np.stack([first, band]), 0.0, MASK_BIAS).astype(np.float32)


def _dil_kernel(q_ref, k_ref, v_ref, bias_ref, o_ref, q32, k32, v32, q17, k17, v17, onorm_s, mu_s):
    S = q_ref.shape[1]
    q32[...] = q_ref[0].astype(F32)
    zeros = jnp.zeros((SKEW_PAD, LANES), F32)
    for plain, skewed, src in ((k32, k17, k_ref), (v32, v17, v_ref)):
        plain[0:DIL_PAD, :] = zeros[0:DIL_PAD]
        skewed[0:SKEW_PAD, :] = zeros
        plain[DIL_PAD:, :] = src[0].astype(F32)

    def skew_rows(g8, _):
        for u in range(8):
            src = pl.multiple_of(g8 * (8 * SKEW_DIL), 8 * SKEW_DIL) + u * SKEW_DIL
            dst = pl.multiple_of(g8 * (8 * SKEW_PITCH), 8) + u * SKEW_PITCH
            q17[pl.ds(dst, SKEW_DIL), :] = q32[pl.ds(src, SKEW_DIL), :]
            k17[pl.ds(SKEW_PAD + dst, SKEW_DIL), :] = k32[pl.ds(DIL_PAD + src, SKEW_DIL), :]
            v17[pl.ds(SKEW_PAD + dst, SKEW_DIL), :] = v32[pl.ds(DIL_PAD + src, SKEW_DIL), :]
        return 0

    lax.fori_loop(0, S // (8 * SKEW_DIL), skew_rows, 0)

    lane = lax.broadcasted_iota(jnp.int32, (BLOCK, LANES), 1)
    is_a = lane < HEAD_DIM
    ones = jnp.ones((2 * BLOCK, LANES), MXU_DTYPE)

    def rows(ref, start, count, stride):
        if stride == 1:
            return ref[pl.ds(start, count), :]
        return ref[pl.ds(start, count, stride=stride), :]

    def unit(dil, rho, n):
        if dil == SKEW_DIL:
            (qs, ks, vs), stride, pad = (q17, k17, v17), SKEW_PITCH, SKEW_PAD
        else:
            (qs, ks, vs), stride, pad = (q32, k32, v32), dil, DIL_PAD
        q0 = rho + stride * BLOCK * n
        qb = rows(qs, q0, BLOCK, stride).astype(MXU_DTYPE)
        kw = rows(ks, q0 + pad - stride * BLOCK, 2 * BLOCK, stride).astype(MXU_DTYPE)
        vw = rows(vs, q0 + pad - stride * BLOCK, 2 * BLOCK, stride).astype(MXU_DTYPE)
        bias = bias_ref[jnp.minimum(n, 1)]
        s = _dot_nt(_pair_rows(qb, is_a), kw)
        s = jnp.concatenate([s[:BLOCK] + bias, s[BLOCK:] + bias], axis=0)
        m = jnp.max(s, axis=1, keepdims=True)
        p = jnp.exp2(s - m).astype(MXU_DTYPE)
        o = _dot(p, jnp.concatenate([vw, ones], axis=1))
        acc = jnp.where(is_a, o[:BLOCK, :LANES], o[BLOCK:, :LANES])
        l = jnp.where(is_a, o[:BLOCK, LANES:], o[BLOCK:, LANES:])
        mu = jnp.where(is_a, m[:BLOCK], m[BLOCK:]) + jnp.log2(l)
        return acc / l, mu

    n_units = S // BLOCK

    for slab, dil in enumerate(d for d in DILATIONS if d != 1):
        blk_bits = (n_units // dil).bit_length() - 1

        def several_units(g, _, slab=slab, dil=dil, blk_bits=blk_bits):
            for u in range(DIL_UNITS_PER_BODY):
                t = g * DIL_UNITS_PER_BODY + u
                rho = lax.shift_right_logical(t, blk_bits)
                n = t & ((1 << blk_bits) - 1)
                o_n, mu = unit(dil, rho, n)
                q0 = rho + dil * BLOCK * n
                onorm_s[slab, pl.ds(q0, BLOCK, stride=dil), :] = o_n
                mu_s[slab, pl.ds(q0, BLOCK, stride=dil), :] = mu
            return 0

        lax.fori_loop(0, n_units // DIL_UNITS_PER_BODY, several_units, 0)

    def merge_blocks(g, _):
        for u in range(DIL_UNITS_PER_BODY):
            n = g * DIL_UNITS_PER_BODY + u
            o_1, mu_1 = unit(1, 0, n)
            r0 = pl.multiple_of(n * BLOCK, BLOCK)
            others = [(onorm_s[sl, pl.ds(r0, BLOCK), :], mu_s[sl, pl.ds(r0, BLOCK), :])
                      for sl in range(len(DILATIONS) - 1)]
            top = functools.reduce(jnp.maximum, [mu_1] + [mu for _, mu in others])
            w = jnp.exp2(mu_1 - top)
            num, den = w * o_1, w
            for o_p, mu_p in others:
                w = jnp.exp2(mu_p - top)
                num = num + w * o_p
                den = den + w
            o_ref[0, pl.ds(r0, BLOCK), :] = (num / den).astype(o_ref.dtype)
        return 0

    lax.fori_loop(0, n_units // DIL_UNITS_PER_BODY, merge_blocks, 0)


def _dilated_attention(qd, kd, vd):
    B, S, _ = qd.shape
    blk = pl.BlockSpec((1, S, LANES), lambda b, p: (b, 0, p))
    n_slabs = len(DILATIONS) - 1
    skewed = S // SKEW_DIL * SKEW_PITCH
    return pl.pallas_call(
        _dil_kernel,
        grid=(B, N_PAIRS_DIL),
        in_specs=[blk, blk, blk, pl.BlockSpec((2, BLOCK, 2 * BLOCK), lambda b, p: (0, 0, 0))],
        out_specs=blk,
        out_shape=jax.ShapeDtypeStruct((B, S, D_DIL), MXU_DTYPE),
        scratch_shapes=[
            pltpu.VMEM((S, LANES), F32),
            pltpu.VMEM((DIL_PAD + S, LANES), F32),
            pltpu.VMEM((DIL_PAD + S, LANES), F32),
            pltpu.VMEM((skewed, LANES), F32),
            pltpu.VMEM((SKEW_PAD + skewed, LANES), F32),
            pltpu.VMEM((SKEW_PAD + skewed, LANES), F32),
            pltpu.VMEM((n_slabs, S, LANES), F32),
            pltpu.VMEM((n_slabs, S, LANES), F32),
        ],
        compiler_params=_cparams(2),
        name="dilated_attention",
    )(qd, kd, vd, jnp.asarray(_band_bias()))


def _layer_norm(u, g, b):
    mu = jnp.mean(u, axis=1, keepdims=True)
    d = u - mu
    var = jnp.mean(d * d, axis=1, keepdims=True)
    return d * lax.rsqrt(var + LN_EPS) * g + b


def _rms_norm(x, g):
    ms = jnp.mean(x * x, axis=1, keepdims=True)
    return x * lax.rsqrt(ms + RMS_EPS) * g


def _top2_of4(vals):
    v1, i1 = vals[0], jnp.zeros(vals[0].shape, jnp.int32)
    for i in range(1, 4):
        better = vals[i] > v1
        v1 = jnp.where(better, vals[i], v1)
        i1 = jnp.where(better, i, i1)
    v2 = jnp.full(vals[0].shape, -1.0, F32)
    i2 = jnp.zeros(vals[0].shape, jnp.int32)
    for i in range(4):
        better = (vals[i] > v2) & (i1 != i)
        v2 = jnp.where(better, vals[i], v2)
        i2 = jnp.where(better, i, i2)
    return v1, i1, v2, i2


def _outproj_kernel(od_ref, of_ref, h_ref, wo_ref, gd_ref, gf_ref, lng_ref, lnb_ref,
                    wr_ref, br_ref, tri_ref,
                    h1_ref, h1t_ref, e_ref, gate_ref, rank_ref, cnt_ref, base_ref):
    step = pl.program_id(0)

    @pl.when(step == 0)
    def _():
        base_ref[...] = jnp.zeros_like(base_ref)

    xd = _rms_norm(od_ref[...].astype(F32), gd_ref[...])
    xf = _rms_norm(of_ref[...].astype(F32), gf_ref[...])
    y = _dot(xd.astype(MXU_DTYPE), wo_ref[0:D_DIL, :]) + _dot(xf.astype(MXU_DTYPE), wo_ref[D_DIL:, :])
    h1 = _layer_norm(DEEPNORM_ALPHA * h_ref[...] + y, lng_ref[...], lnb_ref[...])
    h1_ref[...] = h1
    _store_token_tiles(h1t_ref, h1)
    tm = h1.shape[0]

    h_hi, h_mid, _ = _split3(h1)
    two = _dot_nt(wr_ref[...], h_hi)
    logits = (two[:N_EXPERTS] + two[N_EXPERTS:]) + _dot_nt(wr_ref[0:N_EXPERTS, :], h_mid) + br_ref[...]
    logits = logits - jnp.max(logits, axis=0, keepdims=True)
    ex = jnp.exp(logits)
    probs = ex / jnp.sum(ex, axis=0, keepdims=True)
    pr = [probs[j:j + 1, :] for j in range(N_EXPERTS)]

    def group_score(g):
        v = pr[4 * g:4 * g + 4]
        pairs = [v[a] + v[b] for a in range(4) for b in range(a + 1, 4)]
        return functools.reduce(jnp.maximum, pairs)

    best = group_score(0)
    gsel = jnp.zeros((1, tm), jnp.int32)
    for g in range(1, N_GROUPS):
        sc = group_score(g)
        better = sc > best
        best = jnp.where(better, sc, best)
        gsel = jnp.where(better, g, gsel)
    in_grp = []
    for i in range(EXPERTS_PER_GROUP):
        v = pr[i]
        for g in range(1, N_GROUPS):
            v = jnp.where(gsel == g, pr[4 * g + i], v)
        in_grp.append(v)
    v1, i1, v2, i2 = _top2_of4(in_grp)
    e1 = gsel * EXPERTS_PER_GROUP + i1
    e2 = gsel * EXPERTS_PER_GROUP + i2
    den = v1 + v2
    e_ref[...] = jnp.concatenate([e1, e2], axis=0)
    gate_ref[...] = jnp.concatenate([v1 / den, v2 / den], axis=0)

    eidx = lax.broadcasted_iota(jnp.int32, (N_EXPERTS, tm), 0)
    oh1 = (eidx == e1).astype(F32)
    oh2 = (eidx == e2).astype(F32)
    tot = oh1 + oh2
    before = base_ref[...] + _dot(tot.astype(jnp.bfloat16), tri_ref[...])
    r1 = jnp.sum(oh1 * before, axis=0, keepdims=True)
    r2 = jnp.sum(oh2 * before, axis=0, keepdims=True)
    rank_ref[...] = jnp.concatenate([r1, r2], axis=0).astype(jnp.int32)
    base_ref[...] = base_ref[...] + jnp.sum(tot, axis=1, keepdims=True)
    cnt_ref[...] = jnp.broadcast_to(base_ref[...], cnt_ref.shape)


def _out_projection(od, of, h, w_out, g_dil, g_fox, ln_g, ln_b, wr3, b_router, tm=1024):
    T = h.shape[0]
    tri = (np.arange(tm)[:, None] < np.arange(tm)[None, :]).astype(np.float32)
    rows = lambda width: pl.BlockSpec((tm, width), lambda i: (i, 0))
    full = lambda a: pl.BlockSpec(a.shape, lambda i: (0,) * a.ndim)
    tok = pl.BlockSpec((TOP_K, tm), lambda i: (0, i))
    consts = [w_out, g_dil.reshape(1, -1), g_fox.reshape(1, -1), ln_g.reshape(1, -1), ln_b.reshape(1, -1),
              wr3, b_router.reshape(-1, 1).astype(F32), jnp.asarray(tri, jnp.bfloat16)]
    return pl.pallas_call(
        _outproj_kernel,
        grid=(T // tm,),
        in_specs=[rows(D_DIL), rows(D_FOX), rows(D_MODEL)] + [full(a) for a in consts],
        out_specs=(rows(D_MODEL), pl.BlockSpec((tm * TOKEN_TILE_ROWS, LANES), lambda i: (i, 0)),
                   tok, tok, tok, pl.BlockSpec((N_EXPERTS, LANES), lambda i: (0, 0))),
        out_shape=(
            jax.ShapeDtypeStruct((T, D_MODEL), F32),
            jax.ShapeDtypeStruct((T * TOKEN_TILE_ROWS, LANES), F32),
            jax.ShapeDtypeStruct((TOP_K, T), jnp.int32),
            jax.ShapeDtypeStruct((TOP_K, T), F32),
            jax.ShapeDtypeStruct((TOP_K, T), jnp.int32),
            jax.ShapeDtypeStruct((N_EXPERTS, LANES), F32),
        ),
        scratch_shapes=[pltpu.VMEM((N_EXPERTS, 1), F32)],
        compiler_params=_cparams(1),
        name="out_projection_router",
    )(od, of, h, *consts)


def _store_token_tiles(ref, x):
    n = x.shape[0]
    for c in range(TOKEN_TILE_ROWS):
        ref[pl.ds(c, n, stride=TOKEN_TILE_ROWS), :] = x[:, c * LANES:(c + 1) * LANES]


def _load_token_tiles(ref, n, first=0):
    return jnp.concatenate([ref[pl.ds(first * TOKEN_TILE_ROWS + c, n, stride=TOKEN_TILE_ROWS), :]
                            for c in range(TOKEN_TILE_ROWS)], axis=1)


def _tile_copy(src, src_row, dst, dst_row, sem):
    return pltpu.make_async_copy(src.at[pl.ds(src_row, TOKEN_TILE_ROWS)],
                                 dst.at[pl.ds(dst_row, TOKEN_TILE_ROWS)], sem)


def _dispatch_kernel(pc0_ref, pc1_ref, pp0_ref, pp1_ref, pad0_ref, padn_ref, h_hbm, xg_out,
                     hbuf, sem, sem_load, ztile, *, td):
    j = pl.program_id(0)
    n_steps = pl.num_programs(0)
    tile_rows = td * TOKEN_TILE_ROWS

    def load(tile, slot):
        src = pl.multiple_of(tile * tile_rows, tile_rows)
        return pltpu.make_async_copy(h_hbm.at[pl.ds(src, tile_rows)], hbuf.at[slot], sem_load.at[slot])

    def scatter(p0_ref, p1_ref, slot, which):
        def copies(t):
            src = pl.multiple_of(t * TOKEN_TILE_ROWS, TOKEN_TILE_ROWS)
            return (_tile_copy(hbuf.at[slot], src, xg_out, pl.multiple_of(p0_ref[t], TOKEN_TILE_ROWS), sem.at[which]),
                    _tile_copy(hbuf.at[slot], src, xg_out, pl.multiple_of(p1_ref[t], TOKEN_TILE_ROWS), sem.at[which]))

        def issue(t, _):
            for thread, cp in enumerate(copies(t)):
                cp.start(priority=thread)
            return 0

        def drain(t, _):
            for cp in copies(t):
                cp.wait()
            return 0

        def issue_all():
            lax.fori_loop(0, td, issue, 0, unroll=DMA_ISSUE_UNROLL)

        def wait_all():
            lax.fori_loop(0, td, drain, 0, unroll=DMA_ISSUE_UNROLL)

        return issue_all, wait_all

    @pl.when(j == 0)
    def _():
        load(0, 0).start()
        load(1, 1).start()

    load(j, j % 3).wait()
    issue_cur, wait_cur = scatter(pc0_ref, pc1_ref, j % 3, j % 2)
    issue_cur()

    @pl.when(j >= 1)
    def _():
        scatter(pp0_ref, pp1_ref, (j - 1) % 3, (j - 1) % 2)[1]()

    @pl.when(j + 2 < n_steps)
    def _():
        load(j + 2, (j + 2) % 3).start()

    pl.when(j == n_steps - 1)(wait_cur)

    @pl.when(pl.program_id(0) == pl.num_programs(0) - 1)
    def _():
        ztile[...] = jnp.zeros(ztile.shape, ztile.dtype)

        chunk_rows = ztile.shape[0]
        chunk_tokens = chunk_rows // TOKEN_TILE_ROWS

        def for_each_zero_copy(action):
            def hole(e, _):
                n_chunks = padn_ref[e] // chunk_tokens

                def chunk(c, _):
                    dst = pl.multiple_of(pad0_ref[e] + c * chunk_rows, TOKEN_TILE_ROWS)
                    action(pltpu.make_async_copy(ztile, xg_out.at[pl.ds(dst, chunk_rows)], sem.at[0]))
                    return 0

                def single(r, _):
                    dst = pl.multiple_of(pad0_ref[e] + r * TOKEN_TILE_ROWS, TOKEN_TILE_ROWS)
                    action(_tile_copy(ztile, 0, xg_out, dst, sem.at[0]))
                    return 0

                lax.fori_loop(0, n_chunks, chunk, 0)
                lax.fori_loop(n_chunks * chunk_tokens, padn_ref[e], single, 0)
                return 0
            lax.fori_loop(0, pad0_ref.shape[0], hole, 0)

        for_each_zero_copy(lambda cp: cp.start())
        for_each_zero_copy(lambda cp: cp.wait())


def _dispatch(h1t, pos_rows, pad_rows, pad_counts, n_rows, td=256):
    T = h1t.shape[0] // TOKEN_TILE_ROWS
    assert T // td >= 2
    cur = pl.BlockSpec((td,), lambda i: (i,), memory_space=pltpu.SMEM)
    prev = pl.BlockSpec((td,), lambda i: (jnp.maximum(i - 1, 0),), memory_space=pltpu.SMEM)
    smem_all = pl.BlockSpec(memory_space=pltpu.SMEM)
    anyspec = pl.BlockSpec(memory_space=pl.ANY)
    return pl.pallas_call(
        functools.partial(_dispatch_kernel, td=td),
        grid=(T // td,),
        in_specs=[cur, cur, prev, prev, smem_all, smem_all, anyspec],
        out_specs=anyspec,
        out_shape=jax.ShapeDtypeStruct((n_rows * TOKEN_TILE_ROWS, LANES), F32),
        scratch_shapes=[pltpu.VMEM((3, td * TOKEN_TILE_ROWS, LANES), F32),
                        pltpu.SemaphoreType.DMA((2,)), pltpu.SemaphoreType.DMA((3,)),
                        pltpu.VMEM((ZERO_CHUNK_TOKENS * TOKEN_TILE_ROWS, LANES), F32)],
        compiler_params=_cparams(1),
        name="moe_dispatch",
    )(pos_rows[0], pos_rows[1], pos_rows[0], pos_rows[1], pad_rows, pad_counts, h1t)


def _ffn_kernel(be_ref, nv_ref, slot_ref, nxt_ref, x_ref, wg_hbm, wu_hbm, wd_hbm, y_ref, wstage, wb, sem,
                *, tmf, layer):
    j = pl.program_id(0)
    valid = j < nv_ref[0]

    def fetch(expert, slot):
        return [pltpu.make_async_copy(w.at[layer, expert], wstage.at[slot, k], sem.at[slot])
                for k, w in enumerate((wg_hbm, wu_hbm, wd_hbm))]

    @pl.when(j == 0)
    def _():
        for cp in fetch(be_ref[0], 0):
            cp.start()

    @pl.when(valid & ((j == 0) | (be_ref[j] != be_ref[jnp.maximum(j - 1, 0)])))
    def _():
        slot = slot_ref[j]
        for cp in fetch(be_ref[j], slot):
            cp.wait()
        for k in range(3):
            wb[k] = wstage[slot, k].astype(MXU_DTYPE)

        @pl.when(nxt_ref[j] >= 0)
        def _():
            for cp in fetch(nxt_ref[j], 1 - slot):
                cp.start()

    @pl.when(valid)
    def _():
        xb = _load_token_tiles(x_ref, tmf).astype(MXU_DTYPE)
        a = _dot(xb, wb[0])
        u = _dot(xb, wb[1])
        hmid = (a * jax.nn.sigmoid(a)) * u
        _store_token_tiles(y_ref, _dot(hmid.astype(MXU_DTYPE), wb[2]))

    @pl.when(j >= nv_ref[0])
    def _():
        y_ref[...] = jnp.zeros_like(y_ref)


def _expert_ffn(xg, blk_e, n_valid, present, wg, wu, wd, layer, tmf):
    n_rows = xg.shape[0] // TOKEN_TILE_ROWS
    D = D_MODEL
    nb = n_rows // tmf
    eids = jnp.arange(N_EXPERTS, dtype=jnp.int32)
    order = jnp.cumsum(present.astype(jnp.int32)) - 1
    later = (eids[None, :] > eids[:, None]) & present[None, :]
    nxt_e = jnp.min(jnp.where(later, eids[None, :], N_EXPERTS), axis=1)
    nxt_e = jnp.where(nxt_e == N_EXPERTS, -1, nxt_e)
    slot = (order & 1)[blk_e].astype(jnp.int32)
    nxt = nxt_e[blk_e].astype(jnp.int32)
    xmap = lambda j, be, nv, sl, nx: (jnp.minimum(j, nv[0] - 1), 0)
    anyspec = pl.BlockSpec(memory_space=pl.ANY)
    grid_spec = pltpu.PrefetchScalarGridSpec(
        num_scalar_prefetch=4,
        grid=(nb,),
        in_specs=[pl.BlockSpec((tmf * TOKEN_TILE_ROWS, LANES), xmap), anyspec, anyspec, anyspec],
        out_specs=pl.BlockSpec((tmf * TOKEN_TILE_ROWS, LANES), lambda j, be, nv, sl, nx: (j, 0)),
        scratch_shapes=[pltpu.VMEM((2, 3, D, D), F32), pltpu.VMEM((3, D, D), MXU_DTYPE),
                        pltpu.SemaphoreType.DMA((2,))],
    )
    return pl.pallas_call(
        functools.partial(_ffn_kernel, tmf=tmf, layer=layer),
        grid_spec=grid_spec,
        out_shape=jax.ShapeDtypeStruct(xg.shape, F32),
        compiler_params=_cparams(1),
        name="moe_expert_ffn",
    )(blk_e, n_valid, slot, nxt, xg, wg, wu, wd)


def _combine_kernel(pc0_ref, pc1_ref, pn0_ref, pn1_ref, gate_ref, h_ref, lng_ref, lnb_ref, y_hbm, o_ref,
                    buf_a0, buf_a1, buf_b0, buf_b1, sem, *, tc):
    j = pl.program_id(0)

    def tile(p0_ref, p1_ref, first, bufs, which):
        def copies(t):
            dst = pl.multiple_of(t * TOKEN_TILE_ROWS, TOKEN_TILE_ROWS)
            return (_tile_copy(y_hbm, pl.multiple_of(p0_ref[first + t], TOKEN_TILE_ROWS), bufs[0], dst, sem.at[which]),
                    _tile_copy(y_hbm, pl.multiple_of(p1_ref[first + t], TOKEN_TILE_ROWS), bufs[1], dst, sem.at[which]))

        def issue(t, _):
            for thread, cp in enumerate(copies(t)):
                cp.start(priority=thread)
            return 0

        def drain(t, _):
            for cp in copies(t):
                cp.wait()
            return 0

        def issue_all():
            lax.fori_loop(0, tc, issue, 0, unroll=DMA_ISSUE_UNROLL)

        def wait_all():
            lax.fori_loop(0, tc, drain, 0, unroll=DMA_ISSUE_UNROLL)

        return issue_all, wait_all

    def finish(first, bufs):
        g = gate_ref[first:first + tc, :]
        y = _load_token_tiles(bufs[0], tc) * g[:, 0:1] + _load_token_tiles(bufs[1], tc) * g[:, 1:2]
        o_ref[first:first + tc, :] = _layer_norm(DEEPNORM_ALPHA * h_ref[first:first + tc, :] + y,
                                                 lng_ref[...], lnb_ref[...])

    issue_a, wait_a = tile(pc0_ref, pc1_ref, 0, (buf_a0, buf_a1), 0)
    issue_b, wait_b = tile(pc0_ref, pc1_ref, tc, (buf_b0, buf_b1), 1)
    issue_next_a, _ = tile(pn0_ref, pn1_ref, 0, (buf_a0, buf_a1), 0)

    pl.when(j == 0)(issue_a)
    issue_b()
    wait_a()
    finish(0, (buf_a0, buf_a1))
    pl.when(j + 1 < pl.num_programs(0))(issue_next_a)
    wait_b()
    finish(tc, (buf_b0, buf_b1))


def _combine(yg, pos_rows, gates_col, h1, ln_g, ln_b, tc=256):
    T, D = h1.shape
    n_tiles = T // tc
    cur = pl.BlockSpec((2 * tc,), lambda j: (j,), memory_space=pltpu.SMEM)
    nxt = pl.BlockSpec((tc,), lambda j: (jnp.minimum(2 * j + 2, n_tiles - 1),), memory_space=pltpu.SMEM)
    rows = lambda width: pl.BlockSpec((2 * tc, width), lambda j: (j, 0))
    vec = pl.BlockSpec((1, D), lambda j: (0, 0))
    buf = pltpu.VMEM((tc * TOKEN_TILE_ROWS, LANES), F32)
    return pl.pallas_call(
        functools.partial(_combine_kernel, tc=tc),
        grid=(n_tiles // 2,),
        in_specs=[cur, cur, nxt, nxt, rows(TOP_K), rows(D), vec, vec, pl.BlockSpec(memory_space=pl.ANY)],
        out_specs=rows(D),
        out_shape=jax.ShapeDtypeStruct((T, D), F32),
        scratch_shapes=[buf, buf, buf, buf, pltpu.SemaphoreType.DMA((2,))],
        compiler_params=_cparams(1),
        name="moe_combine_ln",
    )(pos_rows[0], pos_rows[1], pos_rows[0], pos_rows[1], gates_col, h1,
      ln_g.reshape(1, -1), ln_b.reshape(1, -1), yg)


def _grouped_moe(h1, h1t, experts, gates, ranks, counts, wg, wu, wd, layer, ln_g, ln_b, tmf=512):
    T = h1.shape[0]
    nb = (T * TOP_K) // tmf + N_EXPERTS
    cnt = counts[:, 0].astype(jnp.int32)
    pcnt = (cnt + tmf - 1) // tmf * tmf
    pends = jnp.cumsum(pcnt)
    pstart = pends - pcnt
    eids = jnp.arange(N_EXPERTS, dtype=jnp.int32)
    seg = jnp.sum(jnp.where(experts[None] == eids[:, None, None], pstart[:, None, None], 0), axis=0)
    pos_rows = (seg + ranks) * TOKEN_TILE_ROWS
    blk_start = jnp.arange(nb, dtype=jnp.int32) * tmf
    blk_e = jnp.minimum(jnp.sum((pends[None, :] <= blk_start[:, None]).astype(jnp.int32), axis=1), N_EXPERTS - 1)
    n_valid = (pends[-1:] // tmf).astype(jnp.int32)
    hole_start = jnp.concatenate([pstart + cnt, pends[-1:]])
    hole_rows = jnp.concatenate([pcnt - cnt, nb * tmf - pends[-1:]])
    xg = _dispatch(h1t, pos_rows, hole_start * TOKEN_TILE_ROWS, hole_rows, nb * tmf)
    yg = _expert_ffn(xg, blk_e, n_valid, cnt > 0, wg, wu, wd, layer, tmf)
    return _combine(yg, pos_rows, gates.T, h1, ln_g, ln_b)


def _rope_tables(S):
    half = HEAD_DIM // 2
    inv = ROPE_THETA ** (-jnp.arange(half, dtype=F32) / half)
    ang = jnp.arange(S, dtype=F32)[:, None] * inv[None, :]
    cos, sin = jnp.cos(ang), jnp.sin(ang)
    reps = LANES // HEAD_DIM
    cos_tab = jnp.tile(jnp.concatenate([cos, cos], axis=1), (1, reps))
    sin_tab = jnp.tile(jnp.concatenate([-sin, sin], axis=1), (1, reps))
    return cos_tab, sin_tab


def _pad_w_in(w):
    pad = jnp.zeros((D_MODEL, D_PROJ_PAD - w.shape[1]), w.dtype)
    return jnp.concatenate([w, pad], axis=1).astype(MXU_DTYPE)


def kernel(x, w_in, b_forget, g_dil, g_fox, w_out, ln1_g, ln1_b, w_router, b_router,
           w_gate, w_up, w_down, ln2_g, ln2_b):
    B, S, D = x.shape
    T = B * S
    cos_tab, sin_tab = _rope_tables(S)
    wr_hi, wr_mid, _ = _split3(w_router.astype(F32).T)
    wr3 = jnp.concatenate([wr_hi, wr_mid], axis=0)
    h = x
    for l in range(DEPTH):
        qd, kd, vd, qf, kf, vf, zf = _in_projection(h, _pad_w_in(w_in[l]), cos_tab, sin_tab)
        c = _forget_cumsum(zf, b_forget[l])[:, :N_HEADS_FOX]
        o_fox = _fox_attention(qf, kf, vf, c)
        o_dil = _dilated_attention(qd, kd, vd)
        h1, h1t, experts, gates, ranks, counts = _out_projection(
            o_dil.reshape(T, D_DIL), o_fox.reshape(T, D_FOX), h.reshape(T, D),
            w_out[l].astype(MXU_DTYPE), g_dil[l], g_fox[l], ln1_g[l], ln1_b[l], wr3, b_router)
        h2 = _grouped_moe(h1, h1t, experts, gates, ranks, counts,
                          w_gate, w_up, w_down, l,
                          ln2_g[l], ln2_b[l])
        h = h2.reshape(B, S, D)
    return h
```

```python
import functools

import numpy as np
import jax
import jax.numpy as jnp
from jax import lax
from jax.experimental import pallas as pl
from jax.experimental.pallas import tpu as pltpu

D_MODEL = 1024
DEPTH = 2
HEAD_DIM = 64
N_HEADS_DIL = 12
N_HEADS_FOX = 4
D_DIL = N_HEADS_DIL * HEAD_DIM
D_FOX = N_HEADS_FOX * HEAD_DIM
DILATIONS = (1, 4, 16)
BLOCK = 128
ROPE_THETA = 10000.0
N_EXPERTS = 16
N_GROUPS = 4
EXPERTS_PER_GROUP = 4
TOP_K = 2
DEEPNORM_ALPHA = (2.0 * DEPTH) ** 0.25
LN_EPS = 1e-5
RMS_EPS = 1e-6

LANES = 128
N_PAIRS_DIL = D_DIL // LANES
D_PROJ_PAD = 3 * D_DIL + 3 * D_FOX + LANES
VMEM_LIMIT = 48 * 1024 * 1024
TOKEN_TILE_ROWS = D_MODEL // LANES
DMA_ISSUE_UNROLL = 8
ZERO_CHUNK_TOKENS = 32
SKEW_DIL = 16
SKEW_PITCH = SKEW_DIL + 1
DIL_PAD = max(d for d in DILATIONS if d != SKEW_DIL) * BLOCK
SKEW_PAD = SKEW_PITCH * BLOCK
MASK_BIAS = -1e30
LOG2_E = 1.4426950408889634
DIL_UNITS_PER_BODY = 32
FOX_TK = 256
FOX_BLOCKS_PER_BODY = 4
ZF_ROWS = 8

MXU_DTYPE = jnp.bfloat16
F32 = jnp.float32
NEG_INF = float("-inf")


def _cparams(n_axes):
    return pltpu.CompilerParams(dimension_semantics=("arbitrary",) * n_axes,
                                vmem_limit_bytes=VMEM_LIMIT)


def _dot(a, b):
    return jnp.dot(a, b, preferred_element_type=F32)


def _dot_nt(a, b):
    return lax.dot_general(a, b, (((1,), (1,)), ((), ())), preferred_element_type=F32)


def _inproj_kernel(x_ref, w_ref, cos_ref, sin_ref,
                   qd_ref, kd_ref, vd_ref, qf_ref, kf_ref, vf_ref, zf_ref):
    xb = x_ref[0].astype(MXU_DTYPE)
    tm = xb.shape[0]
    cos = cos_ref[...]
    sin = sin_ref[...]
    lane = lax.broadcasted_iota(jnp.int32, (tm, LANES), 1)
    first_half = (lane % HEAD_DIM) < (HEAD_DIM // 2)

    def rope(z):
        rot = jnp.where(first_half, pltpu.roll(z, LANES - 32, 1), pltpu.roll(z, 32, 1))
        return z * cos + rot * sin

    def proj(col, width):
        return _dot(xb, w_ref[:, col:col + width])

    scale = HEAD_DIM ** -0.5
    col = 0
    for out_ref, roped, mul in ((qd_ref, True, scale * LOG2_E), (kd_ref, True, 1.0)):
        for c in range(D_DIL // 256):
            z = proj(col, 256)
            for half in range(2):
                zz = rope(z[:, half * LANES:(half + 1) * LANES]) * mul
                out_ref[0, :, c * 256 + half * LANES:c * 256 + (half + 1) * LANES] = zz.astype(out_ref.dtype)
            col += 256
    for c in range(D_DIL // 256):
        vd_ref[0, :, c * 256:(c + 1) * 256] = proj(col, 256).astype(vd_ref.dtype)
        col += 256
    qf_ref[0] = (proj(col, D_FOX) * (scale * LOG2_E)).astype(qf_ref.dtype)
    col += D_FOX
    kf_ref[0] = proj(col, D_FOX).astype(kf_ref.dtype)
    col += D_FOX
    vf_ref[0] = proj(col, D_FOX).astype(vf_ref.dtype)
    col += D_FOX
    zf_ref[0] = proj(col, LANES).T[0:ZF_ROWS, :]


def _in_projection(h, w_pad, cos_tab, sin_tab, tm=1024):
    B, S, D = h.shape
    grid = (B, S // tm)
    row = lambda width: pl.BlockSpec((1, tm, width), lambda b, i: (b, i, 0))
    out_shape = (
        jax.ShapeDtypeStruct((B, S, D_DIL), MXU_DTYPE),
        jax.ShapeDtypeStruct((B, S, D_DIL), MXU_DTYPE),
        jax.ShapeDtypeStruct((B, S, D_DIL), MXU_DTYPE),
        jax.ShapeDtypeStruct((B, S, D_FOX), MXU_DTYPE),
        jax.ShapeDtypeStruct((B, S, D_FOX), MXU_DTYPE),
        jax.ShapeDtypeStruct((B, S, D_FOX), MXU_DTYPE),
        jax.ShapeDtypeStruct((B, ZF_ROWS, S), F32),
    )
    return pl.pallas_call(
        _inproj_kernel,
        grid=grid,
        in_specs=[
            row(D),
            pl.BlockSpec((D, D_PROJ_PAD), lambda b, i: (0, 0)),
            pl.BlockSpec((tm, LANES), lambda b, i: (i, 0)),
            pl.BlockSpec((tm, LANES), lambda b, i: (i, 0)),
        ],
        out_specs=(row(D_DIL), row(D_DIL), row(D_DIL), row(D_FOX), row(D_FOX), row(D_FOX),
                   pl.BlockSpec((1, ZF_ROWS, tm), lambda b, i: (b, 0, i))),
        out_shape=out_shape,
        compiler_params=_cparams(2),
        name="in_projection",
    )(h, w_pad, cos_tab, sin_tab)


def _split3(x):
    hi = x.astype(jnp.bfloat16)
    r1 = x - hi.astype(F32)
    mid = r1.astype(jnp.bfloat16)
    lo = (r1 - mid.astype(F32)).astype(jnp.bfloat16)
    return hi, mid, lo


def _forget_cumsum_kernel(z_ref, b_ref, tri_ref, c_ref):
    x = z_ref[0] + b_ref[...]
    logf = jnp.minimum(x, 0.0) - jnp.log1p(jnp.exp(-jnp.abs(x)))
    tri = tri_ref[...]
    S = x.shape[1]
    carry = jnp.zeros((x.shape[0], 1), F32)
    for blk in range(S // LANES):
        seg = logf[:, blk * LANES:(blk + 1) * LANES]
        hi, mid, lo = _split3(seg)
        cs = (_dot(hi, tri) + _dot(mid, tri)) + _dot(lo, tri) + carry
        c_ref[0, :, blk * LANES:(blk + 1) * LANES] = cs * LOG2_E
        carry = cs[:, LANES - 1:LANES]


def _forget_cumsum(zt, b_forget):
    B, Hf, S = zt.shape
    tri = (np.arange(LANES)[:, None] <= np.arange(LANES)[None, :]).astype(np.float32)
    return pl.pallas_call(
        _forget_cumsum_kernel,
        grid=(B,),
        in_specs=[
            pl.BlockSpec((1, Hf, S), lambda b: (b, 0, 0)),
            pl.BlockSpec((Hf, 1), lambda b: (0, 0)),
            pl.BlockSpec((LANES, LANES), lambda b: (0, 0)),
        ],
        out_specs=pl.BlockSpec((1, Hf, S), lambda b: (b, 0, 0)),
        out_shape=jax.ShapeDtypeStruct((B, Hf, S), F32),
        compiler_params=_cparams(1),
        name="forget_cumsum",
    )(zt, jnp.pad(b_forget.astype(F32), (0, Hf - b_forget.shape[0])).reshape(Hf, 1), jnp.asarray(tri, jnp.bfloat16))


def _pair_rows(xp, is_a):
    zero = jnp.zeros_like(xp)
    return jnp.concatenate([jnp.where(is_a, xp, zero), jnp.where(is_a, zero, xp)], axis=0)


def _causal_bias():
    qi = np.arange(BLOCK)[:, None]
    ki = np.arange(FOX_TK)[None, :]
    vis = np.stack([ki <= qi, ki <= qi + BLOCK])
    return np.where(vis, 0.0, MASK_BIAS).astype(np.float32)


def _fox_kernel(q_ref, k_ref, v_ref, crow_ref, cbias_ref, o_ref):
    S = q_ref.shape[1]
    tk = FOX_TK
    lane = lax.broadcasted_iota(jnp.int32, (BLOCK, LANES), 1)
    is_a = lane < HEAD_DIM
    ones = jnp.ones((tk, LANES), MXU_DTYPE)

    def unit(q2, kv, bias):
        kb, v_ones, cr = kv
        s = _dot_nt(q2, kb)
        sa = s[:BLOCK] - cr[0:1]
        sb = s[BLOCK:] - cr[1:2]
        if bias is not None:
            sa = sa + bias
            sb = sb + bias
        s = jnp.concatenate([sa, sb], axis=0)
        m = jnp.max(s, axis=1, keepdims=True)
        p = jnp.exp2(s - m).astype(MXU_DTYPE)
        o = _dot(p, v_ones)
        acc = jnp.where(is_a, o[:BLOCK, :LANES], o[BLOCK:, :LANES])
        l = jnp.where(is_a, o[:BLOCK, LANES:], o[BLOCK:, LANES:])
        return acc, l, jnp.where(is_a, m[:BLOCK], m[BLOCK:])

    def key_block(j):
        k0 = pl.multiple_of(j * tk, tk)
        return (k_ref[0, pl.ds(k0, tk), :],
                jnp.concatenate([v_ref[0, pl.ds(k0, tk), :], ones], axis=1),
                crow_ref[0, 0, :, pl.ds(k0, tk)])

    def fold(state, units):
        num, den, top_old = state
        top = functools.reduce(jnp.maximum, [top_old] + [m for _, _, m in units])
        scale = jnp.exp2(top_old - top)
        num, den = scale * num, scale * den
        for acc, l, m in units:
            w = jnp.exp2(m - top)
            num = num + w * acc
            den = den + w * l
        return num, den, top

    def q_block(i, _):
        r0 = pl.multiple_of(i * tk, tk)
        q2 = [_pair_rows(q_ref[0, pl.ds(r0 + h * BLOCK, BLOCK), :], is_a) for h in range(2)]
        zero = jnp.zeros((BLOCK, LANES), F32)
        empty = (zero, zero, jnp.full((BLOCK, LANES), MASK_BIAS, F32))

        def fold_blocks(states, blocks):
            units = [[unit(q2[h], kv, None if bias is None else bias[h]) for h in range(2)]
                     for kv, bias in ((key_block(j), bias) for j, bias in blocks)]
            return tuple(fold(states[h], [u[h] for u in units]) for h in range(2))

        def full_blocks(g, states):
            return fold_blocks(states, [(FOX_BLOCKS_PER_BODY * g + u, None) for u in range(FOX_BLOCKS_PER_BODY)])

        states = lax.fori_loop(0, i // FOX_BLOCKS_PER_BODY, full_blocks, (empty, empty))
        diag = (i, (cbias_ref[0], cbias_ref[1]))

        def tail(n_left):
            return lambda st: fold_blocks(st, [(i - n_left + u, None) for u in range(n_left)] + [diag])

        states = lax.switch(i % FOX_BLOCKS_PER_BODY, [tail(r) for r in range(FOX_BLOCKS_PER_BODY)], states)
        for h in range(2):
            num, den, _ = states[h]
            o_ref[0, pl.ds(r0 + h * BLOCK, BLOCK), :] = (num / den).astype(o_ref.dtype)
        return 0

    lax.fori_loop(0, S // tk, q_block, 0)


def _fox_attention(qf, kf, vf, c2):
    B, S, _ = qf.shape
    n_pairs = D_FOX // LANES
    c4 = c2.reshape(B, n_pairs, 2, S)
    blk = pl.BlockSpec((1, S, LANES), lambda b, p: (b, 0, p))
    return pl.pallas_call(
        _fox_kernel,
        grid=(B, n_pairs),
        in_specs=[blk, blk, blk,
                  pl.BlockSpec((1, 1, 2, S), lambda b, p: (b, p, 0, 0)),
                  pl.BlockSpec((2, BLOCK, FOX_TK), lambda b, p: (0, 0, 0))],
        out_specs=blk,
        out_shape=jax.ShapeDtypeStruct((B, S, D_FOX), MXU_DTYPE),
        compiler_params=_cparams(2),
        name="fox_attention",
    )(qf, kf, vf, c4, jnp.asarray(_causal_bias()))


def _band_bias():
    qi = np.arange(BLOCK)[:, None]
    ki = np.arange(2 * BLOCK)[None, :]
    delta = BLOCK + qi - ki
    band = (delta >= 0) & (delta <= BLOCK)
    first = band & (ki >= BLOCK)
    return np.where(np.stack([first, band]), 0.0, MASK_BIAS).astype(np.float32)


def _dil_kernel(q_ref, k_ref, v_ref, bias_ref, o_ref, q32, k32, v32, q17, k17, v17, onorm_s, mu_s):
    S = q_ref.shape[1]
    q32[...] = q_ref[0].astype(F32)
    zeros = jnp.zeros((SKEW_PAD, LANES), F32)
    for plain, skewed, src in ((k32, k17, k_ref), (v32, v17, v_ref)):
        plain[0:DIL_PAD, :] = zeros[0:DIL_PAD]
        skewed[0:SKEW_PAD, :] = zeros
        plain[DIL_PAD:, :] = src[0].astype(F32)

    def skew_rows(g8, _):
        for u in range(8):
            src = pl.multiple_of(g8 * (8 * SKEW_DIL), 8 * SKEW_DIL) + u * SKEW_DIL
            dst = pl.multiple_of(g8 * (8 * SKEW_PITCH), 8) + u * SKEW_PITCH
            q17[pl.ds(dst, SKEW_DIL), :] = q32[pl.ds(src, SKEW_DIL), :]
            k17[pl.ds(SKEW_PAD + dst, SKEW_DIL), :] = k32[pl.ds(DIL_PAD + src, SKEW_DIL), :]
            v17[pl.ds(SKEW_PAD + dst, SKEW_DIL), :] = v32[pl.ds(DIL_PAD + src, SKEW_DIL), :]
        return 0

    lax.fori_loop(0, S // (8 * SKEW_DIL), skew_rows, 0)

    lane = lax.broadcasted_iota(jnp.int32, (BLOCK, LANES), 1)
    is_a = lane < HEAD_DIM
    ones = jnp.ones((2 * BLOCK, LANES), MXU_DTYPE)

    def rows(ref, start, count, stride):
        if stride == 1:
            return ref[pl.ds(start, count), :]
        return ref[pl.ds(start, count, stride=stride), :]

    def unit(dil, rho, n):
        if dil == SKEW_DIL:
            (qs, ks, vs), stride, pad = (q17, k17, v17), SKEW_PITCH, SKEW_PAD
        else:
            (qs, ks, vs), stride, pad = (q32, k32, v32), dil, DIL_PAD
        q0 = rho + stride * BLOCK * n
        qb = rows(qs, q0, BLOCK, stride).astype(MXU_DTYPE)
        kw = rows(ks, q0 + pad - stride * BLOCK, 2 * BLOCK, stride).astype(MXU_DTYPE)
        vw = rows(vs, q0 + pad - stride * BLOCK, 2 * BLOCK, stride).astype(MXU_DTYPE)
        bias = bias_ref[jnp.minimum(n, 1)]
        s = _dot_nt(_pair_rows(qb, is_a), kw)
        s = jnp.concatenate([s[:BLOCK] + bias, s[BLOCK:] + bias], axis=0)
        m = jnp.max(s, axis=1, keepdims=True)
        p = jnp.exp2(s - m).astype(MXU_DTYPE)
        o = _dot(p, jnp.concatenate([vw, ones], axis=1))
        acc = jnp.where(is_a, o[:BLOCK, :LANES], o[BLOCK:, :LANES])
        l = jnp.where(is_a, o[:BLOCK, LANES:], o[BLOCK:, LANES:])
        mu = jnp.where(is_a, m[:BLOCK], m[BLOCK:]) + jnp.log2(l)
        return acc / l, mu

    n_units = S // BLOCK

    for slab, dil in enumerate(d for d in DILATIONS if d != 1):
        blk_bits = (n_units // dil).bit_length() - 1

        def several_units(g, _, slab=slab, dil=dil, blk_bits=blk_bits):
            for u in range(DIL_UNITS_PER_BODY):
                t = g * DIL_UNITS_PER_BODY + u
                rho = lax.shift_right_logical(t, blk_bits)
                n = t & ((1 << blk_bits) - 1)
                o_n, mu = unit(dil, rho, n)
                q0 = rho + dil * BLOCK * n
                onorm_s[slab, pl.ds(q0, BLOCK, stride=dil), :] = o_n
                mu_s[slab, pl.ds(q0, BLOCK, stride=dil), :] = mu
            return 0

        lax.fori_loop(0, n_units // DIL_UNITS_PER_BODY, several_units, 0)

    def merge_blocks(g, _):
        for u in range(DIL_UNITS_PER_BODY):
            n = g * DIL_UNITS_PER_BODY + u
            o_1, mu_1 = unit(1, 0, n)
            r0 = pl.multiple_of(n * BLOCK, BLOCK)
            others = [(onorm_s[sl, pl.ds(r0, BLOCK), :], mu_s[sl, pl.ds(r0, BLOCK), :])
                      for sl in range(len(DILATIONS) - 1)]
            top = functools.reduce(jnp.maximum, [mu_1] + [mu for _, mu in others])
            w = jnp.exp2(mu_1 - top)
            num, den = w * o_1, w
            for o_p, mu_p in others:
                w = jnp.exp2(mu_p - top)
                num = num + w * o_p
                den = den + w
            o_ref[0, pl.ds(r0, BLOCK), :] = (num / den).astype(o_ref.dtype)
        return 0

    lax.fori_loop(0, n_units // DIL_UNITS_PER_BODY, merge_blocks, 0)


def _dilated_attention(qd, kd, vd):
    B, S, _ = qd.shape
    blk = pl.BlockSpec((1, S, LANES), lambda b, p: (b, 0, p))
    n_slabs = len(DILATIONS) - 1
    skewed = S // SKEW_DIL * SKEW_PITCH
    return pl.pallas_call(
        _dil_kernel,
        grid=(B, N_PAIRS_DIL),
        in_specs=[blk, blk, blk, pl.BlockSpec((2, BLOCK, 2 * BLOCK), lambda b, p: (0, 0, 0))],
        out_specs=blk,
        out_shape=jax.ShapeDtypeStruct((B, S, D_DIL), MXU_DTYPE),
        scratch_shapes=[
            pltpu.VMEM((S, LANES), F32),
            pltpu.VMEM((DIL_PAD + S, LANES), F32),
            pltpu.VMEM((DIL_PAD + S, LANES), F32),
            pltpu.VMEM((skewed, LANES), F32),
            pltpu.VMEM((SKEW_PAD + skewed, LANES), F32),
            pltpu.VMEM((SKEW_PAD + skewed, LANES), F32),
            pltpu.VMEM((n_slabs, S, LANES), F32),
            pltpu.VMEM((n_slabs, S, LANES), F32),
        ],
        compiler_params=_cparams(2),
        name="dilated_attention",
    )(qd, kd, vd, jnp.asarray(_band_bias()))


def _layer_norm(u, g, b):
    mu = jnp.mean(u, axis=1, keepdims=True)
    d = u - mu
    var = jnp.mean(d * d, axis=1, keepdims=True)
    return d * lax.rsqrt(var + LN_EPS) * g + b


def _rms_norm(x, g):
    ms = jnp.mean(x * x, axis=1, keepdims=True)
    return x * lax.rsqrt(ms + RMS_EPS) * g


def _top2_of4(vals):
    v1, i1 = vals[0], jnp.zeros(vals[0].shape, jnp.int32)
    for i in range(1, 4):
        better = vals[i] > v1
        v1 = jnp.where(better, vals[i], v1)
        i1 = jnp.where(better, i, i1)
    v2 = jnp.full(vals[0].shape, -1.0, F32)
    i2 = jnp.zeros(vals[0].shape, jnp.int32)
    for i in range(4):
        better = (vals[i] > v2) & (i1 != i)
        v2 = jnp.where(better, vals[i], v2)
        i2 = jnp.where(better, i, i2)
    return v1, i1, v2, i2


def _outproj_kernel(od_ref, of_ref, h_ref, wo_ref, gd_ref, gf_ref, lng_ref, lnb_ref,
                    wr_ref, br_ref, tri_ref,
                    h1_ref, h1t_ref, e_ref, gate_ref, rank_ref, cnt_ref, base_ref):
    step = pl.program_id(0)

    @pl.when(step == 0)
    def _():
        base_ref[...] = jnp.zeros_like(base_ref)

    xd = _rms_norm(od_ref[...].astype(F32), gd_ref[...])
    xf = _rms_norm(of_ref[...].astype(F32), gf_ref[...])
    y = _dot(xd.astype(MXU_DTYPE), wo_ref[0:D_DIL, :]) + _dot(xf.astype(MXU_DTYPE), wo_ref[D_DIL:, :])
    h1 = _layer_norm(DEEPNORM_ALPHA * h_ref[...] + y, lng_ref[...], lnb_ref[...])
    h1_ref[...] = h1
    _store_token_tiles(h1t_ref, h1)
    tm = h1.shape[0]

    h_hi, h_mid, _ = _split3(h1)
    two = _dot_nt(wr_ref[...], h_hi)
    logits = (two[:N_EXPERTS] + two[N_EXPERTS:]) + _dot_nt(wr_ref[0:N_EXPERTS, :], h_mid) + br_ref[...]
    logits = logits - jnp.max(logits, axis=0, keepdims=True)
    ex = jnp.exp(logits)
    probs = ex / jnp.sum(ex, axis=0, keepdims=True)
    pr = [probs[j:j + 1, :] for j in range(N_EXPERTS)]

    def group_score(g):
        v = pr[4 * g:4 * g + 4]
        pairs = [v[a] + v[b] for a in range(4) for b in range(a + 1, 4)]
        return functools.reduce(jnp.maximum, pairs)

    best = group_score(0)
    gsel = jnp.zeros((1, tm), jnp.int32)
    for g in range(1, N_GROUPS):
        sc = group_score(g)
        better = sc > best
        best = jnp.where(better, sc, best)
        gsel = jnp.where(better, g, gsel)
    in_grp = []
    for i in range(EXPERTS_PER_GROUP):
        v = pr[i]
        for g in range(1, N_GROUPS):
            v = jnp.where(gsel == g, pr[4 * g + i], v)
        in_grp.append(v)
    v1, i1, v2, i2 = _top2_of4(in_grp)
    e1 = gsel * EXPERTS_PER_GROUP + i1
    e2 = gsel * EXPERTS_PER_GROUP + i2
    den = v1 + v2
    e_ref[...] = jnp.concatenate([e1, e2], axis=0)
    gate_ref[...] = jnp.concatenate([v1 / den, v2 / den], axis=0)

    eidx = lax.broadcasted_iota(jnp.int32, (N_EXPERTS, tm), 0)
    oh1 = (eidx == e1).astype(F32)
    oh2 = (eidx == e2).astype(F32)
    tot = oh1 + oh2
    before = base_ref[...] + _dot(tot.astype(jnp.bfloat16), tri_ref[...])
    r1 = jnp.sum(oh1 * before, axis=0, keepdims=True)
    r2 = jnp.sum(oh2 * before, axis=0, keepdims=True)
    rank_ref[...] = jnp.concatenate([r1, r2], axis=0).astype(jnp.int32)
    base_ref[...] = base_ref[...] + jnp.sum(tot, axis=1, keepdims=True)
    cnt_ref[...] = jnp.broadcast_to(base_ref[...], cnt_ref.shape)


def _out_projection(od, of, h, w_out, g_dil, g_fox, ln_g, ln_b, wr3, b_router, tm=1024):
    T = h.shape[0]
    tri = (np.arange(tm)[:, None] < np.arange(tm)[None, :]).astype(np.float32)
    rows = lambda width: pl.BlockSpec((tm, width), lambda i: (i, 0))
    full = lambda a: pl.BlockSpec(a.shape, lambda i: (0,) * a.ndim)
    tok = pl.BlockSpec((TOP_K, tm), lambda i: (0, i))
    consts = [w_out, g_dil.reshape(1, -1), g_fox.reshape(1, -1), ln_g.reshape(1, -1), ln_b.reshape(1, -1),
              wr3, b_router.reshape(-1, 1).astype(F32), jnp.asarray(tri, jnp.bfloat16)]
    return pl.pallas_call(
        _outproj_kernel,
        grid=(T // tm,),
        in_specs=[rows(D_DIL), rows(D_FOX), rows(D_MODEL)] + [full(a) for a in consts],
        out_specs=(rows(D_MODEL), pl.BlockSpec((tm * TOKEN_TILE_ROWS, LANES), lambda i: (i, 0)),
                   tok, tok, tok, pl.BlockSpec((N_EXPERTS, LANES), lambda i: (0, 0))),
        out_shape=(
            jax.ShapeDtypeStruct((T, D_MODEL), F32),
            jax.ShapeDtypeStruct((T * TOKEN_TILE_ROWS, LANES), F32),
            jax.ShapeDtypeStruct((TOP_K, T), jnp.int32),
            jax.ShapeDtypeStruct((TOP_K, T), F32),
            jax.ShapeDtypeStruct((TOP_K, T), jnp.int32),
            jax.ShapeDtypeStruct((N_EXPERTS, LANES), F32),
        ),
        scratch_shapes=[pltpu.VMEM((N_EXPERTS, 1), F32)],
        compiler_params=_cparams(1),
        name="out_projection_router",
    )(od, of, h, *consts)


def _store_token_tiles(ref, x):
    n = x.shape[0]
    for c in range(TOKEN_TILE_ROWS):
        ref[pl.ds(c, n, stride=TOKEN_TILE_ROWS), :] = x[:, c * LANES:(c + 1) * LANES]


def _load_token_tiles(ref, n, first=0):
    return jnp.concatenate([ref[pl.ds(first * TOKEN_TILE_ROWS + c, n, stride=TOKEN_TILE_ROWS), :]
                            for c in range(TOKEN_TILE_ROWS)], axis=1)


def _tile_copy(src, src_row, dst, dst_row, sem):
    return pltpu.make_async_copy(src.at[pl.ds(src_row, TOKEN_TILE_ROWS)],
                                 dst.at[pl.ds(dst_row, TOKEN_TILE_ROWS)], sem)


def _dispatch_kernel(pc0_ref, pc1_ref, pp0_ref, pp1_ref, pad0_ref, padn_ref, h_hbm, xg_out,
                     hbuf, sem, sem_load, ztile, *, td):
    j = pl.program_id(0)
    n_steps = pl.num_programs(0)
    tile_rows = td * TOKEN_TILE_ROWS

    def load(tile, slot):
        src = pl.multiple_of(tile * tile_rows, tile_rows)
        return pltpu.make_async_copy(h_hbm.at[pl.ds(src, tile_rows)], hbuf.at[slot], sem_load.at[slot])

    def scatter(p0_ref, p1_ref, slot, which):
        def copies(t):
            src = pl.multiple_of(t * TOKEN_TILE_ROWS, TOKEN_TILE_ROWS)
            return (_tile_copy(hbuf.at[slot], src, xg_out, pl.multiple_of(p0_ref[t], TOKEN_TILE_ROWS), sem.at[which]),
                    _tile_copy(hbuf.at[slot], src, xg_out, pl.multiple_of(p1_ref[t], TOKEN_TILE_ROWS), sem.at[which]))

        def issue(t, _):
            for thread, cp in enumerate(copies(t)):
                cp.start(priority=thread)
            return 0

        def drain(t, _):
            for cp in copies(t):
                cp.wait()
            return 0

        def issue_all():
            lax.fori_loop(0, td, issue, 0, unroll=DMA_ISSUE_UNROLL)

        def wait_all():
            lax.fori_loop(0, td, drain, 0, unroll=DMA_ISSUE_UNROLL)

        return issue_all, wait_all

    @pl.when(j == 0)
    def _():
        load(0, 0).start()
        load(1, 1).start()

    load(j, j % 3).wait()
    issue_cur, wait_cur = scatter(pc0_ref, pc1_ref, j % 3, j % 2)
    issue_cur()

    @pl.when(j >= 1)
    def _():
        scatter(pp0_ref, pp1_ref, (j - 1) % 3, (j - 1) % 2)[1]()

    @pl.when(j + 2 < n_steps)
    def _():
        load(j + 2, (j + 2) % 3).start()

    pl.when(j == n_steps - 1)(wait_cur)

    @pl.when(pl.program_id(0) == pl.num_programs(0) - 1)
    def _():
        ztile[...] = jnp.zeros(ztile.shape, ztile.dtype)

        chunk_rows = ztile.shape[0]
        chunk_tokens = chunk_rows // TOKEN_TILE_ROWS

        def for_each_zero_copy(action):
            def hole(e, _):
                n_chunks = padn_ref[e] // chunk_tokens

                def chunk(c, _):
                    dst = pl.multiple_of(pad0_ref[e] + c * chunk_rows, TOKEN_TILE_ROWS)
                    action(pltpu.make_async_copy(ztile, xg_out.at[pl.ds(dst, chunk_rows)], sem.at[0]))
                    return 0

                def single(r, _):
                    dst = pl.multiple_of(pad0_ref[e] + r * TOKEN_TILE_ROWS, TOKEN_TILE_ROWS)
                    action(_tile_copy(ztile, 0, xg_out, dst, sem.at[0]))
                    return 0

                lax.fori_loop(0, n_chunks, chunk, 0)
                lax.fori_loop(n_chunks * chunk_tokens, padn_ref[e], single, 0)
                return 0
            lax.fori_loop(0, pad0_ref.shape[0], hole, 0)

        for_each_zero_copy(lambda cp: cp.start())
        for_each_zero_copy(lambda cp: cp.wait())


def _dispatch(h1t, pos_rows, pad_rows, pad_counts, n_rows, td=256):
    T = h1t.shape[0] // TOKEN_TILE_ROWS
    assert T // td >= 2
    cur = pl.BlockSpec((td,), lambda i: (i,), memory_space=pltpu.SMEM)
    prev = pl.BlockSpec((td,), lambda i: (jnp.maximum(i - 1, 0),), memory_space=pltpu.SMEM)
    smem_all = pl.BlockSpec(memory_space=pltpu.SMEM)
    anyspec = pl.BlockSpec(memory_space=pl.ANY)
    return pl.pallas_call(
        functools.partial(_dispatch_kernel, td=td),
        grid=(T // td,),
        in_specs=[cur, cur, prev, prev, smem_all, smem_all, anyspec],
        out_specs=anyspec,
        out_shape=jax.ShapeDtypeStruct((n_rows * TOKEN_TILE_ROWS, LANES), F32),
        scratch_shapes=[pltpu.VMEM((3, td * TOKEN_TILE_ROWS, LANES), F32),
                        pltpu.SemaphoreType.DMA((2,)), pltpu.SemaphoreType.DMA((3,)),
                        pltpu.VMEM((ZERO_CHUNK_TOKENS * TOKEN_TILE_ROWS, LANES), F32)],
        compiler_params=_cparams(1),
        name="moe_dispatch",
    )(pos_rows[0], pos_rows[1], pos_rows[0], pos_rows[1], pad_rows, pad_counts, h1t)


def _ffn_kernel(be_ref, nv_ref, slot_ref, nxt_ref, x_ref, wg_hbm, wu_hbm, wd_hbm, y_ref, wstage, wb, sem,
                *, tmf, layer):
    j = pl.program_id(0)
    valid = j < nv_ref[0]

    def fetch(expert, slot):
        return [pltpu.make_async_copy(w.at[layer, expert], wstage.at[slot, k], sem.at[slot])
                for k, w in enumerate((wg_hbm, wu_hbm, wd_hbm))]

    @pl.when(j == 0)
    def _():
        for cp in fetch(be_ref[0], 0):
            cp.start()

    @pl.when(valid & ((j == 0) | (be_ref[j] != be_ref[jnp.maximum(j - 1, 0)])))
    def _():
        slot = slot_ref[j]
        for cp in fetch(be_ref[j], slot):
            cp.wait()
        for k in range(3):
            wb[k] = wstage[slot, k].astype(MXU_DTYPE)

        @pl.when(nxt_ref[j] >= 0)
        def _():
            for cp in fetch(nxt_ref[j], 1 - slot):
                cp.start()

    @pl.when(valid)
    def _():
        xb = _load_token_tiles(x_ref, tmf).astype(MXU_DTYPE)
        a = _dot(xb, wb[0])
        u = _dot(xb, wb[1])
        hmid = (a * jax.nn.sigmoid(a)) * u
        _store_token_tiles(y_ref, _dot(hmid.astype(MXU_DTYPE), wb[2]))

    @pl.when(j >= nv_ref[0])
    def _():
        y_ref[...] = jnp.zeros_like(y_ref)


def _expert_ffn(xg, blk_e, n_valid, present, wg, wu, wd, layer, tmf):
    n_rows = xg.shape[0] // TOKEN_TILE_ROWS
    D = D_MODEL
    nb = n_rows // tmf
    eids = jnp.arange(N_EXPERTS, dtype=jnp.int32)
    order = jnp.cumsum(present.astype(jnp.int32)) - 1
    later = (eids[None, :] > eids[:, None]) & present[None, :]
    nxt_e = jnp.min(jnp.where(later, eids[None, :], N_EXPERTS), axis=1)
    nxt_e = jnp.where(nxt_e == N_EXPERTS, -1, nxt_e)
    slot = (order & 1)[blk_e].astype(jnp.int32)
    nxt = nxt_e[blk_e].astype(jnp.int32)
    xmap = lambda j, be, nv, sl, nx: (jnp.minimum(j, nv[0] - 1), 0)
    anyspec = pl.BlockSpec(memory_space=pl.ANY)
    grid_spec = pltpu.PrefetchScalarGridSpec(
        num_scalar_prefetch=4,
        grid=(nb,),
        in_specs=[pl.BlockSpec((tmf * TOKEN_TILE_ROWS, LANES), xmap), anyspec, anyspec, anyspec],
        out_specs=pl.BlockSpec((tmf * TOKEN_TILE_ROWS, LANES), lambda j, be, nv, sl, nx: (j, 0)),
        scratch_shapes=[pltpu.VMEM((2, 3, D, D), F32), pltpu.VMEM((3, D, D), MXU_DTYPE),
                        pltpu.SemaphoreType.DMA((2,))],
    )
    return pl.pallas_call(
        functools.partial(_ffn_kernel, tmf=tmf, layer=layer),
        grid_spec=grid_spec,
        out_shape=jax.ShapeDtypeStruct(xg.shape, F32),
        compiler_params=_cparams(1),
        name="moe_expert_ffn",
    )(blk_e, n_valid, slot, nxt, xg, wg, wu, wd)


def _combine_kernel(pc0_ref, pc1_ref, pn0_ref, pn1_ref, gate_ref, h_ref, lng_ref, lnb_ref, y_hbm, o_ref,
                    buf_a0, buf_a1, buf_b0, buf_b1, sem, *, tc):
    j = pl.program_id(0)

    def tile(p0_ref, p1_ref, first, bufs, which):
        def copies(t):
            dst = pl.multiple_of(t * TOKEN_TILE_ROWS, TOKEN_TILE_ROWS)
            return (_tile_copy(y_hbm, pl.multiple_of(p0_ref[first + t], TOKEN_TILE_ROWS), bufs[0], dst, sem.at[which]),
                    _tile_copy(y_hbm, pl.multiple_of(p1_ref[first + t], TOKEN_TILE_ROWS), bufs[1], dst, sem.at[which]))

        def issue(t, _):
            for thread, cp in enumerate(copies(t)):
                cp.start(priority=thread)
            return 0

        def drain(t, _):
            for cp in copies(t):
                cp.wait()
            return 0

        def issue_all():
            lax.fori_loop(0, tc, issue, 0, unroll=DMA_ISSUE_UNROLL)

        def wait_all():
            lax.fori_loop(0, tc, drain, 0, unroll=DMA_ISSUE_UNROLL)

        return issue_all, wait_all

    def finish(first, bufs):
        g = gate_ref[first:first + tc, :]
        y = _load_token_tiles(bufs[0], tc) * g[:, 0:1] + _load_token_tiles(bufs[1], tc) * g[:, 1:2]
        o_ref[first:first + tc, :] = _layer_norm(DEEPNORM_ALPHA * h_ref[first:first + tc, :] + y,
                                                 lng_ref[...], lnb_ref[...])

    issue_a, wait_a = tile(pc0_ref, pc1_ref, 0, (buf_a0, buf_a1), 0)
    issue_b, wait_b = tile(pc0_ref, pc1_ref, tc, (buf_b0, buf_b1), 1)
    issue_next_a, _ = tile(pn0_ref, pn1_ref, 0, (buf_a0, buf_a1), 0)

    pl.when(j == 0)(issue_a)
    issue_b()
    wait_a()
    finish(0, (buf_a0, buf_a1))
    pl.when(j + 1 < pl.num_programs(0))(issue_next_a)
    wait_b()
    finish(tc, (buf_b0, buf_b1))


def _combine(yg, pos_rows, gates_col, h1, ln_g, ln_b, tc=256):
    T, D = h1.shape
    n_tiles = T // tc
    cur = pl.BlockSpec((2 * tc,), lambda j: (j,), memory_space=pltpu.SMEM)
    nxt = pl.BlockSpec((tc,), lambda j: (jnp.minimum(2 * j + 2, n_tiles - 1),), memory_space=pltpu.SMEM)
    rows = lambda width: pl.BlockSpec((2 * tc, width), lambda j: (j, 0))
    vec = pl.BlockSpec((1, D), lambda j: (0, 0))
    buf = pltpu.VMEM((tc * TOKEN_TILE_ROWS, LANES), F32)
    return pl.pallas_call(
        functools.partial(_combine_kernel, tc=tc),
        grid=(n_tiles // 2,),
        in_specs=[cur, cur, nxt, nxt, rows(TOP_K), rows(D), vec, vec, pl.BlockSpec(memory_space=pl.ANY)],
        out_specs=rows(D),
        out_shape=jax.ShapeDtypeStruct((T, D), F32),
        scratch_shapes=[buf, buf, buf, buf, pltpu.SemaphoreType.DMA((2,))],
        compiler_params=_cparams(1),
        name="moe_combine_ln",
    )(pos_rows[0], pos_rows[1], pos_rows[0], pos_rows[1], gates_col, h1,
      ln_g.reshape(1, -1), ln_b.reshape(1, -1), yg)


def _grouped_moe(h1, h1t, experts, gates, ranks, counts, wg, wu, wd, layer, ln_g, ln_b, tmf=512):
    T = h1.shape[0]
    nb = (T * TOP_K) // tmf + N_EXPERTS
    cnt = counts[:, 0].astype(jnp.int32)
    pcnt = (cnt + tmf - 1) // tmf * tmf
    pends = jnp.cumsum(pcnt)
    pstart = pends - pcnt
    eids = jnp.arange(N_EXPERTS, dtype=jnp.int32)
    seg = jnp.sum(jnp.where(experts[None] == eids[:, None, None], pstart[:, None, None], 0), axis=0)
    pos_rows = (seg + ranks) * TOKEN_TILE_ROWS
    blk_start = jnp.arange(nb, dtype=jnp.int32) * tmf
    blk_e = jnp.minimum(jnp.sum((pends[None, :] <= blk_start[:, None]).astype(jnp.int32), axis=1), N_EXPERTS - 1)
    n_valid = (pends[-1:] // tmf).astype(jnp.int32)
    hole_start = jnp.concatenate([pstart + cnt, pends[-1:]])
    hole_rows = jnp.concatenate([pcnt - cnt, nb * tmf - pends[-1:]])
    xg = _dispatch(h1t, pos_rows, hole_start * TOKEN_TILE_ROWS, hole_rows, nb * tmf)
    yg = _expert_ffn(xg, blk_e, n_valid, cnt > 0, wg, wu, wd, layer, tmf)
    return _combine(yg, pos_rows, gates.T, h1, ln_g, ln_b)


def _rope_tables(S):
    half = HEAD_DIM // 2
    inv = ROPE_THETA ** (-jnp.arange(half, dtype=F32) / half)
    ang = jnp.arange(S, dtype=F32)[:, None] * inv[None, :]
    cos, sin = jnp.cos(ang), jnp.sin(ang)
    reps = LANES // HEAD_DIM
    cos_tab = jnp.tile(jnp.concatenate([cos, cos], axis=1), (1, reps))
    sin_tab = jnp.tile(jnp.concatenate([-sin, sin], axis=1), (1, reps))
    return cos_tab, sin_tab


def _pad_w_in(w):
    pad = jnp.zeros((D_MODEL, D_PROJ_PAD - w.shape[1]), w.dtype)
    return jnp.concatenate([w, pad], axis=1).astype(MXU_DTYPE)


def kernel(x, w_in, b_forget, g_dil, g_fox, w_out, ln1_g, ln1_b, w_router, b_router,
           w_gate, w_up, w_down, ln2_g, ln2_b):
    B, S, D = x.shape
    T = B * S
    cos_tab, sin_tab = _rope_tables(S)
    wr_hi, wr_mid, _ = _split3(w_router.astype(F32).T)
    wr3 = jnp.concatenate([wr_hi, wr_mid], axis=0)
    h = x
    for l in range(DEPTH):
        qd, kd, vd, qf, kf, vf, zf = _in_projection(h, _pad_w_in(w_in[l]), cos_tab, sin_tab)
        c = _forget_cumsum(zf, b_forget[l])[:, :N_HEADS_FOX]
        o_fox = _fox_attention(qf, kf, vf, c)
        o_dil = _dilated_attention(qd, kd, vd)
        h1, h1t, experts, gates, ranks, counts = _out_projection(
            o_dil.reshape(T, D_DIL), o_fox.reshape(T, D_FOX), h.reshape(T, D),
            w_out[l].astype(MXU_DTYPE), g_dil[l], g_fox[l], ln1_g[l], ln1_b[l], wr3, b_router)
        h2 = _grouped_moe(h1, h1t, experts, gates, ranks, counts,
                          w_gate, w_up, w_down, l,
                          ln2_g[l], ln2_b[l])
        h = h2.reshape(B, S, D)
    return h
```

```python
import functools

import numpy as np
import jax
import jax.numpy as jnp
from jax import lax
from jax.experimental import pallas as pl
from jax.experimental.pallas import tpu as pltpu

D_MODEL = 1024
DEPTH = 2
HEAD_DIM = 64
N_HEADS_DIL = 12
N_HEADS_FOX = 4
D_DIL = N_HEADS_DIL * HEAD_DIM
D_FOX = N_HEADS_FOX * HEAD_DIM
DILATIONS = (1, 4, 16)
BLOCK = 128
ROPE_THETA = 10000.0
N_EXPERTS = 16
N_GROUPS = 4
EXPERTS_PER_GROUP = 4
TOP_K = 2
DEEPNORM_ALPHA = (2.0 * DEPTH) ** 0.25
LN_EPS = 1e-5
RMS_EPS = 1e-6

LANES = 128
N_PAIRS_DIL = D_DIL // LANES
D_PROJ_PAD = 3 * D_DIL + 3 * D_FOX + LANES
VMEM_LIMIT = 48 * 1024 * 1024
TOKEN_TILE_ROWS = D_MODEL // LANES
DMA_ISSUE_UNROLL = 8
ZERO_CHUNK_TOKENS = 32
SKEW_DIL = 16
SKEW_PITCH = SKEW_DIL + 1
DIL_PAD = max(d for d in DILATIONS if d != SKEW_DIL) * BLOCK
SKEW_PAD = SKEW_PITCH * BLOCK
MASK_BIAS = -1e30
LOG2_E = 1.4426950408889634
DIL_UNITS_PER_BODY = 32
FOX_TK = 256
FOX_BLOCKS_PER_BODY = 8
ZF_ROWS = 8

MXU_DTYPE = jnp.bfloat16
F32 = jnp.float32
NEG_INF = float("-inf")


def _cparams(n_axes):
    return pltpu.CompilerParams(dimension_semantics=("arbitrary",) * n_axes,
                                vmem_limit_bytes=VMEM_LIMIT)


def _dot(a, b):
    return jnp.dot(a, b, preferred_element_type=F32)


def _dot_nt(a, b):
    return lax.dot_general(a, b, (((1,), (1,)), ((), ())), preferred_element_type=F32)


def _inproj_kernel(x_ref, w_ref, cos_ref, sin_ref,
                   qd_ref, kd_ref, vd_ref, qf_ref, kf_ref, vf_ref, zf_ref):
    xb = x_ref[0].astype(MXU_DTYPE)
    tm = xb.shape[0]
    cos = cos_ref[...]
    sin = sin_ref[...]
    lane = lax.broadcasted_iota(jnp.int32, (tm, LANES), 1)
    first_half = (lane % HEAD_DIM) < (HEAD_DIM // 2)

    def rope(z):
        rot = jnp.where(first_half, pltpu.roll(z, LANES - 32, 1), pltpu.roll(z, 32, 1))
        return z * cos + rot * sin

    def proj(col, width):
        return _dot(xb, w_ref[:, col:col + width])

    scale = HEAD_DIM ** -0.5
    col = 0
    for out_ref, roped, mul in ((qd_ref, True, scale * LOG2_E), (kd_ref, True, 1.0)):
        for c in range(D_DIL // 256):
            z = proj(col, 256)
            for half in range(2):
                zz = rope(z[:, half * LANES:(half + 1) * LANES]) * mul
                out_ref[0, :, c * 256 + half * LANES:c * 256 + (half + 1) * LANES] = zz.astype(out_ref.dtype)
            col += 256
    for c in range(D_DIL // 256):
        vd_ref[0, :, c * 256:(c + 1) * 256] = proj(col, 256).astype(vd_ref.dtype)
        col += 256
    qf_ref[0] = (proj(col, D_FOX) * (scale * LOG2_E)).astype(qf_ref.dtype)
    col += D_FOX
    kf_ref[0] = proj(col, D_FOX).astype(kf_ref.dtype)
    col += D_FOX
    vf_ref[0] = proj(col, D_FOX).astype(vf_ref.dtype)
    col += D_FOX
    zf_ref[0] = proj(col, LANES).T[0:ZF_ROWS, :]


def _in_projection(h, w_pad, cos_tab, sin_tab, tm=1024):
    B, S, D = h.shape
    grid = (B, S // tm)
    row = lambda width: pl.BlockSpec((1, tm, width), lambda b, i: (b, i, 0))
    out_shape = (
        jax.ShapeDtypeStruct((B, S, D_DIL), MXU_DTYPE),
        jax.ShapeDtypeStruct((B, S, D_DIL), MXU_DTYPE),
        jax.ShapeDtypeStruct((B, S, D_DIL), MXU_DTYPE),
        jax.ShapeDtypeStruct((B, S, D_FOX), MXU_DTYPE),
        jax.ShapeDtypeStruct((B, S, D_FOX), MXU_DTYPE),
        jax.ShapeDtypeStruct((B, S, D_FOX), MXU_DTYPE),
        jax.ShapeDtypeStruct((B, ZF_ROWS, S), F32),
    )
    return pl.pallas_call(
        _inproj_kernel,
        grid=grid,
        in_specs=[
            row(D),
            pl.BlockSpec((D, D_PROJ_PAD), lambda b, i: (0, 0)),
            pl.BlockSpec((tm, LANES), lambda b, i: (i, 0)),
            pl.BlockSpec((tm, LANES), lambda b, i: (i, 0)),
        ],
        out_specs=(row(D_DIL), row(D_DIL), row(D_DIL), row(D_FOX), row(D_FOX), row(D_FOX),
                   pl.BlockSpec((1, ZF_ROWS, tm), lambda b, i: (b, 0, i))),
        out_shape=out_shape,
        compiler_params=_cparams(2),
        name="in_projection",
    )(h, w_pad, cos_tab, sin_tab)


def _split3(x):
    hi = x.astype(jnp.bfloat16)
    r1 = x - hi.astype(F32)
    mid = r1.astype(jnp.bfloat16)
    lo = (r1 - mid.astype(F32)).astype(jnp.bfloat16)
    return hi, mid, lo


def _forget_cumsum_kernel(z_ref, b_ref, tri_ref, c_ref):
    x = z_ref[0] + b_ref[...]
    logf = jnp.minimum(x, 0.0) - jnp.log1p(jnp.exp(-jnp.abs(x)))
    tri = tri_ref[...]
    S = x.shape[1]
    carry = jnp.zeros((x.shape[0], 1), F32)
    for blk in range(S // LANES):
        seg = logf[:, blk * LANES:(blk + 1) * LANES]
        hi, mid, lo = _split3(seg)
        cs = (_dot(hi, tri) + _dot(mid, tri)) + _dot(lo, tri) + carry
        c_ref[0, :, blk * LANES:(blk + 1) * LANES] = cs * LOG2_E
        carry = cs[:, LANES - 1:LANES]


def _forget_cumsum(zt, b_forget):
    B, Hf, S = zt.shape
    tri = (np.arange(LANES)[:, None] <= np.arange(LANES)[None, :]).astype(np.float32)
    return pl.pallas_call(
        _forget_cumsum_kernel,
        grid=(B,),
        in_specs=[
            pl.BlockSpec((1, Hf, S), lambda b: (b, 0, 0)),
            pl.BlockSpec((Hf, 1), lambda b: (0, 0)),
            pl.BlockSpec((LANES, LANES), lambda b: (0, 0)),
        ],
        out_specs=pl.BlockSpec((1, Hf, S), lambda b: (b, 0, 0)),
        out_shape=jax.ShapeDtypeStruct((B, Hf, S), F32),
        compiler_params=_cparams(1),
        name="forget_cumsum",
    )(zt, jnp.pad(b_forget.astype(F32), (0, Hf - b_forget.shape[0])).reshape(Hf, 1), jnp.asarray(tri, jnp.bfloat16))


def _pair_rows(xp, is_a):
    zero = jnp.zeros_like(xp)
    return jnp.concatenate([jnp.where(is_a, xp, zero), jnp.where(is_a, zero, xp)], axis=0)


def _causal_bias():
    qi = np.arange(BLOCK)[:, None]
    ki = np.arange(FOX_TK)[None, :]
    vis = np.stack([ki <= qi, ki <= qi + BLOCK])
    return np.where(vis, 0.0, MASK_BIAS).astype(np.float32)


def _fox_kernel(q_ref, k_ref, v_ref, crow_ref, cbias_ref, o_ref):
    S = q_ref.shape[1]
    tk = FOX_TK
    lane = lax.broadcasted_iota(jnp.int32, (BLOCK, LANES), 1)
    is_a = lane < HEAD_DIM
    ones = jnp.ones((tk, LANES), MXU_DTYPE)

    def unit(q2, kv, bias):
        kb, v_ones, cr = kv
        s = _dot_nt(q2, kb)
        sa = s[:BLOCK] - cr[0:1]
        sb = s[BLOCK:] - cr[1:2]
        if bias is not None:
            sa = sa + bias
            sb = sb + bias
        s = jnp.concatenate([sa, sb], axis=0)
        m = jnp.max(s, axis=1, keepdims=True)
        p = jnp.exp2(s - m).astype(MXU_DTYPE)
        o = _dot(p, v_ones)
        acc = jnp.where(is_a, o[:BLOCK, :LANES], o[BLOCK:, :LANES])
        l = jnp.where(is_a, o[:BLOCK, LANES:], o[BLOCK:, LANES:])
        return acc, l, jnp.where(is_a, m[:BLOCK], m[BLOCK:])

    def key_block(j):
        k0 = pl.multiple_of(j * tk, tk)
        return (k_ref[0, pl.ds(k0, tk), :],
                jnp.concatenate([v_ref[0, pl.ds(k0, tk), :], ones], axis=1),
                crow_ref[0, 0, :, pl.ds(k0, tk)])

    def fold(state, units):
        num, den, top_old = state
        top = functools.reduce(jnp.maximum, [top_old] + [m for _, _, m in units])
        scale = jnp.exp2(top_old - top)
        num, den = scale * num, scale * den
        for acc, l, m in units:
            w = jnp.exp2(m - top)
            num = num + w * acc
            den = den + w * l
        return num, den, top

    def q_block(i, _):
        r0 = pl.multiple_of(i * tk, tk)
        q2 = [_pair_rows(q_ref[0, pl.ds(r0 + h * BLOCK, BLOCK), :], is_a) for h in range(2)]
        zero = jnp.zeros((BLOCK, LANES), F32)
        empty = (zero, zero, jnp.full((BLOCK, LANES), MASK_BIAS, F32))

        def fold_blocks(states, blocks):
            units = [[unit(q2[h], kv, None if bias is None else bias[h]) for h in range(2)]
                     for kv, bias in ((key_block(j), bias) for j, bias in blocks)]
            return tuple(fold(states[h], [u[h] for u in units]) for h in range(2))

        def full_blocks(g, states):
            return fold_blocks(states, [(FOX_BLOCKS_PER_BODY * g + u, None) for u in range(FOX_BLOCKS_PER_BODY)])

        states = lax.fori_loop(0, i // FOX_BLOCKS_PER_BODY, full_blocks, (empty, empty))
        diag = (i, (cbias_ref[0], cbias_ref[1]))

        def tail(n_left):
            return lambda st: fold_blocks(st, [(i - n_left + u, None) for u in range(n_left)] + [diag])

        states = lax.switch(i % FOX_BLOCKS_PER_BODY, [tail(r) for r in range(FOX_BLOCKS_PER_BODY)], states)
        for h in range(2):
            num, den, _ = states[h]
            o_ref[0, pl.ds(r0 + h * BLOCK, BLOCK), :] = (num / den).astype(o_ref.dtype)
        return 0

    lax.fori_loop(0, S // tk, q_block, 0)


def _fox_attention(qf, kf, vf, c2):
    B, S, _ = qf.shape
    n_pairs = D_FOX // LANES
    c4 = c2.reshape(B, n_pairs, 2, S)
    blk = pl.BlockSpec((1, S, LANES), lambda b, p: (b, 0, p))
    return pl.pallas_call(
        _fox_kernel,
        grid=(B, n_pairs),
        in_specs=[blk, blk, blk,
                  pl.BlockSpec((1, 1, 2, S), lambda b, p: (b, p, 0, 0)),
                  pl.BlockSpec((2, BLOCK, FOX_TK), lambda b, p: (0, 0, 0))],
        out_specs=blk,
        out_shape=jax.ShapeDtypeStruct((B, S, D_FOX), MXU_DTYPE),
        compiler_params=_cparams(2),
        name="fox_attention",
    )(qf, kf, vf, c4, jnp.asarray(_causal_bias()))


def _band_bias():
    qi = np.arange(BLOCK)[:, None]
    ki = np.arange(2 * BLOCK)[None, :]
    delta = BLOCK + qi - ki
    band = (delta >= 0) & (delta <= BLOCK)
    first = band & (ki >= BLOCK)
    return np.where(np.stack([first, band]), 0.0, MASK_BIAS).astype(np.float32)


def _dil_kernel(q_ref, k_ref, v_ref, bias_ref, o_ref, q32, k32, v32, q17, k17, v17, onorm_s, mu_s):
    S = q_ref.shape[1]
    q32[...] = q_ref[0].astype(F32)
    zeros = jnp.zeros((SKEW_PAD, LANES), F32)
    for plain, skewed, src in ((k32, k17, k_ref), (v32, v17, v_ref)):
        plain[0:DIL_PAD, :] = zeros[0:DIL_PAD]
        skewed[0:SKEW_PAD, :] = zeros
        plain[DIL_PAD:, :] = src[0].astype(F32)

    def skew_rows(g8, _):
        for u in range(8):
            src = pl.multiple_of(g8 * (8 * SKEW_DIL), 8 * SKEW_DIL) + u * SKEW_DIL
            dst = pl.multiple_of(g8 * (8 * SKEW_PITCH), 8) + u * SKEW_PITCH
            q17[pl.ds(dst, SKEW_DIL), :] = q32[pl.ds(src, SKEW_DIL), :]
            k17[pl.ds(SKEW_PAD + dst, SKEW_DIL), :] = k32[pl.ds(DIL_PAD + src, SKEW_DIL), :]
            v17[pl.ds(SKEW_PAD + dst, SKEW_DIL), :] = v32[pl.ds(DIL_PAD + src, SKEW_DIL), :]
        return 0

    lax.fori_loop(0, S // (8 * SKEW_DIL), skew_rows, 0)

    lane = lax.broadcasted_iota(jnp.int32, (BLOCK, LANES), 1)
    is_a = lane < HEAD_DIM
    ones = jnp.ones((2 * BLOCK, LANES), MXU_DTYPE)

    def rows(ref, start, count, stride):
        if stride == 1:
            return ref[pl.ds(start, count), :]
        return ref[pl.ds(start, count, stride=stride), :]

    def unit(dil, rho, n):
        if dil == SKEW_DIL:
            (qs, ks, vs), stride, pad = (q17, k17, v17), SKEW_PITCH, SKEW_PAD
        else:
            (qs, ks, vs), stride, pad = (q32, k32, v32), dil, DIL_PAD
        q0 = rho + stride * BLOCK * n
        qb = rows(qs, q0, BLOCK, stride).astype(MXU_DTYPE)
        kw = rows(ks, q0 + pad - stride * BLOCK, 2 * BLOCK, stride).astype(MXU_DTYPE)
        vw = rows(vs, q0 + pad - stride * BLOCK, 2 * BLOCK, stride).astype(MXU_DTYPE)
        bias = bias_ref[jnp.minimum(n, 1)]
        s = _dot_nt(_pair_rows(qb, is_a), kw)
        s = jnp.concatenate([s[:BLOCK] + bias, s[BLOCK:] + bias], axis=0)
        m = jnp.max(s, axis=1, keepdims=True)
        p = jnp.exp2(s - m).astype(MXU_DTYPE)
        o = _dot(p, jnp.concatenate([vw, ones], axis=1))
        acc = jnp.where(is_a, o[:BLOCK, :LANES], o[BLOCK:, :LANES])
        l = jnp.where(is_a, o[:BLOCK, LANES:], o[BLOCK:, LANES:])
        mu = jnp.where(is_a, m[:BLOCK], m[BLOCK:]) + jnp.log2(l)
        return acc / l, mu

    n_units = S // BLOCK

    for slab, dil in enumerate(d for d in DILATIONS if d != 1):
        blk_bits = (n_units // dil).bit_length() - 1

        def several_units(g, _, slab=slab, dil=dil, blk_bits=blk_bits):
            for u in range(DIL_UNITS_PER_BODY):
                t = g * DIL_UNITS_PER_BODY + u
                rho = lax.shift_right_logical(t, blk_bits)
                n = t & ((1 << blk_bits) - 1)
                o_n, mu = unit(dil, rho, n)
                q0 = rho + dil * BLOCK * n
                onorm_s[slab, pl.ds(q0, BLOCK, stride=dil), :] = o_n
                mu_s[slab, pl.ds(q0, BLOCK, stride=dil), :] = mu
            return 0

        lax.fori_loop(0, n_units // DIL_UNITS_PER_BODY, several_units, 0)

    def merge_blocks(g, _):
        for u in range(DIL_UNITS_PER_BODY):
            n = g * DIL_UNITS_PER_BODY + u
            o_1, mu_1 = unit(1, 0, n)
            r0 = pl.multiple_of(n * BLOCK, BLOCK)
            others = [(onorm_s[sl, pl.ds(r0, BLOCK), :], mu_s[sl, pl.ds(r0, BLOCK), :])
                      for sl in range(len(DILATIONS) - 1)]
            top = functools.reduce(jnp.maximum, [mu_1] + [mu for _, mu in others])
            w = jnp.exp2(mu_1 - top)
            num, den = w * o_1, w
            for o_p, mu_p in others:
                w = jnp.exp2(mu_p - top)
                num = num + w * o_p
                den = den + w
            o_ref[0, pl.ds(r0, BLOCK), :] = (num / den).astype(o_ref.dtype)
        return 0

    lax.fori_loop(0, n_units // DIL_UNITS_PER_BODY, merge_blocks, 0)


def _dilated_attention(qd, kd, vd):
    B, S, _ = qd.shape
    blk = pl.BlockSpec((1, S, LANES), lambda b, p: (b, 0, p))
    n_slabs = len(DILATIONS) - 1
    skewed = S // SKEW_DIL * SKEW_PITCH
    return pl.pallas_call(
        _dil_kernel,
        grid=(B, N_PAIRS_DIL),
        in_specs=[blk, blk, blk, pl.BlockSpec((2, BLOCK, 2 * BLOCK), lambda b, p: (0, 0, 0))],
        out_specs=blk,
        out_shape=jax.ShapeDtypeStruct((B, S, D_DIL), MXU_DTYPE),
        scratch_shapes=[
            pltpu.VMEM((S, LANES), F32),
            pltpu.VMEM((DIL_PAD + S, LANES), F32),
            pltpu.VMEM((DIL_PAD + S, LANES), F32),
            pltpu.VMEM((skewed, LANES), F32),
            pltpu.VMEM((SKEW_PAD + skewed, LANES), F32),
            pltpu.VMEM((SKEW_PAD + skewed, LANES), F32),
            pltpu.VMEM((n_slabs, S, LANES), F32),
            pltpu.VMEM((n_slabs, S, LANES), F32),
        ],
        compiler_params=_cparams(2),
        name="dilated_attention",
    )(qd, kd, vd, jnp.asarray(_band_bias()))


def _layer_norm(u, g, b):
    mu = jnp.mean(u, axis=1, keepdims=True)
    d = u - mu
    var = jnp.mean(d * d, axis=1, keepdims=True)
    return d * lax.rsqrt(var + LN_EPS) * g + b


def _rms_norm(x, g):
    ms = jnp.mean(x * x, axis=1, keepdims=True)
    return x * lax.rsqrt(ms + RMS_EPS) * g


def _top2_of4(vals):
    v1, i1 = vals[0], jnp.zeros(vals[0].shape, jnp.int32)
    for i in range(1, 4):
        better = vals[i] > v1
        v1 = jnp.where(better, vals[i], v1)
        i1 = jnp.where(better, i, i1)
    v2 = jnp.full(vals[0].shape, -1.0, F32)
    i2 = jnp.zeros(vals[0].shape, jnp.int32)
    for i in range(4):
        better = (vals[i] > v2) & (i1 != i)
        v2 = jnp.where(better, vals[i], v2)
        i2 = jnp.where(better, i, i2)
    return v1, i1, v2, i2


def _outproj_kernel(od_ref, of_ref, h_ref, wo_ref, gd_ref, gf_ref, lng_ref, lnb_ref,
                    wr_ref, br_ref, tri_ref,
                    h1_ref, h1t_ref, e_ref, gate_ref, rank_ref, cnt_ref, base_ref):
    step = pl.program_id(0)

    @pl.when(step == 0)
    def _():
        base_ref[...] = jnp.zeros_like(base_ref)

    xd = _rms_norm(od_ref[...].astype(F32), gd_ref[...])
    xf = _rms_norm(of_ref[...].astype(F32), gf_ref[...])
    y = _dot(xd.astype(MXU_DTYPE), wo_ref[0:D_DIL, :]) + _dot(xf.astype(MXU_DTYPE), wo_ref[D_DIL:, :])
    h1 = _layer_norm(DEEPNORM_ALPHA * h_ref[...] + y, lng_ref[...], lnb_ref[...])
    h1_ref[...] = h1
    _store_token_tiles(h1t_ref, h1)
    tm = h1.shape[0]

    h_hi, h_mid, _ = _split3(h1)
    two = _dot_nt(wr_ref[...], h_hi)
    logits = (two[:N_EXPERTS] + two[N_EXPERTS:]) + _dot_nt(wr_ref[0:N_EXPERTS, :], h_mid) + br_ref[...]
    logits = logits - jnp.max(logits, axis=0, keepdims=True)
    ex = jnp.exp(logits)
    probs = ex / jnp.sum(ex, axis=0, keepdims=True)
    pr = [probs[j:j + 1, :] for j in range(N_EXPERTS)]

    def group_score(g):
        v = pr[4 * g:4 * g + 4]
        pairs = [v[a] + v[b] for a in range(4) for b in range(a + 1, 4)]
        return functools.reduce(jnp.maximum, pairs)

    best = group_score(0)
    gsel = jnp.zeros((1, tm), jnp.int32)
    for g in range(1, N_GROUPS):
        sc = group_score(g)
        better = sc > best
        best = jnp.where(better, sc, best)
        gsel = jnp.where(better, g, gsel)
    in_grp = []
    for i in range(EXPERTS_PER_GROUP):
        v = pr[i]
        for g in range(1, N_GROUPS):
            v = jnp.where(gsel == g, pr[4 * g + i], v)
        in_grp.append(v)
    v1, i1, v2, i2 = _top2_of4(in_grp)
    e1 = gsel * EXPERTS_PER_GROUP + i1
    e2 = gsel * EXPERTS_PER_GROUP + i2
    den = v1 + v2
    e_ref[...] = jnp.concatenate([e1, e2], axis=0)
    gate_ref[...] = jnp.concatenate([v1 / den, v2 / den], axis=0)

    eidx = lax.broadcasted_iota(jnp.int32, (N_EXPERTS, tm), 0)
    oh1 = (eidx == e1).astype(F32)
    oh2 = (eidx == e2).astype(F32)
    tot = oh1 + oh2
    before = base_ref[...] + _dot(tot.astype(jnp.bfloat16), tri_ref[...])
    r1 = jnp.sum(oh1 * before, axis=0, keepdims=True)
    r2 = jnp.sum(oh2 * before, axis=0, keepdims=True)
    rank_ref[...] = jnp.concatenate([r1, r2], axis=0).astype(jnp.int32)
    base_ref[...] = base_ref[...] + jnp.sum(tot, axis=1, keepdims=True)
    cnt_ref[...] = jnp.broadcast_to(base_ref[...], cnt_ref.shape)


def _out_projection(od, of, h, w_out, g_dil, g_fox, ln_g, ln_b, wr3, b_router, tm=1024):
    T = h.shape[0]
    tri = (np.arange(tm)[:, None] < np.arange(tm)[None, :]).astype(np.float32)
    rows = lambda width: pl.BlockSpec((tm, width), lambda i: (i, 0))
    full = lambda a: pl.BlockSpec(a.shape, lambda i: (0,) * a.ndim)
    tok = pl.BlockSpec((TOP_K, tm), lambda i: (0, i))
    consts = [w_out, g_dil.reshape(1, -1), g_fox.reshape(1, -1), ln_g.reshape(1, -1), ln_b.reshape(1, -1),
              wr3, b_router.reshape(-1, 1).astype(F32), jnp.asarray(tri, jnp.bfloat16)]
    return pl.pallas_call(
        _outproj_kernel,
        grid=(T // tm,),
        in_specs=[rows(D_DIL), rows(D_FOX), rows(D_MODEL)] + [full(a) for a in consts],
        out_specs=(rows(D_MODEL), pl.BlockSpec((tm * TOKEN_TILE_ROWS, LANES), lambda i: (i, 0)),
                   tok, tok, tok, pl.BlockSpec((N_EXPERTS, LANES), lambda i: (0, 0))),
        out_shape=(
            jax.ShapeDtypeStruct((T, D_MODEL), F32),
            jax.ShapeDtypeStruct((T * TOKEN_TILE_ROWS, LANES), F32),
            jax.ShapeDtypeStruct((TOP_K, T), jnp.int32),
            jax.ShapeDtypeStruct((TOP_K, T), F32),
            jax.ShapeDtypeStruct((TOP_K, T), jnp.int32),
            jax.ShapeDtypeStruct((N_EXPERTS, LANES), F32),
        ),
        scratch_shapes=[pltpu.VMEM((N_EXPERTS, 1), F32)],
        compiler_params=_cparams(1),
        name="out_projection_router",
    )(od, of, h, *consts)


def _store_token_tiles(ref, x):
    n = x.shape[0]
    for c in range(TOKEN_TILE_ROWS):
        ref[pl.ds(c, n, stride=TOKEN_TILE_ROWS), :] = x[:, c * LANES:(c + 1) * LANES]


def _load_token_tiles(ref, n, first=0):
    return jnp.concatenate([ref[pl.ds(first * TOKEN_TILE_ROWS + c, n, stride=TOKEN_TILE_ROWS), :]
                            for c in range(TOKEN_TILE_ROWS)], axis=1)


def _tile_copy(src, src_row, dst, dst_row, sem):
    return pltpu.make_async_copy(src.at[pl.ds(src_row, TOKEN_TILE_ROWS)],
                                 dst.at[pl.ds(dst_row, TOKEN_TILE_ROWS)], sem)


def _dispatch_kernel(pc0_ref, pc1_ref, pp0_ref, pp1_ref, pad0_ref, padn_ref, h_hbm, xg_out,
                     hbuf, sem, sem_load, ztile, *, td):
    j = pl.program_id(0)
    n_steps = pl.num_programs(0)
    tile_rows = td * TOKEN_TILE_ROWS

    def load(tile, slot):
        src = pl.multiple_of(tile * tile_rows, tile_rows)
        return pltpu.make_async_copy(h_hbm.at[pl.ds(src, tile_rows)], hbuf.at[slot], sem_load.at[slot])

    def scatter(p0_ref, p1_ref, slot, which):
        def copies(t):
            src = pl.multiple_of(t * TOKEN_TILE_ROWS, TOKEN_TILE_ROWS)
            return (_tile_copy(hbuf.at[slot], src, xg_out, pl.multiple_of(p0_ref[t], TOKEN_TILE_ROWS), sem.at[which]),
                    _tile_copy(hbuf.at[slot], src, xg_out, pl.multiple_of(p1_ref[t], TOKEN_TILE_ROWS), sem.at[which]))

        def issue(t, _):
            for thread, cp in enumerate(copies(t)):
                cp.start(priority=thread)
            return 0

        def drain(t, _):
            for cp in copies(t):
                cp.wait()
            return 0

        def issue_all():
            lax.fori_loop(0, td, issue, 0, unroll=DMA_ISSUE_UNROLL)

        def wait_all():
            lax.fori_loop(0, td, drain, 0, unroll=DMA_ISSUE_UNROLL)

        return issue_all, wait_all

    @pl.when(j == 0)
    def _():
        load(0, 0).start()
        load(1, 1).start()

    load(j, j % 3).wait()
    issue_cur, wait_cur = scatter(pc0_ref, pc1_ref, j % 3, j % 2)
    issue_cur()

    @pl.when(j >= 1)
    def _():
        scatter(pp0_ref, pp1_ref, (j - 1) % 3, (j - 1) % 2)[1]()

    @pl.when(j + 2 < n_steps)
    def _():
        load(j + 2, (j + 2) % 3).start()

    pl.when(j == n_steps - 1)(wait_cur)

    @pl.when(pl.program_id(0) == pl.num_programs(0) - 1)
    def _():
        ztile[...] = jnp.zeros(ztile.shape, ztile.dtype)

        chunk_rows = ztile.shape[0]
        chunk_tokens = chunk_rows // TOKEN_TILE_ROWS

        def for_each_zero_copy(action):
            def hole(e, _):
                n_chunks = padn_ref[e] // chunk_tokens

                def chunk(c, _):
                    dst = pl.multiple_of(pad0_ref[e] + c * chunk_rows, TOKEN_TILE_ROWS)
                    action(pltpu.make_async_copy(ztile, xg_out.at[pl.ds(dst, chunk_rows)], sem.at[0]))
                    return 0

                def single(r, _):
                    dst = pl.multiple_of(pad0_ref[e] + r * TOKEN_TILE_ROWS, TOKEN_TILE_ROWS)
                    action(_tile_copy(ztile, 0, xg_out, dst, sem.at[0]))
                    return 0

                lax.fori_loop(0, n_chunks, chunk, 0)
                lax.fori_loop(n_chunks * chunk_tokens, padn_ref[e], single, 0)
                return 0
            lax.fori_loop(0, pad0_ref.shape[0], hole, 0)

        for_each_zero_copy(lambda cp: cp.start())
        for_each_zero_copy(lambda cp: cp.wait())


def _dispatch(h1t, pos_rows, pad_rows, pad_counts, n_rows, td=256):
    T = h1t.shape[0] // TOKEN_TILE_ROWS
    assert T // td >= 2
    cur = pl.BlockSpec((td,), lambda i: (i,), memory_space=pltpu.SMEM)
    prev = pl.BlockSpec((td,), lambda i: (jnp.maximum(i - 1, 0),), memory_space=pltpu.SMEM)
    smem_all = pl.BlockSpec(memory_space=pltpu.SMEM)
    anyspec = pl.BlockSpec(memory_space=pl.ANY)
    return pl.pallas_call(
        functools.partial(_dispatch_kernel, td=td),
        grid=(T // td,),
        in_specs=[cur, cur, prev, prev, smem_all, smem_all, anyspec],
        out_specs=anyspec,
        out_shape=jax.ShapeDtypeStruct((n_rows * TOKEN_TILE_ROWS, LANES), F32),
        scratch_shapes=[pltpu.VMEM((3, td * TOKEN_TILE_ROWS, LANES), F32),
                        pltpu.SemaphoreType.DMA((2,)), pltpu.SemaphoreType.DMA((3,)),
                        pltpu.VMEM((ZERO_CHUNK_TOKENS * TOKEN_TILE_ROWS, LANES), F32)],
        compiler_params=_cparams(1),
        name="moe_dispatch",
    )(pos_rows[0], pos_rows[1], pos_rows[0], pos_rows[1], pad_rows, pad_counts, h1t)


def _ffn_kernel(be_ref, nv_ref, slot_ref, nxt_ref, x_ref, wg_hbm, wu_hbm, wd_hbm, y_ref, wstage, wb, sem,
                *, tmf, layer):
    j = pl.program_id(0)
    valid = j < nv_ref[0]

    def fetch(expert, slot):
        return [pltpu.make_async_copy(w.at[layer, expert], wstage.at[slot, k], sem.at[slot])
                for k, w in enumerate((wg_hbm, wu_hbm, wd_hbm))]

    @pl.when(j == 0)
    def _():
        for cp in fetch(be_ref[0], 0):
            cp.start()

    @pl.when(valid & ((j == 0) | (be_ref[j] != be_ref[jnp.maximum(j - 1, 0)])))
    def _():
        slot = slot_ref[j]
        for cp in fetch(be_ref[j], slot):
            cp.wait()
        for k in range(3):
            wb[k] = wstage[slot, k].astype(MXU_DTYPE)

        @pl.when(nxt_ref[j] >= 0)
        def _():
            for cp in fetch(nxt_ref[j], 1 - slot):
                cp.start()

    @pl.when(valid)
    def _():
        xb = _load_token_tiles(x_ref, tmf).astype(MXU_DTYPE)
        a = _dot(xb, wb[0])
        u = _dot(xb, wb[1])
        hmid = (a * jax.nn.sigmoid(a)) * u
        _store_token_tiles(y_ref, _dot(hmid.astype(MXU_DTYPE), wb[2]))

    @pl.when(j >= nv_ref[0])
    def _():
        y_ref[...] = jnp.zeros_like(y_ref)


def _expert_ffn(xg, blk_e, n_valid, present, wg, wu, wd, layer, tmf):
    n_rows = xg.shape[0] // TOKEN_TILE_ROWS
    D = D_MODEL
    nb = n_rows // tmf
    eids = jnp.arange(N_EXPERTS, dtype=jnp.int32)
    order = jnp.cumsum(present.astype(jnp.int32)) - 1
    later = (eids[None, :] > eids[:, None]) & present[None, :]
    nxt_e = jnp.min(jnp.where(later, eids[None, :], N_EXPERTS), axis=1)
    nxt_e = jnp.where(nxt_e == N_EXPERTS, -1, nxt_e)
    slot = (order & 1)[blk_e].astype(jnp.int32)
    nxt = nxt_e[blk_e].astype(jnp.int32)
    xmap = lambda j, be, nv, sl, nx: (jnp.minimum(j, nv[0] - 1), 0)
    anyspec = pl.BlockSpec(memory_space=pl.ANY)
    grid_spec = pltpu.PrefetchScalarGridSpec(
        num_scalar_prefetch=4,
        grid=(nb,),
        in_specs=[pl.BlockSpec((tmf * TOKEN_TILE_ROWS, LANES), xmap), anyspec, anyspec, anyspec],
        out_specs=pl.BlockSpec((tmf * TOKEN_TILE_ROWS, LANES), lambda j, be, nv, sl, nx: (j, 0)),
        scratch_shapes=[pltpu.VMEM((2, 3, D, D), F32), pltpu.VMEM((3, D, D), MXU_DTYPE),
                        pltpu.SemaphoreType.DMA((2,))],
    )
    return pl.pallas_call(
        functools.partial(_ffn_kernel, tmf=tmf, layer=layer),
        grid_spec=grid_spec,
        out_shape=jax.ShapeDtypeStruct(xg.shape, F32),
        compiler_params=_cparams(1),
        name="moe_expert_ffn",
    )(blk_e, n_valid, slot, nxt, xg, wg, wu, wd)


def _combine_kernel(pc0_ref, pc1_ref, pn0_ref, pn1_ref, gate_ref, h_ref, lng_ref, lnb_ref, y_hbm, o_ref,
                    buf_a0, buf_a1, buf_b0, buf_b1, sem, *, tc):
    j = pl.program_id(0)

    def tile(p0_ref, p1_ref, first, bufs, which):
        def copies(t):
            dst = pl.multiple_of(t * TOKEN_TILE_ROWS, TOKEN_TILE_ROWS)
            return (_tile_copy(y_hbm, pl.multiple_of(p0_ref[first + t], TOKEN_TILE_ROWS), bufs[0], dst, sem.at[which]),
                    _tile_copy(y_hbm, pl.multiple_of(p1_ref[first + t], TOKEN_TILE_ROWS), bufs[1], dst, sem.at[which]))

        def issue(t, _):
            for thread, cp in enumerate(copies(t)):
                cp.start(priority=thread)
            return 0

        def drain(t, _):
            for cp in copies(t):
                cp.wait()
            return 0

        def issue_all():
            lax.fori_loop(0, tc, issue, 0, unroll=DMA_ISSUE_UNROLL)

        def wait_all():
            lax.fori_loop(0, tc, drain, 0, unroll=DMA_ISSUE_UNROLL)

        return issue_all, wait_all

    def finish(first, bufs):
        g = gate_ref[first:first + tc, :]
        y = _load_token_tiles(bufs[0], tc) * g[:, 0:1] + _load_token_tiles(bufs[1], tc) * g[:, 1:2]
        o_ref[first:first + tc, :] = _layer_norm(DEEPNORM_ALPHA * h_ref[first:first + tc, :] + y,
                                                 lng_ref[...], lnb_ref[...])

    issue_a, wait_a = tile(pc0_ref, pc1_ref, 0, (buf_a0, buf_a1), 0)
    issue_b, wait_b = tile(pc0_ref, pc1_ref, tc, (buf_b0, buf_b1), 1)
    issue_next_a, _ = tile(pn0_ref, pn1_ref, 0, (buf_a0, buf_a1), 0)

    pl.when(j == 0)(issue_a)
    issue_b()
    wait_a()
    finish(0, (buf_a0, buf_a1))
    pl.when(j + 1 < pl.num_programs(0))(issue_next_a)
    wait_b()
    finish(tc, (buf_b0, buf_b1))


def _combine(yg, pos_rows, gates_col, h1, ln_g, ln_b, tc=256):
    T, D = h1.shape
    n_tiles = T // tc
    cur = pl.BlockSpec((2 * tc,), lambda j: (j,), memory_space=pltpu.SMEM)
    nxt = pl.BlockSpec((tc,), lambda j: (jnp.minimum(2 * j + 2, n_tiles - 1),), memory_space=pltpu.SMEM)
    rows = lambda width: pl.BlockSpec((2 * tc, width), lambda j: (j, 0))
    vec = pl.BlockSpec((1, D), lambda j: (0, 0))
    buf = pltpu.VMEM((tc * TOKEN_TILE_ROWS, LANES), F32)
    return pl.pallas_call(
        functools.partial(_combine_kernel, tc=tc),
        grid=(n_tiles // 2,),
        in_specs=[cur, cur, nxt, nxt, rows(TOP_K), rows(D), vec, vec, pl.BlockSpec(memory_space=pl.ANY)],
        out_specs=rows(D),
        out_shape=jax.ShapeDtypeStruct((T, D), F32),
        scratch_shapes=[buf, buf, buf, buf, pltpu.SemaphoreType.DMA((2,))],
        compiler_params=_cparams(1),
        name="moe_combine_ln",
    )(pos_rows[0], pos_rows[1], pos_rows[0], pos_rows[1], gates_col, h1,
      ln_g.reshape(1, -1), ln_b.reshape(1, -1), yg)


def _grouped_moe(h1, h1t, experts, gates, ranks, counts, wg, wu, wd, layer, ln_g, ln_b, tmf=512):
    T = h1.shape[0]
    nb = (T * TOP_K) // tmf + N_EXPERTS
    cnt = counts[:, 0].astype(jnp.int32)
    pcnt = (cnt + tmf - 1) // tmf * tmf
    pends = jnp.cumsum(pcnt)
    pstart = pends - pcnt
    eids = jnp.arange(N_EXPERTS, dtype=jnp.int32)
    seg = jnp.sum(jnp.where(experts[None] == eids[:, None, None], pstart[:, None, None], 0), axis=0)
    pos_rows = (seg + ranks) * TOKEN_TILE_ROWS
    blk_start = jnp.arange(nb, dtype=jnp.int32) * tmf
    blk_e = jnp.minimum(jnp.sum((pends[None, :] <= blk_start[:, None]).astype(jnp.int32), axis=1), N_EXPERTS - 1)
    n_valid = (pends[-1:] // tmf).astype(jnp.int32)
    hole_start = jnp.concatenate([pstart + cnt, pends[-1:]])
    hole_rows = jnp.concatenate([pcnt - cnt, nb * tmf - pends[-1:]])
    xg = _dispatch(h1t, pos_rows, hole_start * TOKEN_TILE_ROWS, hole_rows, nb * tmf)
    yg = _expert_ffn(xg, blk_e, n_valid, cnt > 0, wg, wu, wd, layer, tmf)
    return _combine(yg, pos_rows, gates.T, h1, ln_g, ln_b)


def _rope_tables(S):
    half = HEAD_DIM // 2
    inv = ROPE_THETA ** (-jnp.arange(half, dtype=F32) / half)
    ang = jnp.arange(S, dtype=F32)[:, None] * inv[None, :]
    cos, sin = jnp.cos(ang), jnp.sin(ang)
    reps = LANES // HEAD_DIM
    cos_tab = jnp.tile(jnp.concatenate([cos, cos], axis=1), (1, reps))
    sin_tab = jnp.tile(jnp.concatenate([-sin, sin], axis=1), (1, reps))
    return cos_tab, sin_tab


def _pad_w_in(w):
    pad = jnp.zeros((D_MODEL, D_PROJ_PAD - w.shape[1]), w.dtype)
    return jnp.concatenate([w, pad], axis=1).astype(MXU_DTYPE)


def kernel(x, w_in, b_forget, g_dil, g_fox, w_out, ln1_g, ln1_b, w_router, b_router,
           w_gate, w_up, w_down, ln2_g, ln2_b):
    B, S, D = x.shape
    T = B * S
    cos_tab, sin_tab = _rope_tables(S)
    wr_hi, wr_mid, _ = _split3(w_router.astype(F32).T)
    wr3 = jnp.concatenate([wr_hi, wr_mid], axis=0)
    h = x
    for l in range(DEPTH):
        qd, kd, vd, qf, kf, vf, zf = _in_projection(h, _pad_w_in(w_in[l]), cos_tab, sin_tab)
        c = _forget_cumsum(zf, b_forget[l])[:, :N_HEADS_FOX]
        o_fox = _fox_attention(qf, kf, vf, c)
        o_dil = _dilated_attention(qd, kd, vd)
        h1, h1t, experts, gates, ranks, counts = _out_projection(
            o_dil.reshape(T, D_DIL), o_fox.reshape(T, D_FOX), h.reshape(T, D),
            w_out[l].astype(MXU_DTYPE), g_dil[l], g_fox[l], ln1_g[l], ln1_b[l], wr3, b_router)
        h2 = _grouped_moe(h1, h1t, experts, gates, ranks, counts,
                          w_gate, w_up, w_down, l,
                          ln2_g[l], ln2_b[l])
        h = h2.reshape(B, S, D)
    return h
```

```python
import functools

import numpy as np
import jax
import jax.numpy as jnp
from jax import lax
from jax.experimental import pallas as pl
from jax.experimental.pallas import tpu as pltpu

D_MODEL = 1024
DEPTH = 2
HEAD_DIM = 64
N_HEADS_DIL = 12
N_HEADS_FOX = 4
D_DIL = N_HEADS_DIL * HEAD_DIM
D_FOX = N_HEADS_FOX * HEAD_DIM
DILATIONS = (1, 4, 16)
BLOCK = 128
ROPE_THETA = 10000.0
N_EXPERTS = 16
N_GROUPS = 4
EXPERTS_PER_GROUP = 4
TOP_K = 2
DEEPNORM_ALPHA = (2.0 * DEPTH) ** 0.25
LN_EPS = 1e-5
RMS_EPS = 1e-6

LANES = 128
N_PAIRS_DIL = D_DIL // LANES
D_PROJ_PAD = 3 * D_DIL + 3 * D_FOX + LANES
VMEM_LIMIT = 48 * 1024 * 1024
TOKEN_TILE_ROWS = D_MODEL // LANES
DMA_ISSUE_UNROLL = 8
ZERO_CHUNK_TOKENS = 32
SKEW_DIL = 16
SKEW_PITCH = SKEW_DIL + 1
DIL_PAD = max(d for d in DILATIONS if d != SKEW_DIL) * BLOCK
SKEW_PAD = SKEW_PITCH * BLOCK
MASK_BIAS = -1e30
LOG2_E = 1.4426950408889634
DIL_UNITS_PER_BODY = 32
FOX_TK = 256
FOX_BLOCKS_PER_BODY = 8
ZF_ROWS = 8

MXU_DTYPE = jnp.bfloat16
F32 = jnp.float32
NEG_INF = float("-inf")


def _cparams(n_axes):
    return pltpu.CompilerParams(dimension_semantics=("arbitrary",) * n_axes,
                                vmem_limit_bytes=VMEM_LIMIT)


def _dot(a, b):
    return jnp.dot(a, b, preferred_element_type=F32)


def _dot_nt(a, b):
    return lax.dot_general(a, b, (((1,), (1,)), ((), ())), preferred_element_type=F32)


def _inproj_kernel(x_ref, w_ref, cos_ref, sin_ref,
                   qd_ref, kd_ref, vd_ref, qf_ref, kf_ref, vf_ref, zf_ref):
    xb = x_ref[0].astype(MXU_DTYPE)
    tm = xb.shape[0]
    cos = cos_ref[...]
    sin = sin_ref[...]
    lane = lax.broadcasted_iota(jnp.int32, (tm, LANES), 1)
    first_half = (lane % HEAD_DIM) < (HEAD_DIM // 2)

    def rope(z):
        rot = jnp.where(first_half, pltpu.roll(z, LANES - 32, 1), pltpu.roll(z, 32, 1))
        return z * cos + rot * sin

    def proj(col, width):
        return _dot(xb, w_ref[:, col:col + width])

    scale = HEAD_DIM ** -0.5
    col = 0
    for out_ref, roped, mul in ((qd_ref, True, scale * LOG2_E), (kd_ref, True, 1.0)):
        for c in range(D_DIL // 256):
            z = proj(col, 256)
            for half in range(2):
                zz = rope(z[:, half * LANES:(half + 1) * LANES]) * mul
                out_ref[0, :, c * 256 + half * LANES:c * 256 + (half + 1) * LANES] = zz.astype(out_ref.dtype)
            col += 256
    for c in range(D_DIL // 256):
        vd_ref[0, :, c * 256:(c + 1) * 256] = proj(col, 256).astype(vd_ref.dtype)
        col += 256
    qf_ref[0] = (proj(col, D_FOX) * (scale * LOG2_E)).astype(qf_ref.dtype)
    col += D_FOX
    kf_ref[0] = proj(col, D_FOX).astype(kf_ref.dtype)
    col += D_FOX
    vf_ref[0] = proj(col, D_FOX).astype(vf_ref.dtype)
    col += D_FOX
    zf_ref[0] = proj(col, LANES).T[0:ZF_ROWS, :]


def _in_projection(h, w_pad, cos_tab, sin_tab, tm=1024):
    B, S, D = h.shape
    grid = (B, S // tm)
    row = lambda width: pl.BlockSpec((1, tm, width), lambda b, i: (b, i, 0))
    out_shape = (
        jax.ShapeDtypeStruct((B, S, D_DIL), MXU_DTYPE),
        jax.ShapeDtypeStruct((B, S, D_DIL), MXU_DTYPE),
        jax.ShapeDtypeStruct((B, S, D_DIL), MXU_DTYPE),
        jax.ShapeDtypeStruct((B, S, D_FOX), MXU_DTYPE),
        jax.ShapeDtypeStruct((B, S, D_FOX), MXU_DTYPE),
        jax.ShapeDtypeStruct((B, S, D_FOX), MXU_DTYPE),
        jax.ShapeDtypeStruct((B, ZF_ROWS, S), F32),
    )
    return pl.pallas_call(
        _inproj_kernel,
        grid=grid,
        in_specs=[
            row(D),
            pl.BlockSpec((D, D_PROJ_PAD), lambda b, i: (0, 0)),
            pl.BlockSpec((tm, LANES), lambda b, i: (i, 0)),
            pl.BlockSpec((tm, LANES), lambda b, i: (i, 0)),
        ],
        out_specs=(row(D_DIL), row(D_DIL), row(D_DIL), row(D_FOX), row(D_FOX), row(D_FOX),
                   pl.BlockSpec((1, ZF_ROWS, tm), lambda b, i: (b, 0, i))),
        out_shape=out_shape,
        compiler_params=_cparams(2),
        name="in_projection",
    )(h, w_pad, cos_tab, sin_tab)


def _split3(x):
    hi = x.astype(jnp.bfloat16)
    r1 = x - hi.astype(F32)
    mid = r1.astype(jnp.bfloat16)
    lo = (r1 - mid.astype(F32)).astype(jnp.bfloat16)
    return hi, mid, lo


def _forget_cumsum_kernel(z_ref, b_ref, tri_ref, c_ref):
    x = z_ref[0] + b_ref[...]
    logf = jnp.minimum(x, 0.0) - jnp.log1p(jnp.exp(-jnp.abs(x)))
    tri = tri_ref[...]
    S = x.shape[1]
    carry = jnp.zeros((x.shape[0], 1), F32)
    for blk in range(S // LANES):
        seg = logf[:, blk * LANES:(blk + 1) * LANES]
        hi, mid, lo = _split3(seg)
        cs = (_dot(hi, tri) + _dot(mid, tri)) + _dot(lo, tri) + carry
        c_ref[0, :, blk * LANES:(blk + 1) * LANES] = cs * LOG2_E
        carry = cs[:, LANES - 1:LANES]


def _forget_cumsum(zt, b_forget):
    B, Hf, S = zt.shape
    tri = (np.arange(LANES)[:, None] <= np.arange(LANES)[None, :]).astype(np.float32)
    return pl.pallas_call(
        _forget_cumsum_kernel,
        grid=(B,),
        in_specs=[
            pl.BlockSpec((1, Hf, S), lambda b: (b, 0, 0)),
            pl.BlockSpec((Hf, 1), lambda b: (0, 0)),
            pl.BlockSpec((LANES, LANES), lambda b: (0, 0)),
        ],
        out_specs=pl.BlockSpec((1, Hf, S), lambda b: (b, 0, 0)),
        out_shape=jax.ShapeDtypeStruct((B, Hf, S), F32),
        compiler_params=_cparams(1),
        name="forget_cumsum",
    )(zt, jnp.pad(b_forget.astype(F32), (0, Hf - b_forget.shape[0])).reshape(Hf, 1), jnp.asarray(tri, jnp.bfloat16))


def _pair_rows(xp, is_a):
    zero = jnp.zeros_like(xp)
    return jnp.concatenate([jnp.where(is_a, xp, zero), jnp.where(is_a, zero, xp)], axis=0)


def _causal_bias():
    qi = np.arange(BLOCK)[:, None]
    ki = np.arange(FOX_TK)[None, :]
    vis = np.stack([ki <= qi, ki <= qi + BLOCK])
    return np.where(vis, 0.0, MASK_BIAS).astype(np.float32)


def _fox_kernel(q_ref, k_ref, v_ref, crow_ref, cbias_ref, o_ref):
    S = q_ref.shape[1]
    tk = FOX_TK
    lane = lax.broadcasted_iota(jnp.int32, (BLOCK, LANES), 1)
    is_a = lane < HEAD_DIM
    ones = jnp.ones((tk, LANES), MXU_DTYPE)

    def unit(q2, kv, bias):
        kb, v_ones, cr = kv
        s = _dot_nt(q2, kb)
        sa = s[:BLOCK] - cr[0:1]
        sb = s[BLOCK:] - cr[1:2]
        if bias is not None:
            sa = sa + bias
            sb = sb + bias
        s = jnp.concatenate([sa, sb], axis=0)
        m = jnp.max(s, axis=1, keepdims=True)
        p = jnp.exp2(s - m).astype(MXU_DTYPE)
        o = _dot(p, v_ones)
        acc = jnp.where(is_a, o[:BLOCK, :LANES], o[BLOCK:, :LANES])
        l = jnp.where(is_a, o[:BLOCK, LANES:], o[BLOCK:, LANES:])
        return acc, l, jnp.where(is_a, m[:BLOCK], m[BLOCK:])

    def key_block(j):
        k0 = pl.multiple_of(j * tk, tk)
        return (k_ref[0, pl.ds(k0, tk), :],
                jnp.concatenate([v_ref[0, pl.ds(k0, tk), :], ones], axis=1),
                crow_ref[0, 0, :, pl.ds(k0, tk)])

    def fold(state, units):
        num, den, top_old = state
        top = functools.reduce(jnp.maximum, [top_old] + [m for _, _, m in units])
        scale = jnp.exp2(top_old - top)
        num, den = scale * num, scale * den
        for acc, l, m in units:
            w = jnp.exp2(m - top)
            num = num + w * acc
            den = den + w * l
        return num, den, top

    def q_block(i, _):
        r0 = pl.multiple_of(i * tk, tk)
        q2 = [_pair_rows(q_ref[0, pl.ds(r0 + h * BLOCK, BLOCK), :], is_a) for h in range(2)]
        zero = jnp.zeros((BLOCK, LANES), F32)
        empty = (zero, zero, jnp.full((BLOCK, LANES), MASK_BIAS, F32))

        def fold_blocks(states, blocks):
            units = [[unit(q2[h], kv, None if bias is None else bias[h]) for h in range(2)]
                     for kv, bias in ((key_block(j), bias) for j, bias in blocks)]
            return tuple(fold(states[h], [u[h] for u in units]) for h in range(2))

        def full_blocks(g, states):
            return fold_blocks(states, [(FOX_BLOCKS_PER_BODY * g + u, None) for u in range(FOX_BLOCKS_PER_BODY)])

        states = lax.fori_loop(0, i // FOX_BLOCKS_PER_BODY, full_blocks, (empty, empty))
        diag = (i, (cbias_ref[0], cbias_ref[1]))

        def tail(n_left):
            return lambda st: fold_blocks(st, [(i - n_left + u, None) for u in range(n_left)] + [diag])

        states = lax.switch(i % FOX_BLOCKS_PER_BODY, [tail(r) for r in range(FOX_BLOCKS_PER_BODY)], states)
        for h in range(2):
            num, den, _ = states[h]
            o_ref[0, pl.ds(r0 + h * BLOCK, BLOCK), :] = (num / den).astype(o_ref.dtype)
        return 0

    lax.fori_loop(0, S // tk, q_block, 0)


def _fox_attention(qf, kf, vf, c2):
    B, S, _ = qf.shape
    n_pairs = D_FOX // LANES
    c4 = c2.reshape(B, n_pairs, 2, S)
    blk = pl.BlockSpec((1, S, LANES), lambda b, p: (b, 0, p))
    return pl.pallas_call(
        _fox_kernel,
        grid=(B, n_pairs),
        in_specs=[blk, blk, blk,
                  pl.BlockSpec((1, 1, 2, S), lambda b, p: (b, p, 0, 0)),
                  pl.BlockSpec((2, BLOCK, FOX_TK), lambda b, p: (0, 0, 0))],
        out_specs=blk,
        out_shape=jax.ShapeDtypeStruct((B, S, D_FOX), MXU_DTYPE),
        compiler_params=_cparams(2),
        name="fox_attention",
    )(qf, kf, vf, c4, jnp.asarray(_causal_bias()))


def _band_bias():
    qi = np.arange(BLOCK)[:, None]
    ki = np.arange(2 * BLOCK)[None, :]
    delta = BLOCK + qi - ki
    band = (delta >= 0) & (delta <= BLOCK)
    first = band & (ki >= BLOCK)
    return np.where(np.stack([first, band]), 0.0, MASK_BIAS).astype(np.float32)


def _dil_kernel(q_ref, k_ref, v_ref, bias_ref, o_ref, q32, k32, v32, q17, k17, v17, onorm_s, mu_s):
    S = q_ref.shape[1]
    q32[...] = q_ref[0].astype(F32)
    zeros = jnp.zeros((SKEW_PAD, LANES), F32)
    for plain, skewed, src in ((k32, k17, k_ref), (v32, v17, v_ref)):
        plain[0:DIL_PAD, :] = zeros[0:DIL_PAD]
        skewed[0:SKEW_PAD, :] = zeros
        plain[DIL_PAD:, :] = src[0].astype(F32)

    def skew_rows(g8, _):
        for u in range(8):
            src = pl.multiple_of(g8 * (8 * SKEW_DIL), 8 * SKEW_DIL) + u * SKEW_DIL
            dst = pl.multiple_of(g8 * (8 * SKEW_PITCH), 8) + u * SKEW_PITCH
            q17[pl.ds(dst, SKEW_DIL), :] = q32[pl.ds(src, SKEW_DIL), :]
            k17[pl.ds(SKEW_PAD + dst, SKEW_DIL), :] = k32[pl.ds(DIL_PAD + src, SKEW_DIL), :]
            v17[pl.ds(SKEW_PAD + dst, SKEW_DIL), :] = v32[pl.ds(DIL_PAD + src, SKEW_DIL), :]
        return 0

    lax.fori_loop(0, S // (8 * SKEW_DIL), skew_rows, 0)

    lane = lax.broadcasted_iota(jnp.int32, (BLOCK, LANES), 1)
    is_a = lane < HEAD_DIM
    ones = jnp.ones((2 * BLOCK, LANES), MXU_DTYPE)

    def rows(ref, start, count, stride):
        if stride == 1:
            return ref[pl.ds(start, count), :]
        return ref[pl.ds(start, count, stride=stride), :]

    def unit(dil, rho, n):
        if dil == SKEW_DIL:
            (qs, ks, vs), stride, pad = (q17, k17, v17), SKEW_PITCH, SKEW_PAD
        else:
            (qs, ks, vs), stride, pad = (q32, k32, v32), dil, DIL_PAD
        q0 = rho + stride * BLOCK * n
        qb = rows(qs, q0, BLOCK, stride).astype(MXU_DTYPE)
        kw = rows(ks, q0 + pad - stride * BLOCK, 2 * BLOCK, stride).astype(MXU_DTYPE)
        vw = rows(vs, q0 + pad - stride * BLOCK, 2 * BLOCK, stride).astype(MXU_DTYPE)
        bias = bias_ref[jnp.minimum(n, 1)]
        s = _dot_nt(_pair_rows(qb, is_a), kw)
        s = jnp.concatenate([s[:BLOCK] + bias, s[BLOCK:] + bias], axis=0)
        m = jnp.max(s, axis=1, keepdims=True)
        p = jnp.exp2(s - m).astype(MXU_DTYPE)
        o = _dot(p, jnp.concatenate([vw, ones], axis=1))
        acc = jnp.where(is_a, o[:BLOCK, :LANES], o[BLOCK:, :LANES])
        l = jnp.where(is_a, o[:BLOCK, LANES:], o[BLOCK:, LANES:])
        mu = jnp.where(is_a, m[:BLOCK], m[BLOCK:]) + jnp.log2(l)
        return acc / l, mu

    n_units = S // BLOCK

    for slab, dil in enumerate(d for d in DILATIONS if d != 1):
        blk_bits = (n_units // dil).bit_length() - 1

        def several_units(g, _, slab=slab, dil=dil, blk_bits=blk_bits):
            for u in range(DIL_UNITS_PER_BODY):
                t = g * DIL_UNITS_PER_BODY + u
                rho = lax.shift_right_logical(t, blk_bits)
                n = t & ((1 << blk_bits) - 1)
                o_n, mu = unit(dil, rho, n)
                q0 = rho + dil * BLOCK * n
                onorm_s[slab, pl.ds(q0, BLOCK, stride=dil), :] = o_n
                mu_s[slab, pl.ds(q0, BLOCK, stride=dil), :] = mu
            return 0

        lax.fori_loop(0, n_units // DIL_UNITS_PER_BODY, several_units, 0)

    def merge_blocks(g, _):
        for u in range(DIL_UNITS_PER_BODY):
            n = g * DIL_UNITS_PER_BODY + u
            o_1, mu_1 = unit(1, 0, n)
            r0 = pl.multiple_of(n * BLOCK, BLOCK)
            others = [(onorm_s[sl, pl.ds(r0, BLOCK), :], mu_s[sl, pl.ds(r0, BLOCK), :])
                      for sl in range(len(DILATIONS) - 1)]
            top = functools.reduce(jnp.maximum, [mu_1] + [mu for _, mu in others])
            w = jnp.exp2(mu_1 - top)
            num, den = w * o_1, w
            for o_p, mu_p in others:
                w = jnp.exp2(mu_p - top)
                num = num + w * o_p
                den = den + w
            o_ref[0, pl.ds(r0, BLOCK), :] = (num / den).astype(o_ref.dtype)
        return 0

    lax.fori_loop(0, n_units // DIL_UNITS_PER_BODY, merge_blocks, 0)


def _dilated_attention(qd, kd, vd):
    B, S, _ = qd.shape
    blk = pl.BlockSpec((1, S, LANES), lambda b, p: (b, 0, p))
    n_slabs = len(DILATIONS) - 1
    skewed = S // SKEW_DIL * SKEW_PITCH
    return pl.pallas_call(
        _dil_kernel,
        grid=(B, N_PAIRS_DIL),
        in_specs=[blk, blk, blk, pl.BlockSpec((2, BLOCK, 2 * BLOCK), lambda b, p: (0, 0, 0))],
        out_specs=blk,
        out_shape=jax.ShapeDtypeStruct((B, S, D_DIL), MXU_DTYPE),
        scratch_shapes=[
            pltpu.VMEM((S, LANES), F32),
            pltpu.VMEM((DIL_PAD + S, LANES), F32),
            pltpu.VMEM((DIL_PAD + S, LANES), F32),
            pltpu.VMEM((skewed, LANES), F32),
            pltpu.VMEM((SKEW_PAD + skewed, LANES), F32),
            pltpu.VMEM((SKEW_PAD + skewed, LANES), F32),
            pltpu.VMEM((n_slabs, S, LANES), F32),
            pltpu.VMEM((n_slabs, S, LANES), F32),
        ],
        compiler_params=_cparams(2),
        name="dilated_attention",
    )(qd, kd, vd, jnp.asarray(_band_bias()))


def _layer_norm(u, g, b):
    mu = jnp.mean(u, axis=1, keepdims=True)
    d = u - mu
    var = jnp.mean(d * d, axis=1, keepdims=True)
    return d * lax.rsqrt(var + LN_EPS) * g + b


def _rms_norm(x, g):
    ms = jnp.mean(x * x, axis=1, keepdims=True)
    return x * lax.rsqrt(ms + RMS_EPS) * g


def _top2_of4(vals):
    v1, i1 = vals[0], jnp.zeros(vals[0].shape, jnp.int32)
    for i in range(1, 4):
        better = vals[i] > v1
        v1 = jnp.where(better, vals[i], v1)
        i1 = jnp.where(better, i, i1)
    v2 = jnp.full(vals[0].shape, -1.0, F32)
    i2 = jnp.zeros(vals[0].shape, jnp.int32)
    for i in range(4):
        better = (vals[i] > v2) & (i1 != i)
        v2 = jnp.where(better, vals[i], v2)
        i2 = jnp.where(better, i, i2)
    return v1, i1, v2, i2


def _outproj_kernel(od_ref, of_ref, h_ref, wo_ref, gd_ref, gf_ref, lng_ref, lnb_ref,
                    wr_ref, br_ref, tri_ref,
                    h1_ref, h1t_ref, e_ref, gate_ref, rank_ref, cnt_ref, base_ref):
    step = pl.program_id(0)

    @pl.when(step == 0)
    def _():
        base_ref[...] = jnp.zeros_like(base_ref)

    xd = _rms_norm(od_ref[...].astype(F32), gd_ref[...])
    xf = _rms_norm(of_ref[...].astype(F32), gf_ref[...])
    y = _dot(xd.astype(MXU_DTYPE), wo_ref[0:D_DIL, :]) + _dot(xf.astype(MXU_DTYPE), wo_ref[D_DIL:, :])
    h1 = _layer_norm(DEEPNORM_ALPHA * h_ref[...] + y, lng_ref[...], lnb_ref[...])
    h1_ref[...] = h1
    _store_token_tiles(h1t_ref, h1)
    tm = h1.shape[0]

    h_hi, h_mid, _ = _split3(h1)
    two = _dot_nt(wr_ref[...], h_hi)
    logits = (two[:N_EXPERTS] + two[N_EXPERTS:]) + _dot_nt(wr_ref[0:N_EXPERTS, :], h_mid) + br_ref[...]
    logits = logits - jnp.max(logits, axis=0, keepdims=True)
    ex = jnp.exp(logits)
    probs = ex / jnp.sum(ex, axis=0, keepdims=True)
    pr = [probs[j:j + 1, :] for j in range(N_EXPERTS)]

    def group_score(g):
        v = pr[4 * g:4 * g + 4]
        pairs = [v[a] + v[b] for a in range(4) for b in range(a + 1, 4)]
        return functools.reduce(jnp.maximum, pairs)

    best = group_score(0)
    gsel = jnp.zeros((1, tm), jnp.int32)
    for g in range(1, N_GROUPS):
        sc = group_score(g)
        better = sc > best
        best = jnp.where(better, sc, best)
        gsel = jnp.where(better, g, gsel)
    in_grp = []
    for i in range(EXPERTS_PER_GROUP):
        v = pr[i]
        for g in range(1, N_GROUPS):
            v = jnp.where(gsel == g, pr[4 * g + i], v)
        in_grp.append(v)
    v1, i1, v2, i2 = _top2_of4(in_grp)
    e1 = gsel * EXPERTS_PER_GROUP + i1
    e2 = gsel * EXPERTS_PER_GROUP + i2
    den = v1 + v2
    e_ref[...] = jnp.concatenate([e1, e2], axis=0)
    gate_ref[...] = jnp.concatenate([v1 / den, v2 / den], axis=0)

    eidx = lax.broadcasted_iota(jnp.int32, (N_EXPERTS, tm), 0)
    oh1 = (eidx == e1).astype(F32)
    oh2 = (eidx == e2).astype(F32)
    tot = oh1 + oh2
    before = base_ref[...] + _dot(tot.astype(jnp.bfloat16), tri_ref[...])
    r1 = jnp.sum(oh1 * before, axis=0, keepdims=True)
    r2 = jnp.sum(oh2 * before, axis=0, keepdims=True)
    rank_ref[...] = jnp.concatenate([r1, r2], axis=0).astype(jnp.int32)
    base_ref[...] = base_ref[...] + jnp.sum(tot, axis=1, keepdims=True)
    cnt_ref[...] = jnp.broadcast_to(base_ref[...], cnt_ref.shape)


def _out_projection(od, of, h, w_out, g_dil, g_fox, ln_g, ln_b, wr3, b_router, tm=1024):
    T = h.shape[0]
    tri = (np.arange(tm)[:, None] < np.arange(tm)[None, :]).astype(np.float32)
    rows = lambda width: pl.BlockSpec((tm, width), lambda i: (i, 0))
    full = lambda a: pl.BlockSpec(a.shape, lambda i: (0,) * a.ndim)
    tok = pl.BlockSpec((TOP_K, tm), lambda i: (0, i))
    consts = [w_out, g_dil.reshape(1, -1), g_fox.reshape(1, -1), ln_g.reshape(1, -1), ln_b.reshape(1, -1),
              wr3, b_router.reshape(-1, 1).astype(F32), jnp.asarray(tri, jnp.bfloat16)]
    return pl.pallas_call(
        _outproj_kernel,
        grid=(T // tm,),
        in_specs=[rows(D_DIL), rows(D_FOX), rows(D_MODEL)] + [full(a) for a in consts],
        out_specs=(rows(D_MODEL), pl.BlockSpec((tm * TOKEN_TILE_ROWS, LANES), lambda i: (i, 0)),
                   tok, tok, tok, pl.BlockSpec((N_EXPERTS, LANES), lambda i: (0, 0))),
        out_shape=(
            jax.ShapeDtypeStruct((T, D_MODEL), F32),
            jax.ShapeDtypeStruct((T * TOKEN_TILE_ROWS, LANES), F32),
            jax.ShapeDtypeStruct((TOP_K, T), jnp.int32),
            jax.ShapeDtypeStruct((TOP_K, T), F32),
            jax.ShapeDtypeStruct((TOP_K, T), jnp.int32),
            jax.ShapeDtypeStruct((N_EXPERTS, LANES), F32),
        ),
        scratch_shapes=[pltpu.VMEM((N_EXPERTS, 1), F32)],
        compiler_params=_cparams(1),
        name="out_projection_router",
    )(od, of, h, *consts)


def _store_token_tiles(ref, x):
    n = x.shape[0]
    for c in range(TOKEN_TILE_ROWS):
        ref[pl.ds(c, n, stride=TOKEN_TILE_ROWS), :] = x[:, c * LANES:(c + 1) * LANES]


def _load_token_tiles(ref, n, first=0):
    return jnp.concatenate([ref[pl.ds(first * TOKEN_TILE_ROWS + c, n, stride=TOKEN_TILE_ROWS), :]
                            for c in range(TOKEN_TILE_ROWS)], axis=1)


def _tile_copy(src, src_row, dst, dst_row, sem):
    return pltpu.make_async_copy(src.at[pl.ds(src_row, TOKEN_TILE_ROWS)],
                                 dst.at[pl.ds(dst_row, TOKEN_TILE_ROWS)], sem)


def _dispatch_kernel(pc0_ref, pc1_ref, pp0_ref, pp1_ref, pad0_ref, padn_ref, h_hbm, xg_out,
                     hbuf, sem, sem_load, ztile, *, td):
    j = pl.program_id(0)
    n_steps = pl.num_programs(0)
    tile_rows = td * TOKEN_TILE_ROWS

    def load(tile, slot):
        src = pl.multiple_of(tile * tile_rows, tile_rows)
        return pltpu.make_async_copy(h_hbm.at[pl.ds(src, tile_rows)], hbuf.at[slot], sem_load.at[slot])

    def scatter(p0_ref, p1_ref, slot, which):
        def copies(t):
            src = pl.multiple_of(t * TOKEN_TILE_ROWS, TOKEN_TILE_ROWS)
            return (_tile_copy(hbuf.at[slot], src, xg_out, pl.multiple_of(p0_ref[t], TOKEN_TILE_ROWS), sem.at[which]),
                    _tile_copy(hbuf.at[slot], src, xg_out, pl.multiple_of(p1_ref[t], TOKEN_TILE_ROWS), sem.at[which]))

        def issue(t, _):
            for thread, cp in enumerate(copies(t)):
                cp.start(priority=thread)
            return 0

        def drain(t, _):
            for cp in copies(t):
                cp.wait()
            return 0

        def issue_all():
            lax.fori_loop(0, td, issue, 0, unroll=DMA_ISSUE_UNROLL)

        def wait_all():
            lax.fori_loop(0, td, drain, 0, unroll=DMA_ISSUE_UNROLL)

        return issue_all, wait_all

    @pl.when(j == 0)
    def _():
        load(0, 0).start()
        load(1, 1).start()

    load(j, j % 3).wait()
    issue_cur, wait_cur = scatter(pc0_ref, pc1_ref, j % 3, j % 2)
    issue_cur()

    @pl.when(j >= 1)
    def _():
        scatter(pp0_ref, pp1_ref, (j - 1) % 3, (j - 1) % 2)[1]()

    @pl.when(j + 2 < n_steps)
    def _():
        load(j + 2, (j + 2) % 3).start()

    pl.when(j == n_steps - 1)(wait_cur)

    @pl.when(pl.program_id(0) == pl.num_programs(0) - 1)
    def _():
        ztile[...] = jnp.zeros(ztile.shape, ztile.dtype)

        chunk_rows = ztile.shape[0]
        chunk_tokens = chunk_rows // TOKEN_TILE_ROWS

        def for_each_zero_copy(action):
            def hole(e, _):
                n_chunks = padn_ref[e] // chunk_tokens

                def chunk(c, _):
                    dst = pl.multiple_of(pad0_ref[e] + c * chunk_rows, TOKEN_TILE_ROWS)
                    action(pltpu.make_async_copy(ztile, xg_out.at[pl.ds(dst, chunk_rows)], sem.at[0]))
                    return 0

                def single(r, _):
                    dst = pl.multiple_of(pad0_ref[e] + r * TOKEN_TILE_ROWS, TOKEN_TILE_ROWS)
                    action(_tile_copy(ztile, 0, xg_out, dst, sem.at[0]))
                    return 0

                lax.fori_loop(0, n_chunks, chunk, 0)
                lax.fori_loop(n_chunks * chunk_tokens, padn_ref[e], single, 0)
                return 0
            lax.fori_loop(0, pad0_ref.shape[0], hole, 0)

        for_each_zero_copy(lambda cp: cp.start())
        for_each_zero_copy(lambda cp: cp.wait())


def _dispatch(h1t, pos_rows, pad_rows, pad_counts, n_rows, td=512):
    T = h1t.shape[0] // TOKEN_TILE_ROWS
    assert T // td >= 2
    cur = pl.BlockSpec((td,), lambda i: (i,), memory_space=pltpu.SMEM)
    prev = pl.BlockSpec((td,), lambda i: (jnp.maximum(i - 1, 0),), memory_space=pltpu.SMEM)
    smem_all = pl.BlockSpec(memory_space=pltpu.SMEM)
    anyspec = pl.BlockSpec(memory_space=pl.ANY)
    return pl.pallas_call(
        functools.partial(_dispatch_kernel, td=td),
        grid=(T // td,),
        in_specs=[cur, cur, prev, prev, smem_all, smem_all, anyspec],
        out_specs=anyspec,
        out_shape=jax.ShapeDtypeStruct((n_rows * TOKEN_TILE_ROWS, LANES), F32),
        scratch_shapes=[pltpu.VMEM((3, td * TOKEN_TILE_ROWS, LANES), F32),
                        pltpu.SemaphoreType.DMA((2,)), pltpu.SemaphoreType.DMA((3,)),
                        pltpu.VMEM((ZERO_CHUNK_TOKENS * TOKEN_TILE_ROWS, LANES), F32)],
        compiler_params=_cparams(1),
        name="moe_dispatch",
    )(pos_rows[0], pos_rows[1], pos_rows[0], pos_rows[1], pad_rows, pad_counts, h1t)


def _ffn_kernel(be_ref, nv_ref, slot_ref, nxt_ref, x_ref, wg_hbm, wu_hbm, wd_hbm, y_ref, wstage, wb, sem,
                *, tmf, layer):
    j = pl.program_id(0)
    valid = j < nv_ref[0]

    def fetch(expert, slot):
        return [pltpu.make_async_copy(w.at[layer, expert], wstage.at[slot, k], sem.at[slot])
                for k, w in enumerate((wg_hbm, wu_hbm, wd_hbm))]

    @pl.when(j == 0)
    def _():
        for cp in fetch(be_ref[0], 0):
            cp.start()

    @pl.when(valid & ((j == 0) | (be_ref[j] != be_ref[jnp.maximum(j - 1, 0)])))
    def _():
        slot = slot_ref[j]
        for cp in fetch(be_ref[j], slot):
            cp.wait()
        for k in range(3):
            wb[k] = wstage[slot, k].astype(MXU_DTYPE)

        @pl.when(nxt_ref[j] >= 0)
        def _():
            for cp in fetch(nxt_ref[j], 1 - slot):
                cp.start()

    @pl.when(valid)
    def _():
        xb = _load_token_tiles(x_ref, tmf).astype(MXU_DTYPE)
        a = _dot(xb, wb[0])
        u = _dot(xb, wb[1])
        hmid = (a * jax.nn.sigmoid(a)) * u
        _store_token_tiles(y_ref, _dot(hmid.astype(MXU_DTYPE), wb[2]))

    @pl.when(j >= nv_ref[0])
    def _():
        y_ref[...] = jnp.zeros_like(y_ref)


def _expert_ffn(xg, blk_e, n_valid, present, wg, wu, wd, layer, tmf):
    n_rows = xg.shape[0] // TOKEN_TILE_ROWS
    D = D_MODEL
    nb = n_rows // tmf
    eids = jnp.arange(N_EXPERTS, dtype=jnp.int32)
    order = jnp.cumsum(present.astype(jnp.int32)) - 1
    later = (eids[None, :] > eids[:, None]) & present[None, :]
    nxt_e = jnp.min(jnp.where(later, eids[None, :], N_EXPERTS), axis=1)
    nxt_e = jnp.where(nxt_e == N_EXPERTS, -1, nxt_e)
    slot = (order & 1)[blk_e].astype(jnp.int32)
    nxt = nxt_e[blk_e].astype(jnp.int32)
    xmap = lambda j, be, nv, sl, nx: (jnp.minimum(j, nv[0] - 1), 0)
    anyspec = pl.BlockSpec(memory_space=pl.ANY)
    grid_spec = pltpu.PrefetchScalarGridSpec(
        num_scalar_prefetch=4,
        grid=(nb,),
        in_specs=[pl.BlockSpec((tmf * TOKEN_TILE_ROWS, LANES), xmap), anyspec, anyspec, anyspec],
        out_specs=pl.BlockSpec((tmf * TOKEN_TILE_ROWS, LANES), lambda j, be, nv, sl, nx: (j, 0)),
        scratch_shapes=[pltpu.VMEM((2, 3, D, D), F32), pltpu.VMEM((3, D, D), MXU_DTYPE),
                        pltpu.SemaphoreType.DMA((2,))],
    )
    return pl.pallas_call(
        functools.partial(_ffn_kernel, tmf=tmf, layer=layer),
        grid_spec=grid_spec,
        out_shape=jax.ShapeDtypeStruct(xg.shape, F32),
        compiler_params=_cparams(1),
        name="moe_expert_ffn",
    )(blk_e, n_valid, slot, nxt, xg, wg, wu, wd)


def _combine_kernel(pc0_ref, pc1_ref, pn0_ref, pn1_ref, gate_ref, h_ref, lng_ref, lnb_ref, y_hbm, o_ref,
                    buf_a0, buf_a1, buf_b0, buf_b1, sem, *, tc):
    j = pl.program_id(0)

    def tile(p0_ref, p1_ref, first, bufs, which):
        def copies(t):
            dst = pl.multiple_of(t * TOKEN_TILE_ROWS, TOKEN_TILE_ROWS)
            return (_tile_copy(y_hbm, pl.multiple_of(p0_ref[first + t], TOKEN_TILE_ROWS), bufs[0], dst, sem.at[which]),
                    _tile_copy(y_hbm, pl.multiple_of(p1_ref[first + t], TOKEN_TILE_ROWS), bufs[1], dst, sem.at[which]))

        def issue(t, _):
            for thread, cp in enumerate(copies(t)):
                cp.start(priority=thread)
            return 0

        def drain(t, _):
            for cp in copies(t):
                cp.wait()
            return 0

        def issue_all():
            lax.fori_loop(0, tc, issue, 0, unroll=DMA_ISSUE_UNROLL)

        def wait_all():
            lax.fori_loop(0, tc, drain, 0, unroll=DMA_ISSUE_UNROLL)

        return issue_all, wait_all

    def finish(first, bufs):
        g = gate_ref[first:first + tc, :]
        y = _load_token_tiles(bufs[0], tc) * g[:, 0:1] + _load_token_tiles(bufs[1], tc) * g[:, 1:2]
        o_ref[first:first + tc, :] = _layer_norm(DEEPNORM_ALPHA * h_ref[first:first + tc, :] + y,
                                                 lng_ref[...], lnb_ref[...])

    issue_a, wait_a = tile(pc0_ref, pc1_ref, 0, (buf_a0, buf_a1), 0)
    issue_b, wait_b = tile(pc0_ref, pc1_ref, tc, (buf_b0, buf_b1), 1)
    issue_next_a, _ = tile(pn0_ref, pn1_ref, 0, (buf_a0, buf_a1), 0)

    pl.when(j == 0)(issue_a)
    issue_b()
    wait_a()
    finish(0, (buf_a0, buf_a1))
    pl.when(j + 1 < pl.num_programs(0))(issue_next_a)
    wait_b()
    finish(tc, (buf_b0, buf_b1))


def _combine(yg, pos_rows, gates_col, h1, ln_g, ln_b, tc=512):
    T, D = h1.shape
    n_tiles = T // tc
    cur = pl.BlockSpec((2 * tc,), lambda j: (j,), memory_space=pltpu.SMEM)
    nxt = pl.BlockSpec((tc,), lambda j: (jnp.minimum(2 * j + 2, n_tiles - 1),), memory_space=pltpu.SMEM)
    rows = lambda width: pl.BlockSpec((2 * tc, width), lambda j: (j, 0))
    vec = pl.BlockSpec((1, D), lambda j: (0, 0))
    buf = pltpu.VMEM((tc * TOKEN_TILE_ROWS, LANES), F32)
    return pl.pallas_call(
        functools.partial(_combine_kernel, tc=tc),
        grid=(n_tiles // 2,),
        in_specs=[cur, cur, nxt, nxt, rows(TOP_K), rows(D), vec, vec, pl.BlockSpec(memory_space=pl.ANY)],
        out_specs=rows(D),
        out_shape=jax.ShapeDtypeStruct((T, D), F32),
        scratch_shapes=[buf, buf, buf, buf, pltpu.SemaphoreType.DMA((2,))],
        compiler_params=_cparams(1),
        name="moe_combine_ln",
    )(pos_rows[0], pos_rows[1], pos_rows[0], pos_rows[1], gates_col, h1,
      ln_g.reshape(1, -1), ln_b.reshape(1, -1), yg)


def _grouped_moe(h1, h1t, experts, gates, ranks, counts, wg, wu, wd, layer, ln_g, ln_b, tmf=512):
    T = h1.shape[0]
    nb = (T * TOP_K) // tmf + N_EXPERTS
    cnt = counts[:, 0].astype(jnp.int32)
    pcnt = (cnt + tmf - 1) // tmf * tmf
    pends = jnp.cumsum(pcnt)
    pstart = pends - pcnt
    eids = jnp.arange(N_EXPERTS, dtype=jnp.int32)
    seg = jnp.sum(jnp.where(experts[None] == eids[:, None, None], pstart[:, None, None], 0), axis=0)
    pos_rows = (seg + ranks) * TOKEN_TILE_ROWS
    blk_start = jnp.arange(nb, dtype=jnp.int32) * tmf
    blk_e = jnp.minimum(jnp.sum((pends[None, :] <= blk_start[:, None]).astype(jnp.int32), axis=1), N_EXPERTS - 1)
    n_valid = (pends[-1:] // tmf).astype(jnp.int32)
    hole_start = jnp.concatenate([pstart + cnt, pends[-1:]])
    hole_rows = jnp.concatenate([pcnt - cnt, nb * tmf - pends[-1:]])
    xg = _dispatch(h1t, pos_rows, hole_start * TOKEN_TILE_ROWS, hole_rows, nb * tmf)
    yg = _expert_ffn(xg, blk_e, n_valid, cnt > 0, wg, wu, wd, layer, tmf)
    return _combine(yg, pos_rows, gates.T, h1, ln_g, ln_b)


def _rope_tables(S):
    half = HEAD_DIM // 2
    inv = ROPE_THETA ** (-jnp.arange(half, dtype=F32) / half)
    ang = jnp.arange(S, dtype=F32)[:, None] * inv[None, :]
    cos, sin = jnp.cos(ang), jnp.sin(ang)
    reps = LANES // HEAD_DIM
    cos_tab = jnp.tile(jnp.concatenate([cos, cos], axis=1), (1, reps))
    sin_tab = jnp.tile(jnp.concatenate([-sin, sin], axis=1), (1, reps))
    return cos_tab, sin_tab


def _pad_w_in(w):
    pad = jnp.zeros((D_MODEL, D_PROJ_PAD - w.shape[1]), w.dtype)
    return jnp.concatenate([w, pad], axis=1).astype(MXU_DTYPE)


def kernel(x, w_in, b_forget, g_dil, g_fox, w_out, ln1_g, ln1_b, w_router, b_router,
           w_gate, w_up, w_down, ln2_g, ln2_b):
    B, S, D = x.shape
    T = B * S
    cos_tab, sin_tab = _rope_tables(S)
    wr_hi, wr_mid, _ = _split3(w_router.astype(F32).T)
    wr3 = jnp.concatenate([wr_hi, wr_mid], axis=0)
    h = x
    for l in range(DEPTH):
        qd, kd, vd, qf, kf, vf, zf = _in_projection(h, _pad_w_in(w_in[l]), cos_tab, sin_tab)
        c = _forget_cumsum(zf, b_forget[l])[:, :N_HEADS_FOX]
        o_fox = _fox_attention(qf, kf, vf, c)
        o_dil = _dilated_attention(qd, kd, vd)
        h1, h1t, experts, gates, ranks, counts = _out_projection(
            o_dil.reshape(T, D_DIL), o_fox.reshape(T, D_FOX), h.reshape(T, D),
            w_out[l].astype(MXU_DTYPE), g_dil[l], g_fox[l], ln1_g[l], ln1_b[l], wr3, b_router)
        h2 = _grouped_moe(h1, h1t, experts, gates, ranks, counts,
                          w_gate, w_up, w_down, l,
                          ln2_g[l], ln2_b[l])
        h = h2.reshape(B, S, D)
    return h
```
